```python
import math
import jax
import jax.numpy as jnp
from jax import lax
import numpy as np

D_MODEL = 1024
BATCH = 4
SEQ = 8192
DEPTH = 1
DEC_BATCH = 32
DEC_SEQ = 1
PAST_LEN = 16384
PAGE_SIZE = 128

N_HEADS = 8
KV_HEADS = 2
HEADS_PER_GROUP = N_HEADS // KV_HEADS
HEAD_DIM = 64
Q_DIM = N_HEADS * HEAD_DIM
KV_DIM = KV_HEADS * HEAD_DIM
CMP_LEN = 32
CMP_STRIDE = 16
CMP_HID = 2 * HEAD_DIM
SEL_LEN = 64
N_SEL_BLOCKS = 16
WINDOW = 512
Q_BLOCK = 128
GM_GROUPS = 4
GM_CHUNK = 128
GM_DIM = D_MODEL // 2
GM_GROUP_DIM = GM_DIM // GM_GROUPS
D_FF = -(-8 * D_MODEL // (3 * 256)) * 256
PLE_DIM = 256
N_BUCKETS = 32
MAX_DISTANCE = 128
EPS = 1e-6
NEG_INF = -1e30
FORCE_SCORE = 1e6
IN_DIM = Q_DIM + 6 * KV_DIM + 3 * N_HEADS + 2 * GM_DIM + 2 * D_MODEL

kernel_name = 'nsa_gmlp_parallel_hybrid_step'


def rms_norm(x, g):
    xf = x.astype(jnp.float32)
    y = xf * lax.rsqrt(jnp.mean(xf * xf, axis=-1, keepdims=True) + EPS)
    return (y * g.astype(jnp.float32)).astype(x.dtype)


def layer_norm(x, g, b):
    xf = x.astype(jnp.float32)
    mu = jnp.mean(xf, axis=-1, keepdims=True)
    var = jnp.mean(jnp.square(xf - mu), axis=-1, keepdims=True)
    y = (xf - mu) * lax.rsqrt(var + EPS)
    return (y * g.astype(jnp.float32) + b.astype(jnp.float32)).astype(x.dtype)


def t5_bucket(dist):
    dist = jnp.maximum(dist, 0)
    max_exact = N_BUCKETS // 2
    ratio = jnp.log(jnp.maximum(dist, 1).astype(jnp.float32) / max_exact) / math.log(MAX_DISTANCE / max_exact)
    large = jnp.minimum(max_exact + (ratio * (N_BUCKETS - max_exact)).astype(jnp.int32), N_BUCKETS - 1)
    return jnp.where(dist < max_exact, dist, large)


def rel_bias_heads(rel_bias, dist):
    b = rel_bias[t5_bucket(dist)].astype(jnp.float32)
    return b.reshape(dist.shape + (KV_HEADS, HEADS_PER_GROUP)).transpose(2, 3, 0, 1)


def masked_softmax(s, mask):
    p = jax.nn.softmax(jnp.where(mask, s, NEG_INF), axis=-1)
    return jnp.where(mask, p, 0.0)


def compress_rows(rows, pe, w1, w2):
    bsz, t = rows.shape[:2]
    n_sub = CMP_LEN // CMP_STRIDE
    nc = (t - CMP_LEN) // CMP_STRIDE + 1
    ns = nc + n_sub - 1
    sub = rows[:, :ns * CMP_STRIDE].reshape(bsz, ns, CMP_STRIDE, KV_HEADS, HEAD_DIM)
    sub = sub.transpose(0, 1, 3, 2, 4).reshape(bsz, ns, KV_HEADS, CMP_STRIDE * HEAD_DIM)
    w1s = w1.reshape(n_sub, CMP_STRIDE * HEAD_DIM, CMP_HID)
    pre = pe.reshape(-1) @ w1
    for j in range(n_sub):
        pre = pre + sub[:, j:j + nc] @ w1s[j]
    return jax.nn.silu(pre) @ w2


def compress_kv(rows, comp_w):
    pe_k, w1_k, w2_k, pe_v, w1_v, w2_v = comp_w
    kc = compress_rows(rows[:, :, 0], pe_k, w1_k, w2_k)
    vc = compress_rows(rows[:, :, 1], pe_v, w1_v, w2_v)
    cend = jnp.arange(kc.shape[1]) * CMP_STRIDE + CMP_LEN - 1
    return kc, vc, cend


def selection_blocks(rows):
    bsz, t = rows.shape[:2]
    nsel = -(-t // SEL_LEN)
    rows = jnp.pad(rows, ((0, 0), (0, nsel * SEL_LEN - t), (0, 0), (0, 0), (0, 0)))
    blk = rows.reshape(bsz, nsel, SEL_LEN, 2, KV_HEADS, HEAD_DIM).transpose(3, 0, 4, 1, 2, 5)
    return blk[0], blk[1]


def nsa_attend(q, qpos, kc, vc, cend, ks_blk, vs_blk, kw, vw, wpos, gate, rel_bias):
    bsz, nq = q.shape[:2]
    nsel = ks_blk.shape[2]
    scale = HEAD_DIM ** -0.5
    s_c = jnp.einsum('bqghd,bngd->bghqn', q, kc).astype(jnp.float32) * scale
    s_c = s_c + rel_bias_heads(rel_bias, qpos[:, None] - cend[None, :])
    p_c = masked_softmax(s_c, cend[None, :] <= qpos[:, None])
    o_c = jnp.einsum('bghqn,bngd->bqghd', p_c.astype(vc.dtype), vc)
    ci = jnp.arange(kc.shape[1])[:, None] * CMP_STRIDE
    sj = jnp.arange(nsel)[None, :] * SEL_LEN
    overlap = ((ci <= sj + SEL_LEN - 1) & (ci + CMP_LEN - 1 >= sj)).astype(jnp.float32)
    imp = jnp.einsum('bghqn,nj->bgqj', p_c, overlap)
    blk = jnp.arange(nsel)[None, :]
    qblk = (qpos // SEL_LEN)[:, None]
    forced = (blk == 0) | (blk == qblk) | (blk == qblk - 1)
    score = jnp.where(forced, FORCE_SCORE, jnp.where(blk <= qblk, imp, -FORCE_SCORE))
    _, idx = lax.top_k(score, min(N_SEL_BLOCKS, nsel))
    bi = jnp.arange(bsz)[:, None, None, None]
    gi = jnp.arange(KV_HEADS)[None, :, None, None]
    k_sel = ks_blk[bi, gi, idx]
    v_sel = vs_blk[bi, gi, idx]
    tpos = idx[..., None] * SEL_LEN + jnp.arange(SEL_LEN)
    qp = qpos[None, None, :, None, None]
    tbl = rel_bias.reshape(N_BUCKETS, KV_HEADS, HEADS_PER_GROUP)
    b_s = tbl[t5_bucket(qp - tpos), gi[..., None]].astype(jnp.float32)
    s_s = jnp.einsum('bqghd,bgqkld->bghqkl', q, k_sel).astype(jnp.float32) * scale
    s_s = s_s + b_s.transpose(0, 1, 5, 2, 3, 4)
    nk = idx.shape[-1] * SEL_LEN
    p_s = masked_softmax(s_s.reshape(bsz, KV_HEADS, HEADS_PER_GROUP, nq, nk),
                         (tpos <= qp).reshape(bsz, KV_HEADS, 1, nq, nk))
    o_s = jnp.einsum('bghqm,bgqmd->bqghd', p_s.astype(v_sel.dtype),
                     v_sel.reshape(bsz, KV_HEADS, nq, nk, HEAD_DIM))
    dw = qpos[:, None] - wpos[None, :]
    s_w = jnp.einsum('bqghd,bkgd->bghqk', q, kw).astype(jnp.float32) * scale
    s_w = s_w + rel_bias_heads(rel_bias, dw)
    p_w = masked_softmax(s_w, (dw >= 0) & (dw <= WINDOW) & (wpos[None, :] >= 0))
    o_w = jnp.einsum('bghqk,bkgd->bqghd', p_w.astype(vw.dtype), vw)
    g = jax.nn.sigmoid(gate.astype(jnp.float32))
    o = g[..., 0:1] * o_c + g[..., 1:2] * o_s + g[..., 2:3] * o_w
    return o.astype(q.dtype).reshape(bsz, nq, Q_DIM)


def nsa_prompt(q, gate, cmp, slc, win, comp_w, rel_bias):
    bsz, seq = q.shape[:2]
    kc, vc, cend = compress_kv(cmp, comp_w)
    ks_blk, vs_blk = selection_blocks(slc)
    win_pad = jnp.pad(win, ((0, 0), (WINDOW, 0), (0, 0), (0, 0), (0, 0)))
    n_qb = seq // Q_BLOCK
    q_b = q.reshape((bsz, n_qb, Q_BLOCK) + q.shape[2:]).swapaxes(0, 1)
    g_b = gate.reshape((bsz, n_qb, Q_BLOCK) + gate.shape[2:]).swapaxes(0, 1)

    def one_block(args):
        q_blk, g_blk, b = args
        start = b * Q_BLOCK
        qpos = start + jnp.arange(Q_BLOCK)
        kv_w = lax.dynamic_slice_in_dim(win_pad, start, WINDOW + Q_BLOCK, axis=1)
        wpos = start - WINDOW + jnp.arange(WINDOW + Q_BLOCK)
        return nsa_attend(q_blk, qpos, kc, vc, cend, ks_blk, vs_blk,
                          kv_w[:, :, 0], kv_w[:, :, 1], wpos, g_blk, rel_bias)

    o = lax.map(one_block, (q_b, g_b, jnp.arange(n_qb)))
    o = o.swapaxes(0, 1).reshape(bsz, seq, Q_DIM)
    return o, win[:, seq - min(WINDOW, seq):]


def nsa_sample(q, gate, cmp_new, slc_new, win_new, cache_cmp, cache_slc, win_buf, page_table, comp_w, rel_bias):
    bsz, nq = q.shape[:2]
    past_len = page_table.shape[1] * PAGE_SIZE

    def gather_past(cache):
        return cache[page_table].reshape(bsz, past_len, 2, KV_HEADS, HEAD_DIM)

    cmp_all = jnp.concatenate([gather_past(cache_cmp), cmp_new], axis=1)
    slc_all = jnp.concatenate([gather_past(cache_slc), slc_new], axis=1)
    win_all = jnp.concatenate([win_buf, win_new], axis=1)
    w_buf = win_buf.shape[1]
    qpos = past_len + jnp.arange(nq)
    wpos = past_len - w_buf + jnp.arange(w_buf + nq)
    kc, vc, cend = compress_kv(cmp_all, comp_w)
    ks_blk, vs_blk = selection_blocks(slc_all)
    o = nsa_attend(q, qpos, kc, vc, cend, ks_blk, vs_blk,
                   win_all[:, :, 0], win_all[:, :, 1], wpos, gate, rel_bias)
    return o, win_all[:, nq:]


def chunk_spatial_mix(v, w_s, b_s):
    bsz, t = v.shape[:2]
    nch = -(-t // GM_CHUNK)
    vp = jnp.pad(v, ((0, 0), (0, nch * GM_CHUNK - t), (0, 0)))
    vp = vp.reshape(bsz, nch, GM_CHUNK, GM_GROUPS, GM_GROUP_DIM)
    w = w_s * jnp.tril(jnp.ones((GM_CHUNK, GM_CHUNK), w_s.dtype))
    s = jnp.einsum('gts,bcsgd->bctgd', w, vp) + b_s.T[None, None, :, :, None]
    return s.reshape(bsz, nch * GM_CHUNK, GM_DIM)[:, :t]


def trunk_layer(x, ple, attend, ln_mix_pre, w_in, gm_ln_g, gm_ln_b, gm_ws, gm_bs,
                w_branch_nsa, w_branch_gm, w_out, ln_mix_post, ln_ffn_pre, w_ffn_gate,
                w_ffn_up, w_ffn_down, ln_ffn_post, w_ple, w_ple_gate, ln_ple_post):
    bsz, t, _ = x.shape
    h = rms_norm(x, ln_mix_pre)
    z = h @ w_in
    sizes = [Q_DIM, 2 * KV_DIM, 2 * KV_DIM, 2 * KV_DIM, 3 * N_HEADS, 2 * GM_DIM, 2 * D_MODEL]
    q, cmp, slc, win, gate, uv, mg = jnp.split(z, np.cumsum(sizes)[:-1].tolist(), axis=-1)
    q = q.reshape(bsz, t, KV_HEADS, HEADS_PER_GROUP, HEAD_DIM)
    gate = gate.reshape(bsz, t, KV_HEADS, HEADS_PER_GROUP, 3)
    cmp = cmp.reshape(bsz, t, 2, KV_HEADS, HEAD_DIM)
    slc = slc.reshape(bsz, t, 2, KV_HEADS, HEAD_DIM)
    win = win.reshape(bsz, t, 2, KV_HEADS, HEAD_DIM)
    o_nsa, win_state = attend(q, gate, cmp, slc, win)
    u, v = jnp.split(jax.nn.gelu(uv), 2, axis=-1)
    v = layer_norm(v, gm_ln_g, gm_ln_b)
    o_gm = u * chunk_spatial_mix(v, gm_ws, gm_bs)
    g_a, g_b = jnp.split(jax.nn.sigmoid(mg), 2, axis=-1)
    mixed = (g_a * (o_nsa @ w_branch_nsa) + g_b * (o_gm @ w_branch_gm)) @ w_out
    x = x + rms_norm(mixed, ln_mix_post)
    h2 = rms_norm(x, ln_ffn_pre)
    f = (jax.nn.silu(h2 @ w_ffn_gate) * (h2 @ w_ffn_up)) @ w_ffn_down
    x = x + rms_norm(f, ln_ffn_post)
    e = ple @ w_ple
    x = x + rms_norm(jax.nn.sigmoid(x @ w_ple_gate) * e, ln_ple_post)
    return x, cmp, slc, win_state, v


def setup_inputs(seed: int = 0) -> dict:
    key = jax.random.key(seed)
    keys = iter(jax.random.split(key, 48))

    def nrm(shape, scale):
        return jax.random.normal(next(keys), shape, jnp.float32) * scale

    def gain(shape):
        return 1.0 + nrm(shape, 0.05)

    n_pages = PAST_LEN // PAGE_SIZE
    n_pool = (DEC_BATCH * n_pages * 5) // 4
    w_buf = min(WINDOW, PAST_LEN)
    x_prompt = nrm((BATCH, SEQ, D_MODEL), 1.0)
    x_sample = nrm((DEC_BATCH, DEC_SEQ, D_MODEL), 1.0)
    cache_cmp_kv = nrm((DEPTH, n_pool, PAGE_SIZE, 2, KV_HEADS, HEAD_DIM), 1.0)
    cache_slc_kv = nrm((DEPTH, n_pool, PAGE_SIZE, 2, KV_HEADS, HEAD_DIM), 1.0)
    state_win_kv = nrm((DEPTH, DEC_BATCH, w_buf, 2, KV_HEADS, HEAD_DIM), 1.0)
    perm = jax.random.permutation(next(keys), n_pool)
    page_table = perm[:DEC_BATCH * n_pages].reshape(DEC_BATCH, n_pages).astype(jnp.int32)
    p_prompt = nrm((DEPTH, BATCH, SEQ, PLE_DIM), 1.0)
    p_sample = nrm((DEPTH, DEC_BATCH, DEC_SEQ, PLE_DIM), 1.0)
    return {
        'x_prompt': x_prompt,
        'x_sample': x_sample,
        'cache_cmp_kv': cache_cmp_kv,
        'cache_slc_kv': cache_slc_kv,
        'state_win_kv': state_win_kv,
        'page_table': page_table,
        'p_prompt': p_prompt,
        'p_sample': p_sample,
        'rel_bias': nrm((N_BUCKETS, N_HEADS), 0.5),
        'ln_mix_pre': gain((DEPTH, D_MODEL)),
        'w_in': nrm((DEPTH, D_MODEL, IN_DIM), D_MODEL ** -0.5),
        'cmp_pe_k': nrm((DEPTH, CMP_LEN, HEAD_DIM), 0.1),
        'cmp_w1_k': nrm((DEPTH, CMP_LEN * HEAD_DIM, CMP_HID), (CMP_LEN * HEAD_DIM) ** -0.5),
        'cmp_w2_k': nrm((DEPTH, CMP_HID, HEAD_DIM), CMP_HID ** -0.5),
        'cmp_pe_v': nrm((DEPTH, CMP_LEN, HEAD_DIM), 0.1),
        'cmp_w1_v': nrm((DEPTH, CMP_LEN * HEAD_DIM, CMP_HID), (CMP_LEN * HEAD_DIM) ** -0.5),
        'cmp_w2_v': nrm((DEPTH, CMP_HID, HEAD_DIM), CMP_HID ** -0.5),
        'gm_ln_g': gain((DEPTH, GM_DIM)),
        'gm_ln_b': nrm((DEPTH, GM_DIM), 0.02),
        'gm_ws': nrm((DEPTH, GM_GROUPS, GM_CHUNK, GM_CHUNK), GM_CHUNK ** -0.5),
        'gm_bs': gain((DEPTH, GM_GROUPS, GM_CHUNK)),
        'w_branch_nsa': nrm((DEPTH, Q_DIM, D_MODEL), Q_DIM ** -0.5),
        'w_branch_gm': nrm((DEPTH, GM_DIM, D_MODEL), GM_DIM ** -0.5),
        'w_out': nrm((DEPTH, D_MODEL, D_MODEL), D_MODEL ** -0.5),
        'ln_mix_post': gain((DEPTH, D_MODEL)),
        'ln_ffn_pre': gain((DEPTH, D_MODEL)),
        'w_ffn_gate': nrm((DEPTH, D_MODEL, D_FF), D_MODEL ** -0.5),
        'w_ffn_up': nrm((DEPTH, D_MODEL, D_FF), D_MODEL ** -0.5),
        'w_ffn_down': nrm((DEPTH, D_FF, D_MODEL), D_FF ** -0.5),
        'ln_ffn_post': gain((DEPTH, D_MODEL)),
        'w_ple': nrm((DEPTH, PLE_DIM, D_MODEL), PLE_DIM ** -0.5),
        'w_ple_gate': nrm((DEPTH, D_MODEL, D_MODEL), D_MODEL ** -0.5),
        'ln_ple_post': gain((DEPTH, D_MODEL)),
    }


def reference(x_prompt, x_sample, cache_cmp_kv, cache_slc_kv, state_win_kv, page_table,
              p_prompt, p_sample, rel_bias, ln_mix_pre, w_in, cmp_pe_k, cmp_w1_k, cmp_w2_k,
              cmp_pe_v, cmp_w1_v, cmp_w2_v, gm_ln_g, gm_ln_b, gm_ws, gm_bs, w_branch_nsa,
              w_branch_gm, w_out, ln_mix_post, ln_ffn_pre, w_ffn_gate, w_ffn_up, w_ffn_down,
              ln_ffn_post, w_ple, w_ple_gate, ln_ple_post):
    xp = x_prompt
    xs = x_sample
    pc, ps, pw, sc, ss, sw, sv = [], [], [], [], [], [], []
    for i in range(DEPTH):
        comp_w = (cmp_pe_k[i], cmp_w1_k[i], cmp_w2_k[i], cmp_pe_v[i], cmp_w1_v[i], cmp_w2_v[i])
        layer_w = (ln_mix_pre[i], w_in[i], gm_ln_g[i], gm_ln_b[i], gm_ws[i], gm_bs[i],
                   w_branch_nsa[i], w_branch_gm[i], w_out[i], ln_mix_post[i], ln_ffn_pre[i],
                   w_ffn_gate[i], w_ffn_up[i], w_ffn_down[i], ln_ffn_post[i], w_ple[i],
                   w_ple_gate[i], ln_ple_post[i])

        def attend_prompt(q, gate, cmp, slc, win, comp_w=comp_w):
            return nsa_prompt(q, gate, cmp, slc, win, comp_w, rel_bias)

        def attend_sample(q, gate, cmp, slc, win, comp_w=comp_w, i=i):
            return nsa_sample(q, gate, cmp, slc, win, cache_cmp_kv[i], cache_slc_kv[i],
                              state_win_kv[i], page_table, comp_w, rel_bias)

        xp, c_p, s_p, w_p, _ = trunk_layer(xp, p_prompt[i], attend_prompt, *layer_w)
        xs, c_s, s_s, w_s, v_s = trunk_layer(xs, p_sample[i], attend_sample, *layer_w)
        pc.append(c_p)
        ps.append(s_p)
        pw.append(w_p)
        sc.append(c_s)
        ss.append(s_s)
        sw.append(w_s)
        sv.append(v_s)
    return (xp, xs, jnp.stack(pc), jnp.stack(ps), jnp.stack(pw), jnp.stack(sc), jnp.stack(ss), jnp.stack(sw), jnp.stack(sv))
```

```python
import functools
import math

import jax
import jax.numpy as jnp
import numpy as np
from jax import lax
from jax.experimental import pallas as pl
from jax.experimental.pallas import tpu as pltpu

F32 = jnp.float32
BF16 = jnp.bfloat16

D_MODEL = 1024
N_HEADS = 8
KV_HEADS = 2
HPG = N_HEADS // KV_HEADS
HEAD_DIM = 64
CMP_LEN = 32
CMP_STRIDE = 16
CMP_HID = 2 * HEAD_DIM
SEL_LEN = 64
N_SEL_BLOCKS = 16
WINDOW = 512
Q_BLOCK = 128
GM_GROUPS = 4
GM_CHUNK = 128
GM_DIM = D_MODEL // 2
D_FF = -(-8 * D_MODEL // (3 * 256)) * 256
PLE_DIM = 256
N_BUCKETS = 32
MAX_DISTANCE = 128
EPS = 1e-6
NEG = -1e30
FORCE = 1e6
PAGE = 128

LANE = 128
NJ = 128
SEL_TILE = 256
FF_CHUNK = 256
VMEM_LIMIT = 56 * 1024 * 1024


def _dot(a, b):
    return jnp.dot(a, b, preferred_element_type=F32)


def _dot_nt(a, b):
    return lax.dot_general(a, b, (((1,), (1,)), ((), ())), preferred_element_type=F32)


def _dot_f32(a, b):
    return jnp.dot(a, b, preferred_element_type=F32, precision=lax.Precision.HIGHEST)


def _rms(x, g):
    return x * lax.rsqrt(jnp.mean(x * x, axis=-1, keepdims=True) + EPS) * g


def _params(sem):
    return pltpu.CompilerParams(dimension_semantics=sem, vmem_limit_bytes=VMEM_LIMIT)


def _t5_bucket_np(d):
    d = np.maximum(d, 0)
    max_exact = N_BUCKETS // 2
    ratio = (np.log(np.maximum(d, 1).astype(np.float32) / np.float32(max_exact))
             / np.float32(math.log(MAX_DISTANCE / max_exact)))
    large = np.minimum(max_exact + (ratio * np.float32(N_BUCKETS - max_exact)).astype(np.int32), N_BUCKETS - 1)
    return np.where(d < max_exact, d, large)


_BUCKET = _t5_bucket_np(np.arange(MAX_DISTANCE + 1))
assert _BUCKET[MAX_DISTANCE] == N_BUCKETS - 1


def _shifted_bias(rel_bias):
    return rel_bias[_BUCKET] - rel_bias[N_BUCKETS - 1][None, :]


def _prompt_table(sh, dist, valid):
    t = sh[np.minimum(np.maximum(dist, 0), MAX_DISTANCE)]
    t = jnp.where(valid[..., None], t, NEG)
    t = t.reshape(dist.shape[0], Q_BLOCK, KV_HEADS, HPG).transpose(2, 0, 3, 1)
    return t.reshape(KV_HEADS, dist.shape[0], HPG * Q_BLOCK).astype(F32)


def _sample_table(sh, dist, valid):
    t = sh[np.minimum(np.maximum(dist, 0), MAX_DISTANCE)]
    return jnp.where(valid[:, None], t, NEG).T.astype(F32)


def _overlap_t(n_cmp, n_sel_pad):
    ci = np.arange(n_cmp)[None, :] * CMP_STRIDE
    sj = np.arange(n_sel_pad)[:, None] * SEL_LEN
    return ((ci <= sj + SEL_LEN - 1) & (ci + CMP_LEN - 1 >= sj)).astype(np.float32)


def _inproj_kernel(x_ref, lng_ref, wq_ref, wkv_ref, wg_ref, wuv_ref, wmg_ref, glg_ref, glb_ref, *outs,
                   prompt, seq, tm):
    h = _rms(x_ref[...], lng_ref[...]).astype(BF16)
    q = _dot(h, wq_ref[...]) * (HEAD_DIM ** -0.5)
    kv = _dot(h, wkv_ref[...])
    gate = jax.nn.sigmoid(_dot(h, wg_ref[...]))
    uv = jax.nn.gelu(_dot(h, wuv_ref[...]))
    u = uv[:, :GM_DIM]
    v = uv[:, GM_DIM:]
    mu = jnp.mean(v, axis=-1, keepdims=True)
    var = jnp.mean(jnp.square(v - mu), axis=-1, keepdims=True)
    vn = (v - mu) * lax.rsqrt(var + EPS) * glg_ref[...] + glb_ref[...]
    mg = jax.nn.sigmoid(_dot(h, wmg_ref[...]))
    cmp, slc, win = kv[:, 0:256], kv[:, 256:512], kv[:, 512:768]
    if prompt:
        (q_ref, cmp_ref, slc_ref, win_ref, kaug_ref, vts_ref, kwin_ref, vtw_ref, gt_ref,
         u_ref, vn_ref, ga_ref, gb_ref) = outs
        for hh in range(N_HEADS):
            q_ref[hh] = q[:, hh * LANE:(hh + 1) * LANE].astype(BF16)
        pos = (pl.program_id(0) * tm + lax.broadcasted_iota(jnp.int32, (tm, NJ), 0)) % seq
        onehot = (lax.broadcasted_iota(jnp.int32, (tm, NJ), 1) == pos // SEL_LEN)
        kaug_ref[:, 0:NJ] = onehot.astype(BF16)
        kaug_ref[:, NJ:NJ + LANE] = slc[:, 0:LANE].astype(BF16)
        kwin_ref[...] = win[:, 0:LANE].astype(BF16)
        slc_t = slc.T
        win_t = win.T
        for g in range(KV_HEADS):
            r0 = LANE + HEAD_DIM * g
            for c in range(tm // SEL_TILE):
                vts_ref[g, c] = slc_t[r0:r0 + HEAD_DIM, c * SEL_TILE:(c + 1) * SEL_TILE].astype(BF16)
            for c in range(tm // LANE):
                vtw_ref[g, c] = win_t[r0:r0 + HEAD_DIM, c * LANE:(c + 1) * LANE].astype(BF16)
        gt_ref[...] = gate.T[0:32, :]
    else:
        q_ref, cmp_ref, slc_ref, win_ref, gate_ref, u_ref, vn_ref, ga_ref, gb_ref = outs
        q_ref[...] = q
        gate_ref[...] = gate
    cmp_ref[...] = cmp
    slc_ref[...] = slc
    win_ref[...] = win
    u_ref[...] = u.astype(u_ref.dtype)
    vn_ref[...] = vn.astype(vn_ref.dtype)
    ga_ref[...] = mg[:, :D_MODEL].astype(ga_ref.dtype)
    gb_ref[...] = mg[:, D_MODEL:].astype(gb_ref.dtype)


def _inproj(x, w, *, prompt, seq, tm):
    n = x.shape[0]
    row = lambda i: (i, 0)
    const = lambda i: (0, 0)
    in_specs = [pl.BlockSpec((tm, D_MODEL), row), pl.BlockSpec((1, D_MODEL), const),
                pl.BlockSpec(w["wq"].shape, const), pl.BlockSpec(w["wkv"].shape, const),
                pl.BlockSpec(w["wg"].shape, const), pl.BlockSpec(w["wuv"].shape, const),
                pl.BlockSpec(w["wmg"].shape, const), pl.BlockSpec((1, GM_DIM), const),
                pl.BlockSpec((1, GM_DIM), const)]
    kv_shapes = [jax.ShapeDtypeStruct((n, 256), F32)] * 3
    kv_specs = [pl.BlockSpec((tm, 256), row)] * 3
    if prompt:
        act = BF16
        out_shape = ([jax.ShapeDtypeStruct((N_HEADS, n, LANE), BF16)] + kv_shapes + [
            jax.ShapeDtypeStruct((n, NJ + LANE), BF16),
            jax.ShapeDtypeStruct((KV_HEADS, n // SEL_TILE, HEAD_DIM, SEL_TILE), BF16),
            jax.ShapeDtypeStruct((n, LANE), BF16),
            jax.ShapeDtypeStruct((KV_HEADS, n // LANE, HEAD_DIM, LANE), BF16),
            jax.ShapeDtypeStruct((32, n), F32)])
        out_specs = ([pl.BlockSpec((N_HEADS, tm, LANE), lambda i: (0, i, 0))] + kv_specs + [
            pl.BlockSpec((tm, NJ + LANE), row),
            pl.BlockSpec((KV_HEADS, tm // SEL_TILE, HEAD_DIM, SEL_TILE), lambda i: (0, i, 0, 0)),
            pl.BlockSpec((tm, LANE), row),
            pl.BlockSpec((KV_HEADS, tm // LANE, HEAD_DIM, LANE), lambda i: (0, i, 0, 0)),
            pl.BlockSpec((32, tm), lambda i: (0, i))])
    else:
        act = F32
        out_shape = ([jax.ShapeDtypeStruct((n, N_HEADS * LANE), F32)] + kv_shapes
                     + [jax.ShapeDtypeStruct((n, LANE), F32)])
        out_specs = [pl.BlockSpec((tm, N_HEADS * LANE), row)] + kv_specs + [pl.BlockSpec((tm, LANE), row)]
    out_shape += [jax.ShapeDtypeStruct((n, GM_DIM), act)] * 2 + [jax.ShapeDtypeStruct((n, D_MODEL), act)] * 2
    out_specs += [pl.BlockSpec((tm, GM_DIM), row)] * 2 + [pl.BlockSpec((tm, D_MODEL), row)] * 2
    return pl.pallas_call(
        functools.partial(_inproj_kernel, prompt=prompt, seq=seq, tm=tm),
        grid=(n // tm,), in_specs=in_specs, out_specs=out_specs, out_shape=out_shape,
        compiler_params=_params(("parallel",)),
        name="inproj_prompt" if prompt else "inproj_sample",
    )(x, w["ln_mix_pre"], w["wq"], w["wkv"], w["wg"], w["wuv"], w["wmg"], w["gm_ln_g"], w["gm_ln_b"])


def _compress_finish(hk, hv, pek_ref, w1k_ref, pev_ref, w1v_ref, w2ka_ref, w2kb_ref, n_sub):
    pwk = _dot(pek_ref[...].astype(BF16), w1k_ref[...])[0:1, :]
    pwv = _dot(pev_ref[...].astype(BF16), w1v_ref[...])[0:1, :]

    def act(hh, g, pw):
        a0 = hh[:, 256 * g:256 * g + CMP_HID]
        a1 = hh[:, 256 * g + CMP_HID:256 * g + 2 * CMP_HID]
        return jax.nn.silu(a0 + pltpu.roll(a1, n_sub - 1, 0) + pw).astype(BF16)

    return [act(hk, 0, pwk), act(hk, 1, pwk)], [act(hv, 0, pwv), act(hv, 1, pwv)]


def _compress_prompt_kernel(cmpk_ref, cmpv_ref, wbk_ref, wbv_ref, pek_ref, w1k_ref, pev_ref, w1v_ref,
                            w2ka_ref, w2kb_ref, w2vt_ref, kc_ref, vct_ref, *, n_sub):
    hk = jnp.zeros((n_sub, 4 * CMP_HID), F32)
    hv = jnp.zeros((n_sub, 4 * CMP_HID), F32)
    for r in range(CMP_STRIDE):
        hk += _dot(cmpk_ref[pl.ds(r, n_sub, stride=CMP_STRIDE), :].astype(BF16), wbk_ref[r])
        hv += _dot(cmpv_ref[pl.ds(r, n_sub, stride=CMP_STRIDE), :].astype(BF16), wbv_ref[r])
    ak, av = _compress_finish(hk, hv, pek_ref, w1k_ref, pev_ref, w1v_ref, w2ka_ref, w2kb_ref, n_sub)
    kc_ref[...] = (_dot(ak[0], w2ka_ref[...]) + _dot(ak[1], w2kb_ref[...])).astype(BF16)
    for g in range(KV_HEADS):
        vct_ref[g] = _dot_nt(w2vt_ref[...], av[g]).astype(BF16)


def _compress_prompt(cmp, w, *, bsz, seq):
    n_sub = seq // CMP_STRIDE
    const2 = lambda b: (0, 0)
    const3 = lambda b: (0, 0, 0)
    return pl.pallas_call(
        functools.partial(_compress_prompt_kernel, n_sub=n_sub),
        grid=(bsz,),
        in_specs=[pl.BlockSpec((seq, LANE), lambda b: (b, 0)), pl.BlockSpec((seq, LANE), lambda b: (b, 1)),
                  pl.BlockSpec(w["wbk"].shape, const3), pl.BlockSpec(w["wbv"].shape, const3),
                  pl.BlockSpec(w["pek"].shape, const2), pl.BlockSpec(w["w1k"].shape, const2),
                  pl.BlockSpec(w["pev"].shape, const2), pl.BlockSpec(w["w1v"].shape, const2),
                  pl.BlockSpec(w["w2ka"].shape, const2), pl.BlockSpec(w["w2kb"].shape, const2),
                  pl.BlockSpec(w["w2vt"].shape, const2)],
        out_specs=[pl.BlockSpec((None, n_sub, LANE), lambda b: (b, 0, 0)),
                   pl.BlockSpec((None, KV_HEADS, HEAD_DIM, n_sub), lambda b: (b, 0, 0, 0))],
        out_shape=[jax.ShapeDtypeStruct((bsz, n_sub, LANE), BF16),
                   jax.ShapeDtypeStruct((bsz, KV_HEADS, HEAD_DIM, n_sub), BF16)],
        compiler_params=_params(("parallel",)),
        name="compress_prompt",
    )(cmp, cmp, w["wbk"], w["wbv"], w["pek"], w["w1k"], w["pev"], w["w1v"], w["w2ka"], w["w2kb"], w["w2vt"])


def _compress_paged_kernel(pt_ref, *refs, pps):
    kpages, vpages = refs[:pps], refs[pps:2 * pps]
    wbk_ref, wbv_ref, h_ref = refs[2 * pps:]
    sub_per_page = PAGE // CMP_STRIDE
    n = pps * sub_per_page
    hk = jnp.zeros((n, 4 * CMP_HID), F32)
    hv = jnp.zeros((n, 4 * CMP_HID), F32)
    for r in range(CMP_STRIDE):
        xk = jnp.concatenate([pg[pl.ds(r, sub_per_page, stride=CMP_STRIDE), :] for pg in kpages], axis=0)
        xv = jnp.concatenate([pg[pl.ds(r, sub_per_page, stride=CMP_STRIDE), :] for pg in vpages], axis=0)
        hk += _dot(xk.astype(BF16), wbk_ref[r])
        hv += _dot(xv.astype(BF16), wbv_ref[r])
    h_ref[:, 0:4 * CMP_HID] = hk
    h_ref[:, 4 * CMP_HID:8 * CMP_HID] = hv


def _compress_paged(cache3, page_table, w, *, pps):
    dbsz, n_pages = page_table.shape
    sub_per_page = PAGE // CMP_STRIDE
    n_sub = n_pages * sub_per_page

    def page_spec(k, c):
        return pl.BlockSpec((None, PAGE, LANE), lambda b, t, pt: (pt[b, t * pps + k], 0, c))

    const3 = lambda b, t, pt: (0, 0, 0)
    grid_spec = pltpu.PrefetchScalarGridSpec(
        num_scalar_prefetch=1, grid=(dbsz, n_pages // pps),
        in_specs=[page_spec(k, c) for c in range(2) for k in range(pps)]
        + [pl.BlockSpec(w["wbk"].shape, const3), pl.BlockSpec(w["wbv"].shape, const3)],
        out_specs=pl.BlockSpec((None, pps * sub_per_page, 8 * CMP_HID), lambda b, t, pt: (b, t, 0)))
    return pl.pallas_call(
        functools.partial(_compress_paged_kernel, pps=pps),
        grid_spec=grid_spec,
        out_shape=jax.ShapeDtypeStruct((dbsz, n_sub, 8 * CMP_HID), F32),
        compiler_params=_params(("parallel", "arbitrary")),
        name="compress_paged",
    )(page_table, *([cache3] * (2 * pps)), w["wbk"], w["wbv"])


def _col_softmax(s):
    m = jnp.max(s, axis=0, keepdims=True)
    p = jnp.exp(s - m)
    return p * jnp.where(m > 0.5 * NEG, 1.0 / jnp.sum(p, axis=0, keepdims=True), 0.0)


def _select_blocks(score, jio, axis):
    selneg = jnp.full(score.shape, NEG, F32)
    picked = []
    for _ in range(N_SEL_BLOCKS):
        mx = jnp.max(score, axis=axis, keepdims=True)
        jm = jnp.min(jnp.where(score == mx, jio, 1 << 20), axis=axis, keepdims=True)
        pick = jio == jm
        selneg = jnp.where(pick, 0.0, selneg)
        score = jnp.where(pick, -3e38, score)
        picked.append(jm)
    return selneg, picked


def _nsa_prompt_kernel(q_ref, kc_ref, vct_ref, kaug_ref, vts_ref, kwin_ref, vtw_ref, gt_ref,
                       tc_ref, ts_ref, tw_ref, ovt_ref, o_ref,
                       sc_ref, qaug_ref, m_ref, l_ref, acc_ref, *, n_cmp):
    g = pl.program_id(1)
    qb = pl.program_id(2)
    nl = HPG * Q_BLOCK
    q2 = q_ref[...].reshape(nl, LANE)

    pad = 2 * (Q_BLOCK // CMP_STRIDE)
    band = 3 * (Q_BLOCK // CMP_STRIDE)
    sc_ref[0:pad, :] = jnp.zeros((pad, nl), F32)
    sc_ref[pad:pad + n_cmp, :] = _dot_nt(kc_ref[...], q2)
    w0 = pl.multiple_of(qb * (Q_BLOCK // CMP_STRIDE), 8)
    sc_ref[pl.ds(w0, band), :] += tc_ref[...]
    sc = sc_ref[pad:pad + n_cmp, :]
    row = lax.broadcasted_iota(jnp.int32, (n_cmp, nl), 0)
    sc = jnp.where(row < (qb + 1) * (Q_BLOCK // CMP_STRIDE), sc, NEG)
    pc = _col_softmax(sc)
    o_c = _dot(vct_ref[...], pc.astype(BF16))

    psum = pc[:, 0:Q_BLOCK]
    for hh in range(1, HPG):
        psum = psum + pc[:, hh * Q_BLOCK:(hh + 1) * Q_BLOCK]
    imp_t = _dot_f32(ovt_ref[...], psum)
    jio = lax.broadcasted_iota(jnp.int32, (NJ, Q_BLOCK), 0)
    qi = lax.broadcasted_iota(jnp.int32, (NJ, Q_BLOCK), 1)
    qblk = qb * (Q_BLOCK // SEL_LEN) + qi // SEL_LEN
    forced = (jio == 0) | (jio == qblk) | (jio == qblk - 1)
    score = jnp.where(forced, FORCE, jnp.where(jio <= qblk, imp_t, -FORCE))
    selneg_t, _ = _select_blocks(score, jio, 0)
    selneg = selneg_t.T.astype(BF16)
    for hh in range(HPG):
        qaug_ref[hh * Q_BLOCK:(hh + 1) * Q_BLOCK, 0:NJ] = selneg
    qaug_ref[:, NJ:NJ + LANE] = q2

    m_ref[...] = jnp.full((1, nl), -1e38, F32)
    l_ref[...] = jnp.zeros((1, nl), F32)
    acc_ref[...] = jnp.zeros((HEAD_DIM, nl), F32)

    def tile(u, table):
        ka = kaug_ref[pl.ds(pl.multiple_of(u * SEL_TILE, SEL_TILE), SEL_TILE), :]
        s = _dot_nt(ka, qaug_ref[...])
        if table is not None:
            s = s + table
        m_old = m_ref[...]
        m_new = jnp.maximum(m_old, jnp.max(s, axis=0, keepdims=True))
        alpha = jnp.exp(m_old - m_new)
        p = jnp.exp(s - m_new)
        l_ref[...] = alpha * l_ref[...] + jnp.sum(p, axis=0, keepdims=True)
        acc_ref[...] = alpha * acc_ref[...] + _dot(vts_ref[u], p.astype(BF16))
        m_ref[...] = m_new

    tiles_per_q = SEL_TILE // Q_BLOCK
    u_last = qb // tiles_per_q
    par = qb % tiles_per_q

    def far(u, carry):
        tile(u, None)
        return carry

    lax.fori_loop(0, jnp.maximum(u_last - 1, 0), far, 0)

    @pl.when(u_last >= 1)
    def _():
        tile(u_last - 1, ts_ref[par, 0:SEL_TILE, :])

    tile(u_last, ts_ref[par, SEL_TILE:2 * SEL_TILE, :])
    o_s = acc_ref[...] / l_ref[...]

    n_wt = WINDOW // Q_BLOCK + 1
    tix = [jnp.maximum(qb - (n_wt - 1) + t, 0) for t in range(n_wt)]
    kw = jnp.concatenate([kwin_ref[pl.ds(pl.multiple_of(t * Q_BLOCK, Q_BLOCK), Q_BLOCK), :] for t in tix], axis=0)
    sw = _dot_nt(kw, q2) + tw_ref[...]
    wrow = lax.broadcasted_iota(jnp.int32, (n_wt * Q_BLOCK, nl), 0) + (qb - (n_wt - 1)) * Q_BLOCK
    pw = _col_softmax(jnp.where(wrow >= 0, sw, NEG))
    vw = jnp.concatenate([vtw_ref[t] for t in tix], axis=1)
    o_w = _dot(vw, pw.astype(BF16))

    def gate_row(br):
        rows = gt_ref[br * N_HEADS:(br + 1) * N_HEADS, :]
        rows = jnp.where(g == 0, rows[0:HPG], rows[HPG:2 * HPG])
        return jnp.concatenate([rows[hh:hh + 1, :] for hh in range(HPG)], axis=1)

    o_t = gate_row(0) * o_c + gate_row(1) * o_s + gate_row(2) * o_w
    stack = jnp.concatenate([o_t[:, hh * Q_BLOCK:(hh + 1) * Q_BLOCK] for hh in range(HPG)], axis=0)
    o_ref[...] = stack.T.astype(o_ref.dtype)


def _nsa_prompt(qhm, kc, vct, kaug, vts, kwin, vtw, gt, tabs, *, bsz, seq):
    n = bsz * seq
    nqb = seq // Q_BLOCK
    n_cmp = seq // CMP_STRIDE
    nl = HPG * Q_BLOCK
    tc, ts, tw, ovt = tabs
    in_specs = [
        pl.BlockSpec((HPG, Q_BLOCK, LANE), lambda b, g, i: (g, b * nqb + i, 0)),
        pl.BlockSpec((None, n_cmp, LANE), lambda b, g, i: (b, 0, 0)),
        pl.BlockSpec((None, None, HEAD_DIM, n_cmp), lambda b, g, i: (b, g, 0, 0)),
        pl.BlockSpec((seq, NJ + LANE), lambda b, g, i: (b, 0)),
        pl.BlockSpec((None, seq // SEL_TILE, HEAD_DIM, SEL_TILE), lambda b, g, i: (g, b, 0, 0)),
        pl.BlockSpec((seq, LANE), lambda b, g, i: (b, 0)),
        pl.BlockSpec((None, seq // LANE, HEAD_DIM, LANE), lambda b, g, i: (g, b, 0, 0)),
        pl.BlockSpec((32, Q_BLOCK), lambda b, g, i: (0, b * nqb + i)),
        pl.BlockSpec((None,) + tc.shape[1:], lambda b, g, i: (g, 0, 0)),
        pl.BlockSpec((None,) + ts.shape[1:], lambda b, g, i: (g, 0, 0, 0)),
        pl.BlockSpec((None,) + tw.shape[1:], lambda b, g, i: (g, 0, 0)),
        pl.BlockSpec(ovt.shape, lambda b, g, i: (0, 0)),
    ]
    pad = 2 * (Q_BLOCK // CMP_STRIDE)
    return pl.pallas_call(
        functools.partial(_nsa_prompt_kernel, n_cmp=n_cmp),
        grid=(bsz, KV_HEADS, nqb),
        in_specs=in_specs,
        out_specs=pl.BlockSpec((Q_BLOCK, HPG * HEAD_DIM), lambda b, g, i: (b * nqb + i, g)),
        out_shape=jax.ShapeDtypeStruct((n, N_HEADS * HEAD_DIM), BF16),
        scratch_shapes=[pltpu.VMEM((pad + n_cmp + pad, nl), F32), pltpu.VMEM((nl, NJ + LANE), BF16),
                        pltpu.VMEM((1, nl), F32), pltpu.VMEM((1, nl), F32), pltpu.VMEM((HEAD_DIM, nl), F32)],
        compiler_params=_params(("parallel", "parallel", "arbitrary")),
        name="nsa_prompt",
    )(qhm, kc, vct, kaug, vts, kwin, vtw, gt, tc, ts, tw, ovt)


def _row_softmax_parts(s, s_new):
    m = jnp.maximum(jnp.max(s, axis=1, keepdims=True), s_new)
    p = jnp.exp(s - m)
    pn = jnp.exp(s_new - m)
    return p, pn, jnp.sum(p, axis=1, keepdims=True) + pn


def _nsa_sample_cmp_kernel(h_ref, pek_ref, w1k_ref, pev_ref, w1v_ref, w2ka_ref, w2kb_ref, w2va_ref, w2vb_ref,
                           q_ref, tcs_ref, ovs_ref, oc_ref, idx_ref, *, n_sub, qblk):
    hh = h_ref[...]
    ak, av = _compress_finish(hh[:, 0:4 * CMP_HID], hh[:, 4 * CMP_HID:8 * CMP_HID],
                              pek_ref, w1k_ref, pev_ref, w1v_ref, w2ka_ref, w2kb_ref, n_sub)
    kc = (_dot(ak[0], w2ka_ref[...]) + _dot(ak[1], w2kb_ref[...])).astype(BF16)
    vc = (_dot(av[0], w2va_ref[...]) + _dot(av[1], w2vb_ref[...])).astype(BF16)
    q8 = q_ref[...].astype(BF16)
    s = _dot_nt(q8, kc) + tcs_ref[...]
    m = jnp.max(s, axis=1, keepdims=True)
    p = jnp.exp(s - m)
    pc = p / jnp.sum(p, axis=1, keepdims=True)
    oc_ref[...] = _dot(pc.astype(BF16), vc)
    hrow = lax.broadcasted_iota(jnp.int32, (N_HEADS, n_sub), 0)
    rows = []
    for g in range(KV_HEADS):
        rows.append(jnp.sum(jnp.where(hrow // HPG == g, pc, 0.0), axis=0, keepdims=True))
    psum = jnp.concatenate(rows + [jnp.zeros((N_HEADS - KV_HEADS, n_sub), F32)], axis=0)
    imp = _dot_f32(psum, ovs_ref[...])
    jl = lax.broadcasted_iota(jnp.int32, imp.shape, 1)
    forced = (jl == 0) | (jl == qblk) | (jl == qblk - 1)
    score = jnp.where(forced, FORCE, jnp.where(jl <= qblk, imp, -FORCE))
    _, picked = _select_blocks(score, jl, 1)
    lane = lax.broadcasted_iota(jnp.int32, (N_HEADS, LANE), 1)
    out = jnp.zeros((N_HEADS, LANE), jnp.int32)
    for r, jm in enumerate(picked):
        out = jnp.where(lane == r, jm, out)
    idx_ref[...] = out


def _nsa_sample_cmp(hbuf, q8, w, tcs, ovs, *, qblk):
    dbsz, n_sub, _ = hbuf.shape
    c2 = lambda b: (0, 0)
    names = ["pek", "w1k", "pev", "w1v", "w2ka", "w2kb", "w2va", "w2vb"]
    return pl.pallas_call(
        functools.partial(_nsa_sample_cmp_kernel, n_sub=n_sub, qblk=qblk),
        grid=(dbsz,),
        in_specs=[pl.BlockSpec((None, n_sub, 8 * CMP_HID), lambda b: (b, 0, 0))]
        + [pl.BlockSpec(w[k].shape, c2) for k in names]
        + [pl.BlockSpec((None, N_HEADS, LANE), lambda b: (b, 0, 0)),
           pl.BlockSpec(tcs.shape, c2), pl.BlockSpec(ovs.shape, c2)],
        out_specs=[pl.BlockSpec((None, N_HEADS, LANE), lambda b: (b, 0, 0)),
                   pl.BlockSpec((None, N_HEADS, LANE), lambda b: (b, 0, 0))],
        out_shape=[jax.ShapeDtypeStruct((dbsz, N_HEADS, LANE), F32),
                   jax.ShapeDtypeStruct((dbsz, N_HEADS, LANE), jnp.int32)],
        compiler_params=_params(("parallel",)),
        name="nsa_sample_cmp",
    )(hbuf, *[w[k] for k in names], q8, tcs, ovs)


def _nsa_sample_sel_kernel(idx_ref, pt_ref, *refs, qblk, n_win):
    nblk = KV_HEADS * N_SEL_BLOCKS
    blocks = refs[:nblk]
    (q_ref, slcn_ref, winn_ref, state_ref, gs_ref, oc_ref, tnear_ref, b0_ref, tws_ref,
     o_ref, wout_ref) = refs[nblk:]
    b = pl.program_id(0)
    q8f = q_ref[...]
    q8 = q8f.astype(BF16)
    hrow = lax.broadcasted_iota(jnp.int32, (N_HEADS, LANE), 0)
    b0 = b0_ref[:, 0:1]

    slcn = slcn_ref[0:1, :]
    s_new = jnp.sum(q8f * slcn[:, 0:LANE], axis=1, keepdims=True) + b0
    o_sel = []
    for g in range(KV_HEADS):
        ss = []
        for r in range(N_SEL_BLOCKS):
            blk = idx_ref[b, g * N_SEL_BLOCKS + r]
            s = _dot_nt(q8, blocks[g * N_SEL_BLOCKS + r][:, 0:LANE].astype(BF16))
            near = jnp.where(blk == qblk - 1, tnear_ref[0], jnp.where(blk == qblk - 2, tnear_ref[1], 0.0))
            ss.append(jnp.where(blk < qblk, s + near, NEG))
        s = jnp.concatenate(ss, axis=1)
        p, pn, l = _row_softmax_parts(s, s_new)
        o = pn * slcn[:, LANE:2 * LANE]
        for r in range(N_SEL_BLOCKS):
            pr = p[:, r * SEL_LEN:(r + 1) * SEL_LEN].astype(BF16)
            o = o + _dot(pr, blocks[g * N_SEL_BLOCKS + r][:, LANE:2 * LANE].astype(BF16))
        o_sel.append(o / l)
    o_s = jnp.where(hrow // HPG == 0, o_sel[0], o_sel[1])

    state = state_ref[...]
    winn = winn_ref[0:1, :]
    sw = _dot_nt(q8, state[:, 0:LANE].astype(BF16)) + tws_ref[...]
    sw_new = jnp.sum(q8f * winn[:, 0:LANE], axis=1, keepdims=True) + b0
    p, pn, l = _row_softmax_parts(sw, sw_new)
    o_w = (_dot(p.astype(BF16), state[:, LANE:2 * LANE].astype(BF16)) + pn * winn[:, LANE:2 * LANE]) / l

    o_ref[...] = gs_ref[0] * oc_ref[...] + gs_ref[1] * o_s + gs_ref[2] * o_w
    wout_ref[0:n_win - 1, :] = state_ref[1:n_win, :]
    wout_ref[n_win - 1:n_win, :] = winn


def _nsa_sample_sel(idx, page_table, slc_cache2, q8, slc_new, win_new, state, gs, oc, tnear, b0, tws, *, qblk):
    dbsz, n_pages = page_table.shape
    n_win = state.shape[1]
    halves = PAGE // SEL_LEN

    def blk_spec(k):
        def imap(b, idx_ref, pt_ref):
            j = idx_ref[b, k]
            page = pt_ref[b, jnp.minimum(j // halves, n_pages - 1)]
            return (page * halves + j % halves, 0, 0)
        return pl.BlockSpec((None, SEL_LEN, 256), imap)

    def per_b(shape):
        nd = len(shape)
        return pl.BlockSpec((None,) + tuple(shape[1:]), lambda b, i, p: (b,) + (0,) * (nd - 1))

    def const(shape):
        nd = len(shape)
        return pl.BlockSpec(tuple(shape), lambda b, i, p: (0,) * nd)

    nblk = KV_HEADS * N_SEL_BLOCKS
    grid_spec = pltpu.PrefetchScalarGridSpec(
        num_scalar_prefetch=2, grid=(dbsz,),
        in_specs=[blk_spec(k) for k in range(nblk)]
        + [per_b(q8.shape), per_b(slc_new.shape), per_b(win_new.shape), per_b(state.shape), per_b(gs.shape),
           per_b(oc.shape), const(tnear.shape), const(b0.shape), const(tws.shape)],
        out_specs=[pl.BlockSpec((N_HEADS, LANE), lambda b, i, p: (b, 0)),
                   pl.BlockSpec((None, n_win, 256), lambda b, i, p: (b, 0, 0))])
    return pl.pallas_call(
        functools.partial(_nsa_sample_sel_kernel, qblk=qblk, n_win=n_win),
        grid_spec=grid_spec,
        out_shape=[jax.ShapeDtypeStruct((dbsz * N_HEADS, LANE), F32),
                   jax.ShapeDtypeStruct((dbsz, n_win, 256), F32)],
        compiler_params=_params(("arbitrary",)),
        name="nsa_sample_sel",
    )(idx, page_table, *([slc_cache2] * nblk), q8, slc_new, win_new, state, gs, oc, tnear, b0, tws)


def _merge_kernel(x_ref, o_ref, u_ref, vn_ref, ga_ref, gb_ref, ws_ref, bs_ref, wbn_ref, wbg_ref, wout_ref,
                  lng_ref, x1_ref, *, tm):
    chunk = ws_ref.shape[1]
    keep = (lax.broadcasted_iota(jnp.int32, (chunk, chunk), 0)
            >= lax.broadcasted_iota(jnp.int32, (chunk, chunk), 1))
    vb = vn_ref[...].astype(BF16)
    cols = []
    for gg in range(GM_GROUPS):
        wt = jnp.where(keep, ws_ref[gg], 0.0).astype(BF16)
        rows = [_dot(wt, vb[c * chunk:(c + 1) * chunk, gg * LANE:(gg + 1) * LANE]) for c in range(tm // chunk)]
        cols.append(jnp.concatenate(rows, axis=0) if len(rows) > 1 else rows[0])
    s = jnp.concatenate(cols, axis=1) + jnp.concatenate([bs_ref[...]] * (tm // chunk), axis=0)
    o_gm = (u_ref[...].astype(F32) * s).astype(BF16)
    if len(wbn_ref.shape) == 2:
        a = _dot(o_ref[...].astype(BF16), wbn_ref[...])
    else:
        a = jnp.zeros((tm, D_MODEL), F32)
        for hh in range(N_HEADS):
            a += _dot(o_ref[pl.ds(hh, tm, stride=N_HEADS), :].astype(BF16), wbn_ref[hh])
    bm = _dot(o_gm, wbg_ref[...])
    mixed = (ga_ref[...].astype(F32) * a + gb_ref[...].astype(F32) * bm).astype(BF16)
    x1_ref[...] = x_ref[...] + _rms(_dot(mixed, wout_ref[...]), lng_ref[...])


def _merge(x, o_nsa, u, vn, ga, gb, ws, bs, wbn, w, *, tm, name):
    n = x.shape[0]
    row = lambda i: (i, 0)
    c2 = lambda i: (0, 0)
    c3 = lambda i: (0, 0, 0)
    ws_spec = pl.BlockSpec(ws.shape, c3)
    o_rows = o_nsa.shape[0] // n
    return pl.pallas_call(
        functools.partial(_merge_kernel, tm=tm),
        grid=(n // tm,),
        in_specs=[pl.BlockSpec((tm, D_MODEL), row), pl.BlockSpec((tm * o_rows, o_nsa.shape[1]), row),
                  pl.BlockSpec((tm, GM_DIM), row), pl.BlockSpec((tm, GM_DIM), row),
                  pl.BlockSpec((tm, D_MODEL), row), pl.BlockSpec((tm, D_MODEL), row),
                  ws_spec, pl.BlockSpec(bs.shape, c2),
                  pl.BlockSpec(wbn.shape, c2 if wbn.ndim == 2 else c3), pl.BlockSpec(w["wbg"].shape, c2),
                  pl.BlockSpec(w["wout"].shape, c2), pl.BlockSpec((1, D_MODEL), c2)],
        out_specs=pl.BlockSpec((tm, D_MODEL), row),
        out_shape=jax.ShapeDtypeStruct((n, D_MODEL), F32),
        compiler_params=_params(("parallel",)),
        name=name,
    )(x, o_nsa, u, vn, ga, gb, ws, bs, wbn, w["wbg"], w["wout"], w["ln_mix_post"])


def _ffn_kernel(x_ref, p_ref, lnf_ref, wg_ref, wu_ref, wd_ref, lnp_ref, wple_ref, wpg_ref, lne_ref, y_ref,
                h2_ref, acc_ref):
    j = pl.program_id(1)

    @pl.when(j == 0)
    def _():
        h2_ref[...] = _rms(x_ref[...], lnf_ref[...]).astype(BF16)
        acc_ref[...] = jnp.zeros(acc_ref.shape, F32)

    h2 = h2_ref[...]
    act = (jax.nn.silu(_dot(h2, wg_ref[...])) * _dot(h2, wu_ref[...])).astype(BF16)
    acc_ref[...] += _dot(act, wd_ref[...])

    @pl.when(j == pl.num_programs(1) - 1)
    def _():
        x2 = x_ref[...] + _rms(acc_ref[...], lnp_ref[...])
        e = _dot(p_ref[...].astype(BF16), wple_ref[...])
        gp = jax.nn.sigmoid(_dot(x2.astype(BF16), wpg_ref[...]))
        y_ref[...] = x2 + _rms(gp * e, lne_ref[...])


def _ffn(x1, ple, w, *, tm, name):
    n = x1.shape[0]
    row = lambda i, j: (i, 0)
    c2 = lambda i, j: (0, 0)
    return pl.pallas_call(
        _ffn_kernel,
        grid=(n // tm, D_FF // FF_CHUNK),
        in_specs=[pl.BlockSpec((tm, D_MODEL), row), pl.BlockSpec((tm, PLE_DIM), row),
                  pl.BlockSpec((1, D_MODEL), c2),
                  pl.BlockSpec((D_MODEL, FF_CHUNK), lambda i, j: (0, j)),
                  pl.BlockSpec((D_MODEL, FF_CHUNK), lambda i, j: (0, j)),
                  pl.BlockSpec((FF_CHUNK, D_MODEL), lambda i, j: (j, 0)),
                  pl.BlockSpec((1, D_MODEL), c2), pl.BlockSpec(w["wple"].shape, c2),
                  pl.BlockSpec(w["wpg"].shape, c2), pl.BlockSpec((1, D_MODEL), c2)],
        out_specs=pl.BlockSpec((tm, D_MODEL), row),
        out_shape=jax.ShapeDtypeStruct((n, D_MODEL), F32),
        scratch_shapes=[pltpu.VMEM((tm, D_MODEL), BF16), pltpu.VMEM((tm, D_MODEL), F32)],
        compiler_params=_params(("parallel", "arbitrary")),
        name=name,
    )(x1, ple, w["ln_ffn_pre"], w["wfg"], w["wfu"], w["wfd"], w["ln_ffn_post"], w["wple"], w["wpg"],
      w["ln_ple_post"])


def _prep_weights(i, ln_mix_pre, w_in, cmp_pe_k, cmp_w1_k, cmp_w2_k, cmp_pe_v, cmp_w1_v, cmp_w2_v, gm_ln_g,
                  gm_ln_b, w_branch_nsa, w_branch_gm, w_out, ln_mix_post, ln_ffn_pre, w_ffn_gate, w_ffn_up,
                  w_ffn_down, ln_ffn_post, w_ple, w_ple_gate, ln_ple_post):
    q_dim = N_HEADS * HEAD_DIM
    kv3 = 6 * KV_HEADS * HEAD_DIM
    n_gate = 3 * N_HEADS
    win = w_in[i]
    o = 0
    wq = win[:, o:o + q_dim]; o += q_dim
    wkv = win[:, o:o + kv3]; o += kv3
    wg = win[:, o:o + n_gate]; o += n_gate
    wuv = win[:, o:o + 2 * GM_DIM]; o += 2 * GM_DIM
    wmg = win[:, o:o + 2 * D_MODEL]
    wq4 = wq.reshape(D_MODEL, N_HEADS, 1, HEAD_DIM)
    half = (np.arange(N_HEADS) // HPG)[None, :, None, None] == np.arange(KV_HEADS)[None, None, :, None]
    wq_pad = jnp.where(half, wq4, 0.0).reshape(D_MODEL, N_HEADS * LANE)
    wg_pad = jnp.pad(wg, ((0, 0), (0, LANE - n_gate)))

    def blockdiag(w1):
        w1s = w1.reshape(CMP_LEN // CMP_STRIDE, CMP_STRIDE, HEAD_DIM, CMP_HID).transpose(1, 2, 0, 3)
        w1s = w1s.reshape(CMP_STRIDE, HEAD_DIM, 2 * CMP_HID)
        z = jnp.zeros_like(w1s)
        return jnp.concatenate([jnp.concatenate([w1s, z], axis=2), jnp.concatenate([z, w1s], axis=2)], axis=1)

    wbn4 = w_branch_nsa[i].reshape(N_HEADS, 1, HEAD_DIM, D_MODEL)
    wbn_heads = jnp.where(half.reshape(N_HEADS, KV_HEADS, 1, 1), wbn4, 0.0).reshape(N_HEADS, LANE, D_MODEL)

    zk = jnp.zeros_like(cmp_w2_k[i])
    zv = jnp.zeros_like(cmp_w2_v[i])
    b = lambda a: a.astype(BF16)
    r = lambda a: a[i][None, :].astype(F32)
    return dict(
        ln_mix_pre=r(ln_mix_pre), wq=b(wq_pad), wkv=b(wkv), wg=b(wg_pad), wuv=b(wuv), wmg=b(wmg),
        gm_ln_g=r(gm_ln_g), gm_ln_b=r(gm_ln_b),
        wbk=b(blockdiag(cmp_w1_k[i])), wbv=b(blockdiag(cmp_w1_v[i])),
        pek=jnp.broadcast_to(cmp_pe_k[i].reshape(1, -1), (8, CMP_LEN * HEAD_DIM)).astype(F32), w1k=b(cmp_w1_k[i]),
        pev=jnp.broadcast_to(cmp_pe_v[i].reshape(1, -1), (8, CMP_LEN * HEAD_DIM)).astype(F32), w1v=b(cmp_w1_v[i]),
        w2ka=b(jnp.concatenate([cmp_w2_k[i], zk], axis=1)), w2kb=b(jnp.concatenate([zk, cmp_w2_k[i]], axis=1)),
        w2va=b(jnp.concatenate([cmp_w2_v[i], zv], axis=1)), w2vb=b(jnp.concatenate([zv, cmp_w2_v[i]], axis=1)),
        w2vt=b(cmp_w2_v[i].T),
        wbn=b(w_branch_nsa[i]), wbn_heads=b(wbn_heads), wbg=b(w_branch_gm[i]), wout=b(w_out[i]), ln_mix_post=r(ln_mix_post),
        ln_ffn_pre=r(ln_ffn_pre), wfg=b(w_ffn_gate[i]), wfu=b(w_ffn_up[i]), wfd=b(w_ffn_down[i]),
        ln_ffn_post=r(ln_ffn_post), wple=b(w_ple[i]), wpg=b(w_ple_gate[i]), ln_ple_post=r(ln_ple_post),
    )


def _gate_perm():
    return np.array([hg * 3 + br for br in range(3) for hg in range(N_HEADS)])


def _prompt_layer(x, ple, w, rel_bias, gm_ws, gm_bs):
    bsz, seq, _ = x.shape
    assert seq % SEL_TILE == 0 and seq // SEL_LEN <= NJ and seq >= WINDOW
    n = bsz * seq
    tm = 256
    xf = x.reshape(n, D_MODEL)
    wp = dict(w)
    perm = _gate_perm()
    wp["wg"] = jnp.concatenate([w["wg"][:, perm], w["wg"][:, len(perm):]], axis=1)
    (qhm, cmp, slc, win, kaug, vts, kwin, vtw, gt, u, vn, ga, gb) = _inproj(xf, wp, prompt=True, seq=seq, tm=tm)
    kc, vct = _compress_prompt(cmp, w, bsz=bsz, seq=seq)

    sh = _shifted_bias(rel_bias)
    qi = np.arange(Q_BLOCK)[None, :]
    r = np.arange(3 * (Q_BLOCK // CMP_STRIDE))[:, None]
    d_c = qi - CMP_STRIDE * r + (2 * Q_BLOCK - (CMP_LEN - 1))
    tc = _prompt_table(sh, d_c, d_c >= 0)
    r = np.arange(2 * SEL_TILE)[:, None]
    ts = []
    for par in range(SEL_TILE // Q_BLOCK):
        d_s = SEL_TILE + Q_BLOCK * par + qi - r
        ts.append(_prompt_table(sh, d_s, d_s >= 0))
    ts = jnp.stack(ts, axis=1)
    r = np.arange(WINDOW + Q_BLOCK)[:, None]
    d_w = qi + WINDOW - r
    tw = _prompt_table(sh, d_w, (d_w >= 0) & (d_w <= WINDOW))
    ovt = jnp.asarray(_overlap_t(seq // CMP_STRIDE, NJ))
    o_nsa = _nsa_prompt(qhm, kc, vct, kaug, vts, kwin, vtw, gt, (tc, ts, tw, ovt), bsz=bsz, seq=seq)

    bs_tile = jnp.repeat(gm_bs.T, LANE, axis=1).astype(F32)
    x1 = _merge(xf, o_nsa, u, vn, ga, gb, gm_ws.astype(F32), bs_tile, w["wbn"], w, tm=512, name="merge_prompt")
    y = _ffn(x1, ple.reshape(n, PLE_DIM), w, tm=512, name="ffn_prompt")
    kvshape = (bsz, seq, 2, KV_HEADS, HEAD_DIM)
    win5 = win.reshape(kvshape)
    return (y.reshape(bsz, seq, D_MODEL), cmp.reshape(kvshape), slc.reshape(kvshape),
            win5[:, seq - min(WINDOW, seq):])


def _sample_layer(x, ple, w, rel_bias, gm_ws, gm_bs, cache_cmp, cache_slc, win_buf, page_table):
    dbsz, nq, _ = x.shape
    assert nq == 1
    n_pool = cache_cmp.shape[0]
    n_pages = page_table.shape[1]
    past = n_pages * PAGE
    n_win = win_buf.shape[1]
    assert past % GM_CHUNK == 0 and n_win == WINDOW and past >= WINDOW and dbsz % 8 == 0
    xf = x.reshape(dbsz, D_MODEL)
    q, cmp, slc, win, gate, u, vn, ga, gb = _inproj(xf, w, prompt=False, seq=1, tm=dbsz)

    pps = min(16, n_pages)
    hbuf = _compress_paged(cache_cmp.reshape(n_pool, PAGE, 256), page_table, w, pps=pps)
    n_sub = past // CMP_STRIDE
    qblk = past // SEL_LEN
    n_sel_pad = -(-(qblk + 1) // LANE) * LANE
    sh = _shifted_bias(rel_bias)
    i = np.arange(n_sub)
    d_c = past - (CMP_STRIDE * i + CMP_LEN - 1)
    tcs = _sample_table(sh, d_c, i < n_sub - 1)
    ovs = jnp.asarray(_overlap_t(n_sub, n_sel_pad).T)
    q8 = q.reshape(dbsz, N_HEADS, LANE)
    oc, idx = _nsa_sample_cmp(hbuf, q8, w, tcs, ovs, qblk=qblk)
    idx2 = idx[:, 0:KV_HEADS, 0:N_SEL_BLOCKS].reshape(dbsz, KV_HEADS * N_SEL_BLOCKS)

    l = np.arange(SEL_LEN)
    tnear = jnp.stack([_sample_table(sh, SEL_LEN - l, l >= 0), _sample_table(sh, 2 * SEL_LEN - l, l >= 0)])
    b0 = jnp.broadcast_to(sh[0][:, None], (N_HEADS, LANE)).astype(F32)
    kpos = np.arange(n_win)
    tws = _sample_table(sh, n_win - kpos, kpos >= 0)
    gs = gate[:, 0:3 * N_HEADS].reshape(dbsz, N_HEADS, 3).transpose(0, 2, 1)
    gs = jnp.broadcast_to(gs[..., None], (dbsz, 3, N_HEADS, LANE))
    o8, win_out = _nsa_sample_sel(
        idx2, page_table, cache_slc.reshape(n_pool * (PAGE // SEL_LEN), SEL_LEN, 256), q8,
        jnp.broadcast_to(slc[:, None, :], (dbsz, 8, 256)), jnp.broadcast_to(win[:, None, :], (dbsz, 8, 256)),
        win_buf.reshape(dbsz, n_win, 256), gs, oc,
        tnear, b0, tws, qblk=qblk)

    ws_diag = gm_ws[:, 0, 0][:, None, None] * jnp.eye(dbsz, dtype=F32)[None]
    bs_tile = jnp.broadcast_to(jnp.repeat(gm_bs[:, 0], LANE)[None, :], (dbsz, GM_GROUPS * LANE)).astype(F32)
    x1 = _merge(xf, o8, u, vn, ga, gb, ws_diag, bs_tile, w["wbn_heads"], w, tm=dbsz, name="merge_sample")
    y = _ffn(x1, ple.reshape(dbsz, PLE_DIM), w, tm=dbsz, name="ffn_sample")
    kvshape = (dbsz, 1, 2, KV_HEADS, HEAD_DIM)
    return (y.reshape(dbsz, 1, D_MODEL), cmp.reshape(kvshape), slc.reshape(kvshape),
            win_out.reshape(dbsz, n_win, 2, KV_HEADS, HEAD_DIM), vn.reshape(dbsz, 1, GM_DIM))


def kernel(x_prompt, x_sample, cache_cmp_kv, cache_slc_kv, state_win_kv, page_table, p_prompt, p_sample, rel_bias,
           ln_mix_pre, w_in, cmp_pe_k, cmp_w1_k, cmp_w2_k, cmp_pe_v, cmp_w1_v, cmp_w2_v, gm_ln_g, gm_ln_b, gm_ws,
           gm_bs, w_branch_nsa, w_branch_gm, w_out, ln_mix_post, ln_ffn_pre, w_ffn_gate, w_ffn_up, w_ffn_down,
           ln_ffn_post, w_ple, w_ple_gate, ln_ple_post):
    depth = w_in.shape[0]
    xp, xs = x_prompt, x_sample
    outs = [[] for _ in range(7)]
    for i in range(depth):
        w = _prep_weights(i, ln_mix_pre, w_in, cmp_pe_k, cmp_w1_k, cmp_w2_k, cmp_pe_v, cmp_w1_v, cmp_w2_v, gm_ln_g,
                          gm_ln_b, w_branch_nsa, w_branch_gm, w_out, ln_mix_post, ln_ffn_pre, w_ffn_gate, w_ffn_up,
                          w_ffn_down, ln_ffn_post, w_ple, w_ple_gate, ln_ple_post)
        xp, c_p, s_p, w_p = _prompt_layer(xp, p_prompt[i], w, rel_bias, gm_ws[i], gm_bs[i])
        xs, c_s, s_s, w_s, v_s = _sample_layer(xs, p_sample[i], w, rel_bias, gm_ws[i], gm_bs[i], cache_cmp_kv[i],
                                               cache_slc_kv[i], state_win_kv[i], page_table)
        for lst, val in zip(outs, (c_p, s_p, w_p, c_s, s_s, w_s, v_s)):
            lst.append(val)
    return (xp, xs) + tuple(jnp.stack(o) for o in outs)
```

```python
import functools
import math

import jax
import jax.numpy as jnp
import numpy as np
from jax import lax
from jax.experimental import pallas as pl
from jax.experimental.pallas import tpu as pltpu

F32 = jnp.float32
BF16 = jnp.bfloat16

D_MODEL = 1024
N_HEADS = 8
KV_HEADS = 2
HPG = N_HEADS // KV_HEADS
HEAD_DIM = 64
CMP_LEN = 32
CMP_STRIDE = 16
CMP_HID = 2 * HEAD_DIM
SEL_LEN = 64
N_SEL_BLOCKS = 16
WINDOW = 512
Q_BLOCK = 128
GM_GROUPS = 4
GM_CHUNK = 128
GM_DIM = D_MODEL // 2
D_FF = -(-8 * D_MODEL // (3 * 256)) * 256
PLE_DIM = 256
N_BUCKETS = 32
MAX_DISTANCE = 128
EPS = 1e-6
NEG = -1e30
FORCE = 1e6
PAGE = 128

LANE = 128
NJ = 128
SEL_TILE = 256
FF_CHUNK = 256
VMEM_LIMIT = 56 * 1024 * 1024


def _dot(a, b):
    return jnp.dot(a, b, preferred_element_type=F32)


def _dot_nt(a, b):
    return lax.dot_general(a, b, (((1,), (1,)), ((), ())), preferred_element_type=F32)


def _dot_f32(a, b):
    return jnp.dot(a, b, preferred_element_type=F32, precision=lax.Precision.HIGHEST)


def _rms(x, g):
    return x * lax.rsqrt(jnp.mean(x * x, axis=-1, keepdims=True) + EPS) * g


def _params(sem):
    return pltpu.CompilerParams(dimension_semantics=sem, vmem_limit_bytes=VMEM_LIMIT)


def _t5_bucket_np(d):
    d = np.maximum(d, 0)
    max_exact = N_BUCKETS // 2
    ratio = (np.log(np.maximum(d, 1).astype(np.float32) / np.float32(max_exact))
             / np.float32(math.log(MAX_DISTANCE / max_exact)))
    large = np.minimum(max_exact + (ratio * np.float32(N_BUCKETS - max_exact)).astype(np.int32), N_BUCKETS - 1)
    return np.where(d < max_exact, d, large)


_BUCKET = _t5_bucket_np(np.arange(MAX_DISTANCE + 1))
assert _BUCKET[MAX_DISTANCE] == N_BUCKETS - 1


def _shifted_bias(rel_bias):
    return rel_bias[_BUCKET] - rel_bias[N_BUCKETS - 1][None, :]


def _prompt_table(sh, dist, valid):
    t = sh[np.minimum(np.maximum(dist, 0), MAX_DISTANCE)]
    t = jnp.where(valid[..., None], t, NEG)
    t = t.reshape(dist.shape[0], Q_BLOCK, KV_HEADS, HPG).transpose(2, 0, 3, 1)
    return t.reshape(KV_HEADS, dist.shape[0], HPG * Q_BLOCK).astype(F32)


def _sample_table(sh, dist, valid):
    t = sh[np.minimum(np.maximum(dist, 0), MAX_DISTANCE)]
    return jnp.where(valid[:, None], t, NEG).T.astype(F32)


def _overlap_t(n_cmp, n_sel_pad):
    ci = np.arange(n_cmp)[None, :] * CMP_STRIDE
    sj = np.arange(n_sel_pad)[:, None] * SEL_LEN
    return ((ci <= sj + SEL_LEN - 1) & (ci + CMP_LEN - 1 >= sj)).astype(np.float32)


def _inproj_kernel(x_ref, lng_ref, wq_ref, wkv_ref, wg_ref, wuv_ref, wmg_ref, glg_ref, glb_ref, *outs,
                   prompt, seq, tm):
    h = _rms(x_ref[...], lng_ref[...]).astype(BF16)
    q = _dot(h, wq_ref[...]) * (HEAD_DIM ** -0.5)
    kv = _dot(h, wkv_ref[...])
    gate = jax.nn.sigmoid(_dot(h, wg_ref[...]))
    uv = jax.nn.gelu(_dot(h, wuv_ref[...]))
    u = uv[:, :GM_DIM]
    v = uv[:, GM_DIM:]
    mu = jnp.mean(v, axis=-1, keepdims=True)
    var = jnp.mean(jnp.square(v - mu), axis=-1, keepdims=True)
    vn = (v - mu) * lax.rsqrt(var + EPS) * glg_ref[...] + glb_ref[...]
    mg = jax.nn.sigmoid(_dot(h, wmg_ref[...]))
    cmp, slc, win = kv[:, 0:256], kv[:, 256:512], kv[:, 512:768]
    if prompt:
        (q_ref, cmp_ref, cmpt_ref, slct_ref, wint_ref, kaug_ref, vts_ref, kwin_ref, vtw_ref, gt_ref,
         u_ref, vn_ref, ga_ref, gb_ref) = outs
        for hh in range(N_HEADS):
            q_ref[hh] = q[:, hh * LANE:(hh + 1) * LANE].astype(BF16)
        pos = (pl.program_id(0) * tm + lax.broadcasted_iota(jnp.int32, (tm, NJ), 0)) % seq
        onehot = (lax.broadcasted_iota(jnp.int32, (tm, NJ), 1) == pos // SEL_LEN)
        kaug_ref[:, 0:NJ] = onehot.astype(BF16)
        kaug_ref[:, NJ:NJ + LANE] = slc[:, 0:LANE].astype(BF16)
        kwin_ref[...] = win[:, 0:LANE].astype(BF16)
        slc_t = slc.T
        win_t = win.T
        for g in range(KV_HEADS):
            r0 = LANE + HEAD_DIM * g
            for c in range(tm // SEL_TILE):
                vts_ref[g, c] = slc_t[r0:r0 + HEAD_DIM, c * SEL_TILE:(c + 1) * SEL_TILE].astype(BF16)
            for c in range(tm // LANE):
                vtw_ref[g, c] = win_t[r0:r0 + HEAD_DIM, c * LANE:(c + 1) * LANE].astype(BF16)
        gt_ref[...] = gate.T[0:32, :]
        cmpt_ref[...] = cmp.T
        slct_ref[...] = slc_t
        wint_ref[...] = win_t
    else:
        q_ref, cmp_ref, slc_ref, win_ref, gate_ref, u_ref, vn_ref, ga_ref, gb_ref = outs
        q_ref[...] = q
        gate_ref[...] = gate
        slc_ref[...] = slc
        win_ref[...] = win
    cmp_ref[...] = cmp
    u_ref[...] = u.astype(u_ref.dtype)
    vn_ref[...] = vn.astype(vn_ref.dtype)
    ga_ref[...] = mg[:, :D_MODEL].astype(ga_ref.dtype)
    gb_ref[...] = mg[:, D_MODEL:].astype(gb_ref.dtype)


def _inproj(x, w, *, prompt, seq, tm):
    n = x.shape[0]
    row = lambda i: (i, 0)
    const = lambda i: (0, 0)
    in_specs = [pl.BlockSpec((tm, D_MODEL), row), pl.BlockSpec((1, D_MODEL), const),
                pl.BlockSpec(w["wq"].shape, const), pl.BlockSpec(w["wkv"].shape, const),
                pl.BlockSpec(w["wg"].shape, const), pl.BlockSpec(w["wuv"].shape, const),
                pl.BlockSpec(w["wmg"].shape, const), pl.BlockSpec((1, GM_DIM), const),
                pl.BlockSpec((1, GM_DIM), const)]
    kv_shapes = [jax.ShapeDtypeStruct((n, 256), F32)] * 3
    kv_specs = [pl.BlockSpec((tm, 256), row)] * 3
    if prompt:
        act = BF16
        tiles = seq // tm
        kvt_shapes = [jax.ShapeDtypeStruct((n // seq, 256, seq), F32)] * 3
        kvt_specs = [pl.BlockSpec((None, 256, tm), lambda i: (i // tiles, 0, i % tiles))] * 3
        out_shape = ([jax.ShapeDtypeStruct((N_HEADS, n, LANE), BF16)] + kv_shapes[:1] + kvt_shapes + [
            jax.ShapeDtypeStruct((n, NJ + LANE), BF16),
            jax.ShapeDtypeStruct((KV_HEADS, n // SEL_TILE, HEAD_DIM, SEL_TILE), BF16),
            jax.ShapeDtypeStruct((n, LANE), BF16),
            jax.ShapeDtypeStruct((KV_HEADS, n // LANE, HEAD_DIM, LANE), BF16),
            jax.ShapeDtypeStruct((32, n), F32)])
        out_specs = ([pl.BlockSpec((N_HEADS, tm, LANE), lambda i: (0, i, 0))] + kv_specs[:1] + kvt_specs + [
            pl.BlockSpec((tm, NJ + LANE), row),
            pl.BlockSpec((KV_HEADS, tm // SEL_TILE, HEAD_DIM, SEL_TILE), lambda i: (0, i, 0, 0)),
            pl.BlockSpec((tm, LANE), row),
            pl.BlockSpec((KV_HEADS, tm // LANE, HEAD_DIM, LANE), lambda i: (0, i, 0, 0)),
            pl.BlockSpec((32, tm), lambda i: (0, i))])
    else:
        act = F32
        out_shape = ([jax.ShapeDtypeStruct((n, N_HEADS * LANE), F32)] + kv_shapes
                     + [jax.ShapeDtypeStruct((n, LANE), F32)])
        out_specs = [pl.BlockSpec((tm, N_HEADS * LANE), row)] + kv_specs + [pl.BlockSpec((tm, LANE), row)]
    out_shape += [jax.ShapeDtypeStruct((n, GM_DIM), act)] * 2 + [jax.ShapeDtypeStruct((n, D_MODEL), act)] * 2
    out_specs += [pl.BlockSpec((tm, GM_DIM), row)] * 2 + [pl.BlockSpec((tm, D_MODEL), row)] * 2
    return pl.pallas_call(
        functools.partial(_inproj_kernel, prompt=prompt, seq=seq, tm=tm),
        grid=(n // tm,), in_specs=in_specs, out_specs=out_specs, out_shape=out_shape,
        compiler_params=_params(("parallel",)),
        name="inproj_prompt" if prompt else "inproj_sample",
    )(x, w["ln_mix_pre"], w["wq"], w["wkv"], w["wg"], w["wuv"], w["wmg"], w["gm_ln_g"], w["gm_ln_b"])


def _compress_finish(hk, hv, pek_ref, w1k_ref, pev_ref, w1v_ref, w2ka_ref, w2kb_ref, n_sub):
    pwk = _dot(pek_ref[...].astype(BF16), w1k_ref[...])[0:1, :]
    pwv = _dot(pev_ref[...].astype(BF16), w1v_ref[...])[0:1, :]

    def act(hh, g, pw):
        a0 = hh[:, 256 * g:256 * g + CMP_HID]
        a1 = hh[:, 256 * g + CMP_HID:256 * g + 2 * CMP_HID]
        return jax.nn.silu(a0 + pltpu.roll(a1, n_sub - 1, 0) + pw).astype(BF16)

    return [act(hk, 0, pwk), act(hk, 1, pwk)], [act(hv, 0, pwv), act(hv, 1, pwv)]


def _compress_prompt_kernel(cmpk_ref, cmpv_ref, wbk_ref, wbv_ref, pek_ref, w1k_ref, pev_ref, w1v_ref,
                            w2ka_ref, w2kb_ref, w2vt_ref, kc_ref, vct_ref, *, n_sub):
    hk = jnp.zeros((n_sub, 4 * CMP_HID), F32)
    hv = jnp.zeros((n_sub, 4 * CMP_HID), F32)
    for r in range(CMP_STRIDE):
        hk += _dot(cmpk_ref[pl.ds(r, n_sub, stride=CMP_STRIDE), :].astype(BF16), wbk_ref[r])
        hv += _dot(cmpv_ref[pl.ds(r, n_sub, stride=CMP_STRIDE), :].astype(BF16), wbv_ref[r])
    ak, av = _compress_finish(hk, hv, pek_ref, w1k_ref, pev_ref, w1v_ref, w2ka_ref, w2kb_ref, n_sub)
    kc_ref[...] = (_dot(ak[0], w2ka_ref[...]) + _dot(ak[1], w2kb_ref[...])).astype(BF16)
    for g in range(KV_HEADS):
        vct_ref[g] = _dot_nt(w2vt_ref[...], av[g]).astype(BF16)


def _compress_prompt(cmp, w, *, bsz, seq):
    n_sub = seq // CMP_STRIDE
    const2 = lambda b: (0, 0)
    const3 = lambda b: (0, 0, 0)
    return pl.pallas_call(
        functools.partial(_compress_prompt_kernel, n_sub=n_sub),
        grid=(bsz,),
        in_specs=[pl.BlockSpec((seq, LANE), lambda b: (b, 0)), pl.BlockSpec((seq, LANE), lambda b: (b, 1)),
                  pl.BlockSpec(w["wbk"].shape, const3), pl.BlockSpec(w["wbv"].shape, const3),
                  pl.BlockSpec(w["pek"].shape, const2), pl.BlockSpec(w["w1k"].shape, const2),
                  pl.BlockSpec(w["pev"].shape, const2), pl.BlockSpec(w["w1v"].shape, const2),
                  pl.BlockSpec(w["w2ka"].shape, const2), pl.BlockSpec(w["w2kb"].shape, const2),
                  pl.BlockSpec(w["w2vt"].shape, const2)],
        out_specs=[pl.BlockSpec((None, n_sub, LANE), lambda b: (b, 0, 0)),
                   pl.BlockSpec((None, KV_HEADS, HEAD_DIM, n_sub), lambda b: (b, 0, 0, 0))],
        out_shape=[jax.ShapeDtypeStruct((bsz, n_sub, LANE), BF16),
                   jax.ShapeDtypeStruct((bsz, KV_HEADS, HEAD_DIM, n_sub), BF16)],
        compiler_params=_params(("parallel",)),
        name="compress_prompt",
    )(cmp, cmp, w["wbk"], w["wbv"], w["pek"], w["w1k"], w["pev"], w["w1v"], w["w2ka"], w["w2kb"], w["w2vt"])


def _compress_paged_kernel(pt_ref, *refs, pps):
    kpages, vpages = refs[:pps], refs[pps:2 * pps]
    wbk_ref, wbv_ref, h_ref, xk_ref, xv_ref = refs[2 * pps:]
    sub_per_page = PAGE // CMP_STRIDE
    n = pps * sub_per_page
    for k in range(pps):
        xk_ref[k * PAGE:(k + 1) * PAGE, :] = kpages[k][...].T
        xv_ref[k * PAGE:(k + 1) * PAGE, :] = vpages[k][...].T
    hk = jnp.zeros((n, 4 * CMP_HID), F32)
    hv = jnp.zeros((n, 4 * CMP_HID), F32)
    for r in range(CMP_STRIDE):
        hk += _dot(xk_ref[pl.ds(r, n, stride=CMP_STRIDE), :].astype(BF16), wbk_ref[r])
        hv += _dot(xv_ref[pl.ds(r, n, stride=CMP_STRIDE), :].astype(BF16), wbv_ref[r])
    h_ref[:, 0:4 * CMP_HID] = hk
    h_ref[:, 4 * CMP_HID:8 * CMP_HID] = hv


def _compress_paged(cache_t, page_table, w, *, pps):
    dbsz, n_pages = page_table.shape
    sub_per_page = PAGE // CMP_STRIDE
    n_sub = n_pages * sub_per_page

    def page_spec(k, c):
        return pl.BlockSpec((None, None, LANE, PAGE), lambda b, t, pt: (pt[b, t * pps + k], c, 0, 0))

    const3 = lambda b, t, pt: (0, 0, 0)
    grid_spec = pltpu.PrefetchScalarGridSpec(
        num_scalar_prefetch=1, grid=(dbsz, n_pages // pps),
        in_specs=[page_spec(k, c) for c in range(2) for k in range(pps)]
        + [pl.BlockSpec(w["wbk"].shape, const3), pl.BlockSpec(w["wbv"].shape, const3)],
        out_specs=pl.BlockSpec((None, pps * sub_per_page, 8 * CMP_HID), lambda b, t, pt: (b, t, 0)),
        scratch_shapes=[pltpu.VMEM((pps * PAGE, LANE), F32), pltpu.VMEM((pps * PAGE, LANE), F32)])
    return pl.pallas_call(
        functools.partial(_compress_paged_kernel, pps=pps),
        grid_spec=grid_spec,
        out_shape=jax.ShapeDtypeStruct((dbsz, n_sub, 8 * CMP_HID), F32),
        compiler_params=_params(("parallel", "arbitrary")),
        name="compress_paged",
    )(page_table, *([cache_t] * (2 * pps)), w["wbk"], w["wbv"])


def _pages_t(cache):
    n, npos = cache.shape[0], cache.shape[1]
    return cache.transpose(0, 2, 3, 4, 1).reshape(n, 2, KV_HEADS * HEAD_DIM, npos)


def _col_softmax(s):
    m = jnp.max(s, axis=0, keepdims=True)
    p = jnp.exp(s - m)
    return p * jnp.where(m > 0.5 * NEG, 1.0 / jnp.sum(p, axis=0, keepdims=True), 0.0)


def _select_blocks(score, jio, axis):
    selneg = jnp.full(score.shape, NEG, F32)
    picked = []
    for _ in range(N_SEL_BLOCKS):
        mx = jnp.max(score, axis=axis, keepdims=True)
        jm = jnp.min(jnp.where(score == mx, jio, 1 << 20), axis=axis, keepdims=True)
        pick = jio == jm
        selneg = jnp.where(pick, 0.0, selneg)
        score = jnp.where(pick, -3e38, score)
        picked.append(jm)
    return selneg, picked


def _nsa_prompt_kernel(q_ref, kc_ref, vct_ref, kaug_ref, vts_ref, kwin_ref, vtw_ref, gt_ref,
                       tc_ref, ts_ref, tw_ref, ovt_ref, o_ref,
                       sc_ref, qaug_ref, m_ref, l_ref, acc_ref, *, n_cmp):
    g = pl.program_id(1)
    qb = pl.program_id(2)
    nl = HPG * Q_BLOCK
    q2 = q_ref[...].reshape(nl, LANE)

    pad = 2 * (Q_BLOCK // CMP_STRIDE)
    band = 3 * (Q_BLOCK // CMP_STRIDE)
    sc_ref[0:pad, :] = jnp.zeros((pad, nl), F32)
    sc_ref[pad:pad + n_cmp, :] = _dot_nt(kc_ref[...], q2)
    w0 = pl.multiple_of(qb * (Q_BLOCK // CMP_STRIDE), 8)
    sc_ref[pl.ds(w0, band), :] += tc_ref[...]
    sc = sc_ref[pad:pad + n_cmp, :]
    row = lax.broadcasted_iota(jnp.int32, (n_cmp, nl), 0)
    sc = jnp.where(row < (qb + 1) * (Q_BLOCK // CMP_STRIDE), sc, NEG)
    pc = _col_softmax(sc)
    o_c = _dot(vct_ref[...], pc.astype(BF16))

    psum = pc[:, 0:Q_BLOCK]
    for hh in range(1, HPG):
        psum = psum + pc[:, hh * Q_BLOCK:(hh + 1) * Q_BLOCK]
    imp_t = _dot_f32(ovt_ref[...], psum)
    jio = lax.broadcasted_iota(jnp.int32, (NJ, Q_BLOCK), 0)
    qi = lax.broadcasted_iota(jnp.int32, (NJ, Q_BLOCK), 1)
    qblk = qb * (Q_BLOCK // SEL_LEN) + qi // SEL_LEN
    forced = (jio == 0) | (jio == qblk) | (jio == qblk - 1)
    score = jnp.where(forced, FORCE, jnp.where(jio <= qblk, imp_t, -FORCE))
    selneg_t, _ = _select_blocks(score, jio, 0)
    selneg = selneg_t.T.astype(BF16)
    for hh in range(HPG):
        qaug_ref[hh * Q_BLOCK:(hh + 1) * Q_BLOCK, 0:NJ] = selneg
    qaug_ref[:, NJ:NJ + LANE] = q2

    m_ref[...] = jnp.full((1, nl), -1e38, F32)
    l_ref[...] = jnp.zeros((1, nl), F32)
    acc_ref[...] = jnp.zeros((HEAD_DIM, nl), F32)

    def tile(u, table):
        ka = kaug_ref[pl.ds(pl.multiple_of(u * SEL_TILE, SEL_TILE), SEL_TILE), :]
        s = _dot_nt(ka, qaug_ref[...])
        if table is not None:
            s = s + table
        m_old = m_ref[...]
        m_new = jnp.maximum(m_old, jnp.max(s, axis=0, keepdims=True))
        alpha = jnp.exp(m_old - m_new)
        p = jnp.exp(s - m_new)
        l_ref[...] = alpha * l_ref[...] + jnp.sum(p, axis=0, keepdims=True)
        acc_ref[...] = alpha * acc_ref[...] + _dot(vts_ref[u], p.astype(BF16))
        m_ref[...] = m_new

    tiles_per_q = SEL_TILE // Q_BLOCK
    u_last = qb // tiles_per_q
    par = qb % tiles_per_q

    def far(u, carry):
        tile(u, None)
        return carry

    lax.fori_loop(0, jnp.maximum(u_last - 1, 0), far, 0)

    @pl.when(u_last >= 1)
    def _():
        tile(u_last - 1, ts_ref[par, 0:SEL_TILE, :])

    tile(u_last, ts_ref[par, SEL_TILE:2 * SEL_TILE, :])
    o_s = acc_ref[...] / l_ref[...]

    n_wt = WINDOW // Q_BLOCK + 1
    tix = [jnp.maximum(qb - (n_wt - 1) + t, 0) for t in range(n_wt)]
    kw = jnp.concatenate([kwin_ref[pl.ds(pl.multiple_of(t * Q_BLOCK, Q_BLOCK), Q_BLOCK), :] for t in tix], axis=0)
    sw = _dot_nt(kw, q2) + tw_ref[...]
    wrow = lax.broadcasted_iota(jnp.int32, (n_wt * Q_BLOCK, nl), 0) + (qb - (n_wt - 1)) * Q_BLOCK
    pw = _col_softmax(jnp.where(wrow >= 0, sw, NEG))
    vw = jnp.concatenate([vtw_ref[t] for t in tix], axis=1)
    o_w = _dot(vw, pw.astype(BF16))

    def gate_row(br):
        rows = gt_ref[br * N_HEADS:(br + 1) * N_HEADS, :]
        rows = jnp.where(g == 0, rows[0:HPG], rows[HPG:2 * HPG])
        return jnp.concatenate([rows[hh:hh + 1, :] for hh in range(HPG)], axis=1)

    o_t = gate_row(0) * o_c + gate_row(1) * o_s + gate_row(2) * o_w
    stack = jnp.concatenate([o_t[:, hh * Q_BLOCK:(hh + 1) * Q_BLOCK] for hh in range(HPG)], axis=0)
    o_ref[...] = stack.T.astype(o_ref.dtype)


def _nsa_prompt(qhm, kc, vct, kaug, vts, kwin, vtw, gt, tabs, *, bsz, seq):
    n = bsz * seq
    nqb = seq // Q_BLOCK
    n_cmp = seq // CMP_STRIDE
    nl = HPG * Q_BLOCK
    tc, ts, tw, ovt = tabs
    in_specs = [
        pl.BlockSpec((HPG, Q_BLOCK, LANE), lambda b, g, i: (g, b * nqb + i, 0)),
        pl.BlockSpec((None, n_cmp, LANE), lambda b, g, i: (b, 0, 0)),
        pl.BlockSpec((None, None, HEAD_DIM, n_cmp), lambda b, g, i: (b, g, 0, 0)),
        pl.BlockSpec((seq, NJ + LANE), lambda b, g, i: (b, 0)),
        pl.BlockSpec((None, seq // SEL_TILE, HEAD_DIM, SEL_TILE), lambda b, g, i: (g, b, 0, 0)),
        pl.BlockSpec((seq, LANE), lambda b, g, i: (b, 0)),
        pl.BlockSpec((None, seq // LANE, HEAD_DIM, LANE), lambda b, g, i: (g, b, 0, 0)),
        pl.BlockSpec((32, Q_BLOCK), lambda b, g, i: (0, b * nqb + i)),
        pl.BlockSpec((None,) + tc.shape[1:], lambda b, g, i: (g, 0, 0)),
        pl.BlockSpec((None,) + ts.shape[1:], lambda b, g, i: (g, 0, 0, 0)),
        pl.BlockSpec((None,) + tw.shape[1:], lambda b, g, i: (g, 0, 0)),
        pl.BlockSpec(ovt.shape, lambda b, g, i: (0, 0)),
    ]
    pad = 2 * (Q_BLOCK // CMP_STRIDE)
    return pl.pallas_call(
        functools.partial(_nsa_prompt_kernel, n_cmp=n_cmp),
        grid=(bsz, KV_HEADS, nqb),
        in_specs=in_specs,
        out_specs=pl.BlockSpec((Q_BLOCK, HPG * HEAD_DIM), lambda b, g, i: (b * nqb + i, g)),
        out_shape=jax.ShapeDtypeStruct((n, N_HEADS * HEAD_DIM), BF16),
        scratch_shapes=[pltpu.VMEM((pad + n_cmp + pad, nl), F32), pltpu.VMEM((nl, NJ + LANE), BF16),
                        pltpu.VMEM((1, nl), F32), pltpu.VMEM((1, nl), F32), pltpu.VMEM((HEAD_DIM, nl), F32)],
        compiler_params=_params(("parallel", "parallel", "arbitrary")),
        name="nsa_prompt",
    )(qhm, kc, vct, kaug, vts, kwin, vtw, gt, tc, ts, tw, ovt)


def _row_softmax_parts(s, s_new):
    m = jnp.maximum(jnp.max(s, axis=1, keepdims=True), s_new)
    p = jnp.exp(s - m)
    pn = jnp.exp(s_new - m)
    return p, pn, jnp.sum(p, axis=1, keepdims=True) + pn


def _nsa_sample_cmp_kernel(h_ref, pek_ref, w1k_ref, pev_ref, w1v_ref, w2ka_ref, w2kb_ref, w2va_ref, w2vb_ref,
                           q_ref, tcs_ref, ovs_ref, oc_ref, idx_ref, *, n_sub, qblk):
    hh = h_ref[...]
    ak, av = _compress_finish(hh[:, 0:4 * CMP_HID], hh[:, 4 * CMP_HID:8 * CMP_HID],
                              pek_ref, w1k_ref, pev_ref, w1v_ref, w2ka_ref, w2kb_ref, n_sub)
    kc = (_dot(ak[0], w2ka_ref[...]) + _dot(ak[1], w2kb_ref[...])).astype(BF16)
    vc = (_dot(av[0], w2va_ref[...]) + _dot(av[1], w2vb_ref[...])).astype(BF16)
    q8 = q_ref[...].astype(BF16)
    s = _dot_nt(q8, kc) + tcs_ref[...]
    m = jnp.max(s, axis=1, keepdims=True)
    p = jnp.exp(s - m)
    pc = p / jnp.sum(p, axis=1, keepdims=True)
    oc_ref[...] = _dot(pc.astype(BF16), vc)
    hrow = lax.broadcasted_iota(jnp.int32, (N_HEADS, n_sub), 0)
    rows = []
    for g in range(KV_HEADS):
        rows.append(jnp.sum(jnp.where(hrow // HPG == g, pc, 0.0), axis=0, keepdims=True))
    psum = jnp.concatenate(rows + [jnp.zeros((N_HEADS - KV_HEADS, n_sub), F32)], axis=0)
    imp = _dot_f32(psum, ovs_ref[...])
    jl = lax.broadcasted_iota(jnp.int32, imp.shape, 1)
    forced = (jl == 0) | (jl == qblk) | (jl == qblk - 1)
    score = jnp.where(forced, FORCE, jnp.where(jl <= qblk, imp, -FORCE))
    _, picked = _select_blocks(score, jl, 1)
    lane = lax.broadcasted_iota(jnp.int32, (N_HEADS, LANE), 1)
    out = jnp.zeros((N_HEADS, LANE), jnp.int32)
    for r, jm in enumerate(picked):
        out = jnp.where(lane == r, jm, out)
    idx_ref[...] = out


def _nsa_sample_cmp(hbuf, q8, w, tcs, ovs, *, qblk):
    dbsz, n_sub, _ = hbuf.shape
    c2 = lambda b: (0, 0)
    names = ["pek", "w1k", "pev", "w1v", "w2ka", "w2kb", "w2va", "w2vb"]
    return pl.pallas_call(
        functools.partial(_nsa_sample_cmp_kernel, n_sub=n_sub, qblk=qblk),
        grid=(dbsz,),
        in_specs=[pl.BlockSpec((None, n_sub, 8 * CMP_HID), lambda b: (b, 0, 0))]
        + [pl.BlockSpec(w[k].shape, c2) for k in names]
        + [pl.BlockSpec((None, N_HEADS, LANE), lambda b: (b, 0, 0)),
           pl.BlockSpec(tcs.shape, c2), pl.BlockSpec(ovs.shape, c2)],
        out_specs=[pl.BlockSpec((None, N_HEADS, LANE), lambda b: (b, 0, 0)),
                   pl.BlockSpec((None, N_HEADS, LANE), lambda b: (b, 0, 0))],
        out_shape=[jax.ShapeDtypeStruct((dbsz, N_HEADS, LANE), F32),
                   jax.ShapeDtypeStruct((dbsz, N_HEADS, LANE), jnp.int32)],
        compiler_params=_params(("parallel",)),
        name="nsa_sample_cmp",
    )(hbuf, *[w[k] for k in names], q8, tcs, ovs)


def _nsa_sample_sel_kernel(idx_ref, pt_ref, *refs, qblk, n_win):
    nblk = KV_HEADS * N_SEL_BLOCKS
    blocks = refs[:nblk]
    (q_ref, slcn_ref, winn_ref, state_ref, gs_ref, oc_ref, tnear_ref, b0_ref, tws_ref,
     o_ref, wout_ref) = refs[nblk:]
    b = pl.program_id(0)
    q8f = q_ref[...]
    q8 = q8f.astype(BF16)
    hrow = lax.broadcasted_iota(jnp.int32, (N_HEADS, LANE), 0)
    b0 = b0_ref[:, 0:1]

    slcn = slcn_ref[0:1, :]
    s_new = jnp.sum(q8f * slcn[:, 0:LANE], axis=1, keepdims=True) + b0
    halves = PAGE // SEL_LEN
    lane = lax.broadcasted_iota(jnp.int32, (N_HEADS, PAGE), 1)
    o_sel = []
    for g in range(KV_HEADS):
        ss = []
        for r in range(N_SEL_BLOCKS):
            blk = idx_ref[b, g * N_SEL_BLOCKS + r]
            s = _dot(q8, blocks[g * N_SEL_BLOCKS + r][0].astype(BF16))
            near = jnp.where(blk // halves == qblk // halves - 1, tnear_ref[...], 0.0)
            ok = (blk < qblk) & (lane // SEL_LEN == blk % halves)
            ss.append(jnp.where(ok, s + near, NEG))
        s = jnp.concatenate(ss, axis=1)
        p, pn, l = _row_softmax_parts(s, s_new)
        o = pn * slcn[:, LANE:2 * LANE]
        for r in range(N_SEL_BLOCKS):
            pr = p[:, r * PAGE:(r + 1) * PAGE].astype(BF16)
            o = o + _dot_nt(pr, blocks[g * N_SEL_BLOCKS + r][1].astype(BF16))
        o_sel.append(o / l)
    o_s = jnp.where(hrow // HPG == 0, o_sel[0], o_sel[1])

    winn = winn_ref[0:1, :]
    sw = _dot(q8, state_ref[0].astype(BF16)) + tws_ref[...]
    sw_new = jnp.sum(q8f * winn[:, 0:LANE], axis=1, keepdims=True) + b0
    p, pn, l = _row_softmax_parts(sw, sw_new)
    o_w = (_dot_nt(p.astype(BF16), state_ref[1].astype(BF16)) + pn * winn[:, LANE:2 * LANE]) / l

    o_ref[...] = gs_ref[0] * oc_ref[...] + gs_ref[1] * o_s + gs_ref[2] * o_w
    wlane = lax.broadcasted_iota(jnp.int32, (LANE, n_win), 1)
    for kv in range(2):
        newcol = jnp.broadcast_to(winn[:, kv * LANE:(kv + 1) * LANE], (LANE, LANE)).T
        newcol = jnp.concatenate([newcol] * (n_win // LANE), axis=1)
        wout_ref[kv] = jnp.where(wlane == n_win - 1, newcol, pltpu.roll(state_ref[kv], n_win - 1, 1))


def _nsa_sample_sel(idx, page_table, slc_cache_t, q8, slc_new, win_new, state, gs, oc, tnear, b0, tws, *, qblk):
    dbsz, n_pages = page_table.shape
    n_win = state.shape[-1]
    halves = PAGE // SEL_LEN

    def blk_spec(k):
        def imap(b, idx_ref, pt_ref):
            j = idx_ref[b, k]
            return (pt_ref[b, jnp.minimum(j // halves, n_pages - 1)], 0, 0, 0)
        return pl.BlockSpec((None, 2, LANE, PAGE), imap)

    def per_b(shape):
        nd = len(shape)
        return pl.BlockSpec((None,) + tuple(shape[1:]), lambda b, i, p: (b,) + (0,) * (nd - 1))

    def const(shape):
        nd = len(shape)
        return pl.BlockSpec(tuple(shape), lambda b, i, p: (0,) * nd)

    nblk = KV_HEADS * N_SEL_BLOCKS
    grid_spec = pltpu.PrefetchScalarGridSpec(
        num_scalar_prefetch=2, grid=(dbsz,),
        in_specs=[blk_spec(k) for k in range(nblk)]
        + [per_b(q8.shape), per_b(slc_new.shape), per_b(win_new.shape), per_b(state.shape), per_b(gs.shape),
           per_b(oc.shape), const(tnear.shape), const(b0.shape), const(tws.shape)],
        out_specs=[pl.BlockSpec((N_HEADS, LANE), lambda b, i, p: (b, 0)),
                   pl.BlockSpec((None, 2, LANE, n_win), lambda b, i, p: (b, 0, 0, 0))])
    return pl.pallas_call(
        functools.partial(_nsa_sample_sel_kernel, qblk=qblk, n_win=n_win),
        grid_spec=grid_spec,
        out_shape=[jax.ShapeDtypeStruct((dbsz * N_HEADS, LANE), F32),
                   jax.ShapeDtypeStruct((dbsz, 2, LANE, n_win), F32)],
        compiler_params=_params(("arbitrary",)),
        name="nsa_sample_sel",
    )(idx, page_table, *([slc_cache_t] * nblk), q8, slc_new, win_new, state, gs, oc, tnear, b0, tws)


def _merge_kernel(x_ref, o_ref, u_ref, vn_ref, ga_ref, gb_ref, ws_ref, bs_ref, wbn_ref, wbg_ref, wout_ref,
                  lng_ref, x1_ref, *, tm):
    chunk = ws_ref.shape[1]
    keep = (lax.broadcasted_iota(jnp.int32, (chunk, chunk), 0)
            >= lax.broadcasted_iota(jnp.int32, (chunk, chunk), 1))
    vb = vn_ref[...].astype(BF16)
    cols = []
    for gg in range(GM_GROUPS):
        wt = jnp.where(keep, ws_ref[gg], 0.0).astype(BF16)
        rows = [_dot(wt, vb[c * chunk:(c + 1) * chunk, gg * LANE:(gg + 1) * LANE]) for c in range(tm // chunk)]
        cols.append(jnp.concatenate(rows, axis=0) if len(rows) > 1 else rows[0])
    s = jnp.concatenate(cols, axis=1) + jnp.concatenate([bs_ref[...]] * (tm // chunk), axis=0)
    o_gm = (u_ref[...].astype(F32) * s).astype(BF16)
    if len(wbn_ref.shape) == 2:
        a = _dot(o_ref[...].astype(BF16), wbn_ref[...])
    else:
        a = jnp.zeros((tm, D_MODEL), F32)
        for hh in range(N_HEADS):
            a += _dot(o_ref[pl.ds(hh, tm, stride=N_HEADS), :].astype(BF16), wbn_ref[hh])
    bm = _dot(o_gm, wbg_ref[...])
    mixed = (ga_ref[...].astype(F32) * a + gb_ref[...].astype(F32) * bm).astype(BF16)
    x1_ref[...] = x_ref[...] + _rms(_dot(mixed, wout_ref[...]), lng_ref[...])


def _merge(x, o_nsa, u, vn, ga, gb, ws, bs, wbn, w, *, tm, name):
    n = x.shape[0]
    row = lambda i: (i, 0)
    c2 = lambda i: (0, 0)
    c3 = lambda i: (0, 0, 0)
    ws_spec = pl.BlockSpec(ws.shape, c3)
    o_rows = o_nsa.shape[0] // n
    return pl.pallas_call(
        functools.partial(_merge_kernel, tm=tm),
        grid=(n // tm,),
        in_specs=[pl.BlockSpec((tm, D_MODEL), row), pl.BlockSpec((tm * o_rows, o_nsa.shape[1]), row),
                  pl.BlockSpec((tm, GM_DIM), row), pl.BlockSpec((tm, GM_DIM), row),
                  pl.BlockSpec((tm, D_MODEL), row), pl.BlockSpec((tm, D_MODEL), row),
                  ws_spec, pl.BlockSpec(bs.shape, c2),
                  pl.BlockSpec(wbn.shape, c2 if wbn.ndim == 2 else c3), pl.BlockSpec(w["wbg"].shape, c2),
                  pl.BlockSpec(w["wout"].shape, c2), pl.BlockSpec((1, D_MODEL), c2)],
        out_specs=pl.BlockSpec((tm, D_MODEL), row),
        out_shape=jax.ShapeDtypeStruct((n, D_MODEL), F32),
        compiler_params=_params(("parallel",)),
        name=name,
    )(x, o_nsa, u, vn, ga, gb, ws, bs, wbn, w["wbg"], w["wout"], w["ln_mix_post"])


def _ffn_kernel(x_ref, p_ref, lnf_ref, wg_ref, wu_ref, wd_ref, lnp_ref, wple_ref, wpg_ref, lne_ref, y_ref,
                h2_ref, acc_ref):
    j = pl.program_id(1)

    @pl.when(j == 0)
    def _():
        h2_ref[...] = _rms(x_ref[...], lnf_ref[...]).astype(BF16)
        acc_ref[...] = jnp.zeros(acc_ref.shape, F32)

    h2 = h2_ref[...]
    act = (jax.nn.silu(_dot(h2, wg_ref[...])) * _dot(h2, wu_ref[...])).astype(BF16)
    acc_ref[...] += _dot(act, wd_ref[...])

    @pl.when(j == pl.num_programs(1) - 1)
    def _():
        x2 = x_ref[...] + _rms(acc_ref[...], lnp_ref[...])
        e = _dot(p_ref[...].astype(BF16), wple_ref[...])
        gp = jax.nn.sigmoid(_dot(x2.astype(BF16), wpg_ref[...]))
        y_ref[...] = x2 + _rms(gp * e, lne_ref[...])


def _ffn(x1, ple, w, *, tm, name):
    n = x1.shape[0]
    row = lambda i, j: (i, 0)
    c2 = lambda i, j: (0, 0)
    return pl.pallas_call(
        _ffn_kernel,
        grid=(n // tm, D_FF // FF_CHUNK),
        in_specs=[pl.BlockSpec((tm, D_MODEL), row), pl.BlockSpec((tm, PLE_DIM), row),
                  pl.BlockSpec((1, D_MODEL), c2),
                  pl.BlockSpec((D_MODEL, FF_CHUNK), lambda i, j: (0, j)),
                  pl.BlockSpec((D_MODEL, FF_CHUNK), lambda i, j: (0, j)),
                  pl.BlockSpec((FF_CHUNK, D_MODEL), lambda i, j: (j, 0)),
                  pl.BlockSpec((1, D_MODEL), c2), pl.BlockSpec(w["wple"].shape, c2),
                  pl.BlockSpec(w["wpg"].shape, c2), pl.BlockSpec((1, D_MODEL), c2)],
        out_specs=pl.BlockSpec((tm, D_MODEL), row),
        out_shape=jax.ShapeDtypeStruct((n, D_MODEL), F32),
        scratch_shapes=[pltpu.VMEM((tm, D_MODEL), BF16), pltpu.VMEM((tm, D_MODEL), F32)],
        compiler_params=_params(("parallel", "arbitrary")),
        name=name,
    )(x1, ple, w["ln_ffn_pre"], w["wfg"], w["wfu"], w["wfd"], w["ln_ffn_post"], w["wple"], w["wpg"],
      w["ln_ple_post"])


def _prep_weights(i, ln_mix_pre, w_in, cmp_pe_k, cmp_w1_k, cmp_w2_k, cmp_pe_v, cmp_w1_v, cmp_w2_v, gm_ln_g,
                  gm_ln_b, w_branch_nsa, w_branch_gm, w_out, ln_mix_post, ln_ffn_pre, w_ffn_gate, w_ffn_up,
                  w_ffn_down, ln_ffn_post, w_ple, w_ple_gate, ln_ple_post):
    q_dim = N_HEADS * HEAD_DIM
    kv3 = 6 * KV_HEADS * HEAD_DIM
    n_gate = 3 * N_HEADS
    win = w_in[i]
    o = 0
    wq = win[:, o:o + q_dim]; o += q_dim
    wkv = win[:, o:o + kv3]; o += kv3
    wg = win[:, o:o + n_gate]; o += n_gate
    wuv = win[:, o:o + 2 * GM_DIM]; o += 2 * GM_DIM
    wmg = win[:, o:o + 2 * D_MODEL]
    wq4 = wq.reshape(D_MODEL, N_HEADS, 1, HEAD_DIM)
    half = (np.arange(N_HEADS) // HPG)[None, :, None, None] == np.arange(KV_HEADS)[None, None, :, None]
    wq_pad = jnp.where(half, wq4, 0.0).reshape(D_MODEL, N_HEADS * LANE)
    wg_pad = jnp.pad(wg, ((0, 0), (0, LANE - n_gate)))

    def blockdiag(w1):
        w1s = w1.reshape(CMP_LEN // CMP_STRIDE, CMP_STRIDE, HEAD_DIM, CMP_HID).transpose(1, 2, 0, 3)
        w1s = w1s.reshape(CMP_STRIDE, HEAD_DIM, 2 * CMP_HID)
        z = jnp.zeros_like(w1s)
        return jnp.concatenate([jnp.concatenate([w1s, z], axis=2), jnp.concatenate([z, w1s], axis=2)], axis=1)

    wbn4 = w_branch_nsa[i].reshape(N_HEADS, 1, HEAD_DIM, D_MODEL)
    wbn_heads = jnp.where(half.reshape(N_HEADS, KV_HEADS, 1, 1), wbn4, 0.0).reshape(N_HEADS, LANE, D_MODEL)

    zk = jnp.zeros_like(cmp_w2_k[i])
    zv = jnp.zeros_like(cmp_w2_v[i])
    b = lambda a: a.astype(BF16)
    r = lambda a: a[i][None, :].astype(F32)
    return dict(
        ln_mix_pre=r(ln_mix_pre), wq=b(wq_pad), wkv=b(wkv), wg=b(wg_pad), wuv=b(wuv), wmg=b(wmg),
        gm_ln_g=r(gm_ln_g), gm_ln_b=r(gm_ln_b),
        wbk=b(blockdiag(cmp_w1_k[i])), wbv=b(blockdiag(cmp_w1_v[i])),
        pek=jnp.broadcast_to(cmp_pe_k[i].reshape(1, -1), (8, CMP_LEN * HEAD_DIM)).astype(F32), w1k=b(cmp_w1_k[i]),
        pev=jnp.broadcast_to(cmp_pe_v[i].reshape(1, -1), (8, CMP_LEN * HEAD_DIM)).astype(F32), w1v=b(cmp_w1_v[i]),
        w2ka=b(jnp.concatenate([cmp_w2_k[i], zk], axis=1)), w2kb=b(jnp.concatenate([zk, cmp_w2_k[i]], axis=1)),
        w2va=b(jnp.concatenate([cmp_w2_v[i], zv], axis=1)), w2vb=b(jnp.concatenate([zv, cmp_w2_v[i]], axis=1)),
        w2vt=b(cmp_w2_v[i].T),
        wbn=b(w_branch_nsa[i]), wbn_heads=b(wbn_heads), wbg=b(w_branch_gm[i]), wout=b(w_out[i]), ln_mix_post=r(ln_mix_post),
        ln_ffn_pre=r(ln_ffn_pre), wfg=b(w_ffn_gate[i]), wfu=b(w_ffn_up[i]), wfd=b(w_ffn_down[i]),
        ln_ffn_post=r(ln_ffn_post), wple=b(w_ple[i]), wpg=b(w_ple_gate[i]), ln_ple_post=r(ln_ple_post),
    )


def _gate_perm():
    return np.array([hg * 3 + br for br in range(3) for hg in range(N_HEADS)])


def _prompt_layer(x, ple, w, rel_bias, gm_ws, gm_bs):
    bsz, seq, _ = x.shape
    assert seq % SEL_TILE == 0 and seq // SEL_LEN <= NJ and seq >= WINDOW
    n = bsz * seq
    tm = 256
    xf = x.reshape(n, D_MODEL)
    wp = dict(w)
    perm = _gate_perm()
    wp["wg"] = jnp.concatenate([w["wg"][:, perm], w["wg"][:, len(perm):]], axis=1)
    (qhm, cmp, cmpt, slct, wint, kaug, vts, kwin, vtw, gt, u, vn, ga, gb) = _inproj(
        xf, wp, prompt=True, seq=seq, tm=tm)
    kc, vct = _compress_prompt(cmp, w, bsz=bsz, seq=seq)

    sh = _shifted_bias(rel_bias)
    qi = np.arange(Q_BLOCK)[None, :]
    r = np.arange(3 * (Q_BLOCK // CMP_STRIDE))[:, None]
    d_c = qi - CMP_STRIDE * r + (2 * Q_BLOCK - (CMP_LEN - 1))
    tc = _prompt_table(sh, d_c, d_c >= 0)
    r = np.arange(2 * SEL_TILE)[:, None]
    ts = []
    for par in range(SEL_TILE // Q_BLOCK):
        d_s = SEL_TILE + Q_BLOCK * par + qi - r
        ts.append(_prompt_table(sh, d_s, d_s >= 0))
    ts = jnp.stack(ts, axis=1)
    r = np.arange(WINDOW + Q_BLOCK)[:, None]
    d_w = qi + WINDOW - r
    tw = _prompt_table(sh, d_w, (d_w >= 0) & (d_w <= WINDOW))
    ovt = jnp.asarray(_overlap_t(seq // CMP_STRIDE, NJ))
    o_nsa = _nsa_prompt(qhm, kc, vct, kaug, vts, kwin, vtw, gt, (tc, ts, tw, ovt), bsz=bsz, seq=seq)

    bs_tile = jnp.repeat(gm_bs.T, LANE, axis=1).astype(F32)
    x1 = _merge(xf, o_nsa, u, vn, ga, gb, gm_ws.astype(F32), bs_tile, w["wbn"], w, tm=512, name="merge_prompt")
    y = _ffn(x1, ple.reshape(n, PLE_DIM), w, tm=512, name="ffn_prompt")

    def rows_form(at):
        return at.reshape(bsz, 2, KV_HEADS, HEAD_DIM, at.shape[-1]).transpose(0, 4, 1, 2, 3)

    return (y.reshape(bsz, seq, D_MODEL), rows_form(cmpt), rows_form(slct),
            rows_form(wint[:, :, seq - min(WINDOW, seq):]))


def _sample_layer(x, ple, w, rel_bias, gm_ws, gm_bs, cache_cmp, cache_slc, win_buf, page_table):
    dbsz, nq, _ = x.shape
    assert nq == 1
    n_pool = cache_cmp.shape[0]
    n_pages = page_table.shape[1]
    past = n_pages * PAGE
    n_win = win_buf.shape[1]
    assert past % GM_CHUNK == 0 and n_win == WINDOW and past >= WINDOW and dbsz % 8 == 0
    xf = x.reshape(dbsz, D_MODEL)
    q, cmp, slc, win, gate, u, vn, ga, gb = _inproj(xf, w, prompt=False, seq=1, tm=dbsz)

    pps = min(16, n_pages)
    hbuf = _compress_paged(_pages_t(cache_cmp), page_table, w, pps=pps)
    n_sub = past // CMP_STRIDE
    qblk = past // SEL_LEN
    n_sel_pad = -(-(qblk + 1) // LANE) * LANE
    sh = _shifted_bias(rel_bias)
    i = np.arange(n_sub)
    d_c = past - (CMP_STRIDE * i + CMP_LEN - 1)
    tcs = _sample_table(sh, d_c, i < n_sub - 1)
    ovs = jnp.asarray(_overlap_t(n_sub, n_sel_pad).T)
    q8 = q.reshape(dbsz, N_HEADS, LANE)
    oc, idx = _nsa_sample_cmp(hbuf, q8, w, tcs, ovs, qblk=qblk)
    idx2 = idx[:, 0:KV_HEADS, 0:N_SEL_BLOCKS].reshape(dbsz, KV_HEADS * N_SEL_BLOCKS)

    l = np.arange(PAGE)
    tnear = _sample_table(sh, PAGE - l, l >= 0)
    b0 = jnp.broadcast_to(sh[0][:, None], (N_HEADS, LANE)).astype(F32)
    kpos = np.arange(n_win)
    tws = _sample_table(sh, n_win - kpos, kpos >= 0)
    gs = gate[:, 0:3 * N_HEADS].reshape(dbsz, N_HEADS, 3).transpose(0, 2, 1)
    gs = jnp.broadcast_to(gs[..., None], (dbsz, 3, N_HEADS, LANE))
    o8, win_out = _nsa_sample_sel(
        idx2, page_table, _pages_t(cache_slc), q8,
        jnp.broadcast_to(slc[:, None, :], (dbsz, 8, 256)), jnp.broadcast_to(win[:, None, :], (dbsz, 8, 256)),
        _pages_t(win_buf), gs, oc, tnear, b0, tws, qblk=qblk)
    win_out = win_out.reshape(dbsz, 2, KV_HEADS, HEAD_DIM, n_win).transpose(0, 4, 1, 2, 3)

    ws_diag = gm_ws[:, 0, 0][:, None, None] * jnp.eye(dbsz, dtype=F32)[None]
    bs_tile = jnp.broadcast_to(jnp.repeat(gm_bs[:, 0], LANE)[None, :], (dbsz, GM_GROUPS * LANE)).astype(F32)
    x1 = _merge(xf, o8, u, vn, ga, gb, ws_diag, bs_tile, w["wbn_heads"], w, tm=dbsz, name="merge_sample")
    y = _ffn(x1, ple.reshape(dbsz, PLE_DIM), w, tm=dbsz, name="ffn_sample")
    kvshape = (dbsz, 1, 2, KV_HEADS, HEAD_DIM)
    return (y.reshape(dbsz, 1, D_MODEL), cmp.reshape(kvshape), slc.reshape(kvshape),
            win_out, vn.reshape(dbsz, 1, GM_DIM))


def kernel(x_prompt, x_sample, cache_cmp_kv, cache_slc_kv, state_win_kv, page_table, p_prompt, p_sample, rel_bias,
           ln_mix_pre, w_in, cmp_pe_k, cmp_w1_k, cmp_w2_k, cmp_pe_v, cmp_w1_v, cmp_w2_v, gm_ln_g, gm_ln_b, gm_ws,
           gm_bs, w_branch_nsa, w_branch_gm, w_out, ln_mix_post, ln_ffn_pre, w_ffn_gate, w_ffn_up, w_ffn_down,
           ln_ffn_post, w_ple, w_ple_gate, ln_ple_post):
    depth = w_in.shape[0]
    xp, xs = x_prompt, x_sample
    outs = [[] for _ in range(7)]
    for i in range(depth):
        w = _prep_weights(i, ln_mix_pre, w_in, cmp_pe_k, cmp_w1_k, cmp_w2_k, cmp_pe_v, cmp_w1_v, cmp_w2_v, gm_ln_g,
                          gm_ln_b, w_branch_nsa, w_branch_gm, w_out, ln_mix_post, ln_ffn_pre, w_ffn_gate, w_ffn_up,
                          w_ffn_down, ln_ffn_post, w_ple, w_ple_gate, ln_ple_post)
        xp, c_p, s_p, w_p = _prompt_layer(xp, p_prompt[i], w, rel_bias, gm_ws[i], gm_bs[i])
        xs, c_s, s_s, w_s, v_s = _sample_layer(xs, p_sample[i], w, rel_bias, gm_ws[i], gm_bs[i], cache_cmp_kv[i],
                                               cache_slc_kv[i], state_win_kv[i], page_table)
        for lst, val in zip(outs, (c_p, s_p, w_p, c_s, s_s, w_s, v_s)):
            lst.append(val)
    return (xp, xs) + tuple(jnp.stack(o) for o in outs)
```

```python
import functools
import math

import jax
import jax.numpy as jnp
import numpy as np
from jax import lax
from jax.experimental import pallas as pl
from jax.experimental.pallas import tpu as pltpu

F32 = jnp.float32
BF16 = jnp.bfloat16

D_MODEL = 1024
N_HEADS = 8
KV_HEADS = 2
HPG = N_HEADS // KV_HEADS
HEAD_DIM = 64
CMP_LEN = 32
CMP_STRIDE = 16
CMP_HID = 2 * HEAD_DIM
SEL_LEN = 64
N_SEL_BLOCKS = 16
WINDOW = 512
Q_BLOCK = 128
GM_GROUPS = 4
GM_CHUNK = 128
GM_DIM = D_MODEL // 2
D_FF = -(-8 * D_MODEL // (3 * 256)) * 256
PLE_DIM = 256
N_BUCKETS = 32
MAX_DISTANCE = 128
EPS = 1e-6
NEG = -1e30
LOG2E = 1.4426950408889634
FORCE = 1e6
PAGE = 128

LANE = 128
NJ = 128
SEL_TILE = 256
FF_CHUNK = 256
VMEM_LIMIT = 56 * 1024 * 1024


def _dot(a, b):
    return jnp.dot(a, b, preferred_element_type=F32)


def _dot_nt(a, b):
    return lax.dot_general(a, b, (((1,), (1,)), ((), ())), preferred_element_type=F32)


def _dot_f32(a, b):
    return jnp.dot(a, b, preferred_element_type=F32, precision=lax.Precision.HIGHEST)


def _rms(x, g):
    return x * lax.rsqrt(jnp.mean(x * x, axis=-1, keepdims=True) + EPS) * g


def _params(sem):
    return pltpu.CompilerParams(dimension_semantics=sem, vmem_limit_bytes=VMEM_LIMIT)


def _t5_bucket_np(d):
    d = np.maximum(d, 0)
    max_exact = N_BUCKETS // 2
    ratio = (np.log(np.maximum(d, 1).astype(np.float32) / np.float32(max_exact))
             / np.float32(math.log(MAX_DISTANCE / max_exact)))
    large = np.minimum(max_exact + (ratio * np.float32(N_BUCKETS - max_exact)).astype(np.int32), N_BUCKETS - 1)
    return np.where(d < max_exact, d, large)


_BUCKET = _t5_bucket_np(np.arange(MAX_DISTANCE + 1))
assert _BUCKET[MAX_DISTANCE] == N_BUCKETS - 1


def _shifted_bias(rel_bias):
    return rel_bias[_BUCKET] - rel_bias[N_BUCKETS - 1][None, :]


def _sample_table(sh, dist, valid):
    t = sh[np.minimum(np.maximum(dist, 0), MAX_DISTANCE)] * LOG2E
    return jnp.where(valid[:, None], t, NEG).T.astype(F32)


_BUCKET_START = [int(np.argmax(_BUCKET >= k)) for k in range(N_BUCKETS)]
_BAND_ROWS = 3 * (Q_BLOCK // CMP_STRIDE)


def _bias_tables_kernel(rb_ref, tc_ref, ts_ref, tw_ref):
    def fill(store, rows, chunk, c0, stride, dmax):
        def body(c, carry):
            r0 = pl.multiple_of(c * chunk, chunk)
            r = r0 + lax.broadcasted_iota(jnp.int32, (chunk, Q_BLOCK), 0)
            d = c0 + lax.broadcasted_iota(jnp.int32, (chunk, Q_BLOCK), 1) - stride * r
            ok = (d >= 0) & (d <= dmax)
            dc = jnp.minimum(d, MAX_DISTANCE)
            for hd in range(N_HEADS):
                v = jnp.full((chunk, Q_BLOCK), rb_ref[0, hd], F32)
                for k in range(1, N_BUCKETS):
                    v = jnp.where(dc >= _BUCKET_START[k], rb_ref[k, hd], v)
                v = (v - rb_ref[N_BUCKETS - 1, hd]) * LOG2E
                store(hd // HPG, r0, chunk, hd % HPG, jnp.where(ok, v, NEG))
            return carry
        lax.fori_loop(0, rows // chunk, body, 0)

    def lanes(hh):
        return slice(hh * Q_BLOCK, (hh + 1) * Q_BLOCK)

    def st_c(g, r0, n, hh, v):
        tc_ref[g, pl.ds(r0, n), lanes(hh)] = v

    def st_w(g, r0, n, hh, v):
        tw_ref[g, pl.ds(r0, n), lanes(hh)] = v

    big = 1 << 30
    fill(st_c, _BAND_ROWS, 8, 2 * Q_BLOCK - (CMP_LEN - 1), CMP_STRIDE, big)
    for par in range(SEL_TILE // Q_BLOCK):
        def st_s(g, r0, n, hh, v, par=par):
            ts_ref[g, par, pl.ds(r0, n), lanes(hh)] = v
        fill(st_s, 2 * SEL_TILE, 64, SEL_TILE + Q_BLOCK * par, 1, big)
    fill(st_w, WINDOW + Q_BLOCK, 64, WINDOW, 1, WINDOW)


def _bias_tables(rel_bias):
    nl = HPG * Q_BLOCK
    return pl.pallas_call(
        _bias_tables_kernel,
        in_specs=[pl.BlockSpec(memory_space=pltpu.SMEM)],
        out_shape=[jax.ShapeDtypeStruct((KV_HEADS, _BAND_ROWS, nl), F32),
                   jax.ShapeDtypeStruct((KV_HEADS, SEL_TILE // Q_BLOCK, 2 * SEL_TILE, nl), F32),
                   jax.ShapeDtypeStruct((KV_HEADS, WINDOW + Q_BLOCK, nl), F32)],
        compiler_params=pltpu.CompilerParams(vmem_limit_bytes=VMEM_LIMIT),
        name="bias_tables",
    )(rel_bias.astype(F32))


def _overlap_t(n_cmp, n_sel_pad):
    ci = np.arange(n_cmp)[None, :] * CMP_STRIDE
    sj = np.arange(n_sel_pad)[:, None] * SEL_LEN
    return ((ci <= sj + SEL_LEN - 1) & (ci + CMP_LEN - 1 >= sj)).astype(np.float32)


def _overlap_offsets():
    per = SEL_LEN // CMP_STRIDE
    ov = _overlap_t(16 * per, 16)
    jj, ii = np.nonzero(ov)
    offs = sorted({int(i - per * j) for j, i in zip(jj, ii)})
    band = np.isin(np.arange(16 * per)[None, :] - per * np.arange(16)[:, None], offs)
    assert (band == (ov > 0)).all() and min(offs) >= -8
    return offs


_OVERLAP_OFFSETS = _overlap_offsets()


def _inproj_kernel(x_ref, lng_ref, wq_ref, wkv_ref, wg_ref, wuv_ref, wmg_ref, glg_ref, glb_ref, *outs,
                   prompt, seq, tm):
    h = _rms(x_ref[...], lng_ref[...]).astype(BF16)
    q = _dot(h, wq_ref[...]) * (HEAD_DIM ** -0.5 * LOG2E)
    kv = _dot(h, wkv_ref[...])
    gate = jax.nn.sigmoid(_dot(h, wg_ref[...]))
    uv = jax.nn.gelu(_dot(h, wuv_ref[...]))
    u = uv[:, :GM_DIM]
    v = uv[:, GM_DIM:]
    mu = jnp.mean(v, axis=-1, keepdims=True)
    var = jnp.mean(jnp.square(v - mu), axis=-1, keepdims=True)
    vn = (v - mu) * lax.rsqrt(var + EPS) * glg_ref[...] + glb_ref[...]
    mg = jax.nn.sigmoid(_dot(h, wmg_ref[...]))
    cmp, slc, win = kv[:, 0:256], kv[:, 256:512], kv[:, 512:768]
    if prompt:
        (q_ref, cmp_ref, cmpt_ref, slct_ref, wint_ref, kaug_ref, vts_ref, kwin_ref, vtw_ref, gt_ref,
         u_ref, vn_ref, ga_ref, gb_ref) = outs
        for hh in range(N_HEADS):
            q_ref[hh] = q[:, hh * LANE:(hh + 1) * LANE].astype(BF16)
        pos = (pl.program_id(0) * tm + lax.broadcasted_iota(jnp.int32, (tm, NJ), 0)) % seq
        onehot = (lax.broadcasted_iota(jnp.int32, (tm, NJ), 1) == pos // SEL_LEN)
        kaug_ref[:, 0:NJ] = onehot.astype(BF16)
        kaug_ref[:, NJ:NJ + LANE] = slc[:, 0:LANE].astype(BF16)
        kwin_ref[...] = win[:, 0:LANE].astype(BF16)
        slc_t = slc.T
        win_t = win.T
        for g in range(KV_HEADS):
            r0 = LANE + HEAD_DIM * g
            for c in range(tm // SEL_TILE):
                vts_ref[g, c] = slc_t[r0:r0 + HEAD_DIM, c * SEL_TILE:(c + 1) * SEL_TILE].astype(BF16)
            for c in range(tm // LANE):
                vtw_ref[g, c] = win_t[r0:r0 + HEAD_DIM, c * LANE:(c + 1) * LANE].astype(BF16)
        gt_ref[...] = gate.T[0:32, :]
        cmpt_ref[...] = cmp.T
        slct_ref[...] = slc_t
        wint_ref[...] = win_t
    else:
        q_ref, cmp_ref, slc_ref, win_ref, gate_ref, u_ref, vn_ref, ga_ref, gb_ref = outs
        q_ref[...] = q
        gate_ref[...] = gate
        slc_ref[...] = slc
        win_ref[...] = win
    cmp_ref[...] = cmp
    u_ref[...] = u.astype(u_ref.dtype)
    vn_ref[...] = vn.astype(vn_ref.dtype)
    ga_ref[...] = mg[:, :D_MODEL].astype(ga_ref.dtype)
    gb_ref[...] = mg[:, D_MODEL:].astype(gb_ref.dtype)


def _inproj(x, w, *, prompt, seq, tm):
    n = x.shape[0]
    row = lambda i: (i, 0)
    const = lambda i: (0, 0)
    in_specs = [pl.BlockSpec((tm, D_MODEL), row), pl.BlockSpec((1, D_MODEL), const),
                pl.BlockSpec(w["wq"].shape, const), pl.BlockSpec(w["wkv"].shape, const),
                pl.BlockSpec(w["wg"].shape, const), pl.BlockSpec(w["wuv"].shape, const),
                pl.BlockSpec(w["wmg"].shape, const), pl.BlockSpec((1, GM_DIM), const),
                pl.BlockSpec((1, GM_DIM), const)]
    kv_shapes = [jax.ShapeDtypeStruct((n, 256), F32)] * 3
    kv_specs = [pl.BlockSpec((tm, 256), row)] * 3
    if prompt:
        act = BF16
        tiles = seq // tm
        kvt_shapes = [jax.ShapeDtypeStruct((n // seq, 256, seq), F32)] * 3
        kvt_specs = [pl.BlockSpec((None, 256, tm), lambda i: (i // tiles, 0, i % tiles))] * 3
        out_shape = ([jax.ShapeDtypeStruct((N_HEADS, n, LANE), BF16)] + kv_shapes[:1] + kvt_shapes + [
            jax.ShapeDtypeStruct((n, NJ + LANE), BF16),
            jax.ShapeDtypeStruct((KV_HEADS, n // SEL_TILE, HEAD_DIM, SEL_TILE), BF16),
            jax.ShapeDtypeStruct((n, LANE), BF16),
            jax.ShapeDtypeStruct((KV_HEADS, n // LANE, HEAD_DIM, LANE), BF16),
            jax.ShapeDtypeStruct((32, n), F32)])
        out_specs = ([pl.BlockSpec((N_HEADS, tm, LANE), lambda i: (0, i, 0))] + kv_specs[:1] + kvt_specs + [
            pl.BlockSpec((tm, NJ + LANE), row),
            pl.BlockSpec((KV_HEADS, tm // SEL_TILE, HEAD_DIM, SEL_TILE), lambda i: (0, i, 0, 0)),
            pl.BlockSpec((tm, LANE), row),
            pl.BlockSpec((KV_HEADS, tm // LANE, HEAD_DIM, LANE), lambda i: (0, i, 0, 0)),
            pl.BlockSpec((32, tm), lambda i: (0, i))])
    else:
        act = F32
        out_shape = ([jax.ShapeDtypeStruct((n, N_HEADS * LANE), F32)] + kv_shapes
                     + [jax.ShapeDtypeStruct((n, LANE), F32)])
        out_specs = [pl.BlockSpec((tm, N_HEADS * LANE), row)] + kv_specs + [pl.BlockSpec((tm, LANE), row)]
    out_shape += [jax.ShapeDtypeStruct((n, GM_DIM), act)] * 2 + [jax.ShapeDtypeStruct((n, D_MODEL), act)] * 2
    out_specs += [pl.BlockSpec((tm, GM_DIM), row)] * 2 + [pl.BlockSpec((tm, D_MODEL), row)] * 2
    return pl.pallas_call(
        functools.partial(_inproj_kernel, prompt=prompt, seq=seq, tm=tm),
        grid=(n // tm,), in_specs=in_specs, out_specs=out_specs, out_shape=out_shape,
        compiler_params=_params(("parallel",)),
        name="inproj_prompt" if prompt else "inproj_sample",
    )(x, w["ln_mix_pre"], w["wq"], w["wkv"], w["wg"], w["wuv"], w["wmg"], w["gm_ln_g"], w["gm_ln_b"])


def _compress_finish(hk, hv, pek_ref, w1k_ref, pev_ref, w1v_ref, w2ka_ref, w2kb_ref, n_sub):
    pwk = _dot(pek_ref[...].astype(BF16), w1k_ref[...])[0:1, :]
    pwv = _dot(pev_ref[...].astype(BF16), w1v_ref[...])[0:1, :]

    def act(hh, g, pw):
        a0 = hh[:, 256 * g:256 * g + CMP_HID]
        a1 = hh[:, 256 * g + CMP_HID:256 * g + 2 * CMP_HID]
        return jax.nn.silu(a0 + pltpu.roll(a1, n_sub - 1, 0) + pw).astype(BF16)

    return [act(hk, 0, pwk), act(hk, 1, pwk)], [act(hv, 0, pwv), act(hv, 1, pwv)]


def _compress_prompt_kernel(cmpk_ref, cmpv_ref, wbk_ref, wbv_ref, pek_ref, w1k_ref, pev_ref, w1v_ref,
                            w2ka_ref, w2kb_ref, w2vt_ref, kc_ref, vct_ref, *, n_sub):
    hk = jnp.zeros((n_sub, 4 * CMP_HID), F32)
    hv = jnp.zeros((n_sub, 4 * CMP_HID), F32)
    for r in range(CMP_STRIDE):
        hk += _dot(cmpk_ref[pl.ds(r, n_sub, stride=CMP_STRIDE), :].astype(BF16), wbk_ref[r])
        hv += _dot(cmpv_ref[pl.ds(r, n_sub, stride=CMP_STRIDE), :].astype(BF16), wbv_ref[r])
    ak, av = _compress_finish(hk, hv, pek_ref, w1k_ref, pev_ref, w1v_ref, w2ka_ref, w2kb_ref, n_sub)
    kc_ref[...] = (_dot(ak[0], w2ka_ref[...]) + _dot(ak[1], w2kb_ref[...])).astype(BF16)
    for g in range(KV_HEADS):
        vct_ref[g] = _dot_nt(w2vt_ref[...], av[g]).astype(BF16)


def _compress_prompt(cmp, w, *, bsz, seq):
    n_sub = seq // CMP_STRIDE
    const2 = lambda b: (0, 0)
    const3 = lambda b: (0, 0, 0)
    return pl.pallas_call(
        functools.partial(_compress_prompt_kernel, n_sub=n_sub),
        grid=(bsz,),
        in_specs=[pl.BlockSpec((seq, LANE), lambda b: (b, 0)), pl.BlockSpec((seq, LANE), lambda b: (b, 1)),
                  pl.BlockSpec(w["wbk"].shape, const3), pl.BlockSpec(w["wbv"].shape, const3),
                  pl.BlockSpec(w["pek"].shape, const2), pl.BlockSpec(w["w1k"].shape, const2),
                  pl.BlockSpec(w["pev"].shape, const2), pl.BlockSpec(w["w1v"].shape, const2),
                  pl.BlockSpec(w["w2ka"].shape, const2), pl.BlockSpec(w["w2kb"].shape, const2),
                  pl.BlockSpec(w["w2vt"].shape, const2)],
        out_specs=[pl.BlockSpec((None, n_sub, LANE), lambda b: (b, 0, 0)),
                   pl.BlockSpec((None, KV_HEADS, HEAD_DIM, n_sub), lambda b: (b, 0, 0, 0))],
        out_shape=[jax.ShapeDtypeStruct((bsz, n_sub, LANE), BF16),
                   jax.ShapeDtypeStruct((bsz, KV_HEADS, HEAD_DIM, n_sub), BF16)],
        compiler_params=_params(("parallel",)),
        name="compress_prompt",
    )(cmp, cmp, w["wbk"], w["wbv"], w["pek"], w["w1k"], w["pev"], w["w1v"], w["w2ka"], w["w2kb"], w["w2vt"])


def _compress_paged_kernel(pt_ref, *refs, pps):
    kpages, vpages = refs[:pps], refs[pps:2 * pps]
    wbk_ref, wbv_ref, h_ref, xk_ref, xv_ref = refs[2 * pps:]
    sub_per_page = PAGE // CMP_STRIDE
    n = pps * sub_per_page
    for k in range(pps):
        xk_ref[k * PAGE:(k + 1) * PAGE, :] = kpages[k][...].T
        xv_ref[k * PAGE:(k + 1) * PAGE, :] = vpages[k][...].T
    hk = jnp.zeros((n, 4 * CMP_HID), F32)
    hv = jnp.zeros((n, 4 * CMP_HID), F32)
    for r in range(CMP_STRIDE):
        hk += _dot(xk_ref[pl.ds(r, n, stride=CMP_STRIDE), :].astype(BF16), wbk_ref[r])
        hv += _dot(xv_ref[pl.ds(r, n, stride=CMP_STRIDE), :].astype(BF16), wbv_ref[r])
    h_ref[:, 0:4 * CMP_HID] = hk
    h_ref[:, 4 * CMP_HID:8 * CMP_HID] = hv


def _compress_paged(cache_t, page_table, w, *, pps):
    dbsz, n_pages = page_table.shape
    sub_per_page = PAGE // CMP_STRIDE
    n_sub = n_pages * sub_per_page

    def page_spec(k, c):
        return pl.BlockSpec((None, None, LANE, PAGE), lambda b, t, pt: (pt[b, t * pps + k], c, 0, 0))

    const3 = lambda b, t, pt: (0, 0, 0)
    grid_spec = pltpu.PrefetchScalarGridSpec(
        num_scalar_prefetch=1, grid=(dbsz, n_pages // pps),
        in_specs=[page_spec(k, c) for c in range(2) for k in range(pps)]
        + [pl.BlockSpec(w["wbk"].shape, const3), pl.BlockSpec(w["wbv"].shape, const3)],
        out_specs=pl.BlockSpec((None, pps * sub_per_page, 8 * CMP_HID), lambda b, t, pt: (b, t, 0)),
        scratch_shapes=[pltpu.VMEM((pps * PAGE, LANE), F32), pltpu.VMEM((pps * PAGE, LANE), F32)])
    return pl.pallas_call(
        functools.partial(_compress_paged_kernel, pps=pps),
        grid_spec=grid_spec,
        out_shape=jax.ShapeDtypeStruct((dbsz, n_sub, 8 * CMP_HID), F32),
        compiler_params=_params(("parallel", "arbitrary")),
        name="compress_paged",
    )(page_table, *([cache_t] * (2 * pps)), w["wbk"], w["wbv"])


def _pages_t(cache):
    n, npos = cache.shape[0], cache.shape[1]
    return cache.transpose(0, 2, 3, 4, 1).reshape(n, 2, KV_HEADS * HEAD_DIM, npos)


def _col_softmax(s):
    m = jnp.max(s, axis=0, keepdims=True)
    p = jnp.exp2(s - m)
    return p * jnp.where(m > 0.5 * NEG, 1.0 / jnp.sum(p, axis=0, keepdims=True), 0.0)


def _select_blocks(score, jio, axis):
    selneg = jnp.full(score.shape, NEG, F32)
    picked = []
    for _ in range(N_SEL_BLOCKS):
        mx = jnp.max(score, axis=axis, keepdims=True)
        jm = jnp.min(jnp.where(score == mx, jio, 1 << 20), axis=axis, keepdims=True)
        pick = jio == jm
        selneg = jnp.where(pick, 0.0, selneg)
        score = jnp.where(pick, -3e38, score)
        picked.append(jm)
    return selneg, picked


def _nsa_prompt_kernel(q_ref, kc_ref, vct_ref, kaug_ref, vts_ref, kwin_ref, vtw_ref, gt_ref,
                       tc_ref, ts_ref, tw_ref, o_ref,
                       sc_ref, ps_ref, qaug_ref, m_ref, l_ref, acc_ref, *, n_cmp):
    g = pl.program_id(1)
    qb = pl.program_id(2)
    nl = HPG * Q_BLOCK
    q2 = q_ref[...].reshape(nl, LANE)

    pad = 2 * (Q_BLOCK // CMP_STRIDE)
    band = 3 * (Q_BLOCK // CMP_STRIDE)
    sc_ref[0:pad, :] = jnp.zeros((pad, nl), F32)
    sc_ref[pad:pad + n_cmp, :] = _dot_nt(kc_ref[...], q2)
    w0 = pl.multiple_of(qb * (Q_BLOCK // CMP_STRIDE), 8)
    sc_ref[pl.ds(w0, band), :] += tc_ref[...]
    sc = sc_ref[pad:pad + n_cmp, :]
    row = lax.broadcasted_iota(jnp.int32, (n_cmp, nl), 0)
    sc = jnp.where(row < (qb + 1) * (Q_BLOCK // CMP_STRIDE), sc, NEG)
    pc = _col_softmax(sc)
    o_c = _dot(vct_ref[...], pc.astype(BF16))

    n_wt = WINDOW // Q_BLOCK + 1
    tix = [jnp.maximum(qb - (n_wt - 1) + t, 0) for t in range(n_wt)]
    kw = jnp.concatenate([kwin_ref[pl.ds(pl.multiple_of(t * Q_BLOCK, Q_BLOCK), Q_BLOCK), :] for t in tix], axis=0)
    sw = _dot_nt(kw, q2) + tw_ref[...]
    wrow = lax.broadcasted_iota(jnp.int32, (n_wt * Q_BLOCK, nl), 0) + (qb - (n_wt - 1)) * Q_BLOCK
    pw = _col_softmax(jnp.where(wrow >= 0, sw, NEG))
    vw = jnp.concatenate([vtw_ref[t] for t in tix], axis=1)
    o_w = _dot(vw, pw.astype(BF16))

    psum = pc[:, 0:Q_BLOCK]
    for hh in range(1, HPG):
        psum = psum + pc[:, hh * Q_BLOCK:(hh + 1) * Q_BLOCK]
    per_sel = SEL_LEN // CMP_STRIDE
    n_sel = n_cmp // per_sel
    ps_ref[0:8, :] = jnp.zeros((8, Q_BLOCK), F32)
    ps_ref[8:8 + n_cmp, :] = psum
    imp_t = ps_ref[pl.ds(8 + _OVERLAP_OFFSETS[0], n_sel, stride=per_sel), :]
    for off in _OVERLAP_OFFSETS[1:]:
        imp_t = imp_t + ps_ref[pl.ds(8 + off, n_sel, stride=per_sel), :]
    if n_sel < NJ:
        imp_t = jnp.concatenate([imp_t, jnp.zeros((NJ - n_sel, Q_BLOCK), F32)], axis=0)
    jio = lax.broadcasted_iota(jnp.int32, (NJ, Q_BLOCK), 0)
    qi = lax.broadcasted_iota(jnp.int32, (NJ, Q_BLOCK), 1)
    qblk = qb * (Q_BLOCK // SEL_LEN) + qi // SEL_LEN
    forced = (jio == 0) | (jio == qblk) | (jio == qblk - 1)
    score = jnp.where(forced, FORCE, jnp.where(jio <= qblk, imp_t, -FORCE))
    selneg_t, _ = _select_blocks(score, jio, 0)
    selneg = selneg_t.T.astype(BF16)
    for hh in range(HPG):
        qaug_ref[hh * Q_BLOCK:(hh + 1) * Q_BLOCK, 0:NJ] = selneg
    qaug_ref[:, NJ:NJ + LANE] = q2

    m_ref[...] = jnp.full((1, nl), -1e38, F32)
    l_ref[...] = jnp.zeros((1, nl), F32)
    acc_ref[...] = jnp.zeros((HEAD_DIM, nl), F32)

    def tiles(u0, ntile, table):
        rows = ntile * SEL_TILE
        ka = kaug_ref[pl.ds(pl.multiple_of(u0 * SEL_TILE, SEL_TILE), rows), :]
        s = _dot_nt(ka, qaug_ref[...])
        if table is not None:
            s = s + table
        m_old = m_ref[...]
        m_new = jnp.maximum(m_old, jnp.max(s, axis=0, keepdims=True))
        alpha = jnp.exp2(m_old - m_new)
        p = jnp.exp2(s - m_new)
        l_ref[...] = alpha * l_ref[...] + jnp.sum(p, axis=0, keepdims=True)
        p = p.astype(BF16)
        pv = _dot(vts_ref[u0], p[0:SEL_TILE])
        for t in range(1, ntile):
            pv = pv + _dot(vts_ref[u0 + t], p[t * SEL_TILE:(t + 1) * SEL_TILE])
        acc_ref[...] = alpha * acc_ref[...] + pv
        m_ref[...] = m_new

    tiles_per_q = SEL_TILE // Q_BLOCK
    u_last = qb // tiles_per_q
    par = qb % tiles_per_q
    n_far = jnp.maximum(u_last - 1, 0)

    def far(i, carry):
        tiles(2 * i, 2, None)
        return carry

    lax.fori_loop(0, n_far // 2, far, 0)

    @pl.when(n_far % 2 == 1)
    def _():
        tiles(n_far - 1, 1, None)

    @pl.when(u_last >= 1)
    def _():
        tiles(u_last - 1, 2, ts_ref[par])

    @pl.when(u_last == 0)
    def _():
        tiles(0, 1, ts_ref[par, SEL_TILE:2 * SEL_TILE, :])

    o_s = acc_ref[...] / l_ref[...]

    def gate_row(br):
        rows = gt_ref[br * N_HEADS:(br + 1) * N_HEADS, :]
        rows = jnp.where(g == 0, rows[0:HPG], rows[HPG:2 * HPG])
        return jnp.concatenate([rows[hh:hh + 1, :] for hh in range(HPG)], axis=1)

    o_t = gate_row(0) * o_c + gate_row(1) * o_s + gate_row(2) * o_w
    stack = jnp.concatenate([o_t[:, hh * Q_BLOCK:(hh + 1) * Q_BLOCK] for hh in range(HPG)], axis=0)
    o_ref[...] = stack.T.astype(o_ref.dtype)


def _nsa_prompt(qhm, kc, vct, kaug, vts, kwin, vtw, gt, tabs, *, bsz, seq):
    n = bsz * seq
    nqb = seq // Q_BLOCK
    n_cmp = seq // CMP_STRIDE
    nl = HPG * Q_BLOCK
    tc, ts, tw = tabs
    in_specs = [
        pl.BlockSpec((HPG, Q_BLOCK, LANE), lambda b, g, i: (g, b * nqb + i, 0)),
        pl.BlockSpec((None, n_cmp, LANE), lambda b, g, i: (b, 0, 0)),
        pl.BlockSpec((None, None, HEAD_DIM, n_cmp), lambda b, g, i: (b, g, 0, 0)),
        pl.BlockSpec((seq, NJ + LANE), lambda b, g, i: (b, 0)),
        pl.BlockSpec((None, seq // SEL_TILE, HEAD_DIM, SEL_TILE), lambda b, g, i: (g, b, 0, 0)),
        pl.BlockSpec((seq, LANE), lambda b, g, i: (b, 0)),
        pl.BlockSpec((None, seq // LANE, HEAD_DIM, LANE), lambda b, g, i: (g, b, 0, 0)),
        pl.BlockSpec((32, Q_BLOCK), lambda b, g, i: (0, b * nqb + i)),
        pl.BlockSpec((None,) + tc.shape[1:], lambda b, g, i: (g, 0, 0)),
        pl.BlockSpec((None,) + ts.shape[1:], lambda b, g, i: (g, 0, 0, 0)),
        pl.BlockSpec((None,) + tw.shape[1:], lambda b, g, i: (g, 0, 0)),
    ]
    pad = 2 * (Q_BLOCK // CMP_STRIDE)
    return pl.pallas_call(
        functools.partial(_nsa_prompt_kernel, n_cmp=n_cmp),
        grid=(bsz, KV_HEADS, nqb),
        in_specs=in_specs,
        out_specs=pl.BlockSpec((Q_BLOCK, HPG * HEAD_DIM), lambda b, g, i: (b * nqb + i, g)),
        out_shape=jax.ShapeDtypeStruct((n, N_HEADS * HEAD_DIM), BF16),
        scratch_shapes=[pltpu.VMEM((pad + n_cmp + pad, nl), F32), pltpu.VMEM((8 + n_cmp, Q_BLOCK), F32),
                        pltpu.VMEM((nl, NJ + LANE), BF16),
                        pltpu.VMEM((1, nl), F32), pltpu.VMEM((1, nl), F32), pltpu.VMEM((HEAD_DIM, nl), F32)],
        compiler_params=_params(("parallel", "parallel", "arbitrary")),
        name="nsa_prompt",
    )(qhm, kc, vct, kaug, vts, kwin, vtw, gt, tc, ts, tw)


def _row_softmax_parts(s, s_new):
    m = jnp.maximum(jnp.max(s, axis=1, keepdims=True), s_new)
    p = jnp.exp2(s - m)
    pn = jnp.exp2(s_new - m)
    return p, pn, jnp.sum(p, axis=1, keepdims=True) + pn


def _nsa_sample_cmp_kernel(h_ref, pek_ref, w1k_ref, pev_ref, w1v_ref, w2ka_ref, w2kb_ref, w2va_ref, w2vb_ref,
                           q_ref, tcs_ref, ovs_ref, oc_ref, idx_ref, *, n_sub, qblk):
    hh = h_ref[...]
    ak, av = _compress_finish(hh[:, 0:4 * CMP_HID], hh[:, 4 * CMP_HID:8 * CMP_HID],
                              pek_ref, w1k_ref, pev_ref, w1v_ref, w2ka_ref, w2kb_ref, n_sub)
    kc = (_dot(ak[0], w2ka_ref[...]) + _dot(ak[1], w2kb_ref[...])).astype(BF16)
    vc = (_dot(av[0], w2va_ref[...]) + _dot(av[1], w2vb_ref[...])).astype(BF16)
    q8 = q_ref[...].astype(BF16)
    s = _dot_nt(q8, kc) + tcs_ref[...]
    m = jnp.max(s, axis=1, keepdims=True)
    p = jnp.exp2(s - m)
    pc = p / jnp.sum(p, axis=1, keepdims=True)
    oc_ref[...] = _dot(pc.astype(BF16), vc)
    hrow = lax.broadcasted_iota(jnp.int32, (N_HEADS, n_sub), 0)
    rows = []
    for g in range(KV_HEADS):
        rows.append(jnp.sum(jnp.where(hrow // HPG == g, pc, 0.0), axis=0, keepdims=True))
    psum = jnp.concatenate(rows + [jnp.zeros((N_HEADS - KV_HEADS, n_sub), F32)], axis=0)
    imp = _dot_f32(psum, ovs_ref[...])
    jl = lax.broadcasted_iota(jnp.int32, imp.shape, 1)
    forced = (jl == 0) | (jl == qblk) | (jl == qblk - 1)
    score = jnp.where(forced, FORCE, jnp.where(jl <= qblk, imp, -FORCE))
    _, picked = _select_blocks(score, jl, 1)
    lane = lax.broadcasted_iota(jnp.int32, (N_HEADS, LANE), 1)
    out = jnp.zeros((N_HEADS, LANE), jnp.int32)
    for r, jm in enumerate(picked):
        out = jnp.where(lane == r, jm, out)
    idx_ref[...] = out


def _nsa_sample_cmp(hbuf, q8, w, tcs, ovs, *, qblk):
    dbsz, n_sub, _ = hbuf.shape
    c2 = lambda b: (0, 0)
    names = ["pek", "w1k", "pev", "w1v", "w2ka", "w2kb", "w2va", "w2vb"]
    return pl.pallas_call(
        functools.partial(_nsa_sample_cmp_kernel, n_sub=n_sub, qblk=qblk),
        grid=(dbsz,),
        in_specs=[pl.BlockSpec((None, n_sub, 8 * CMP_HID), lambda b: (b, 0, 0))]
        + [pl.BlockSpec(w[k].shape, c2) for k in names]
        + [pl.BlockSpec((None, N_HEADS, LANE), lambda b: (b, 0, 0)),
           pl.BlockSpec(tcs.shape, c2), pl.BlockSpec(ovs.shape, c2)],
        out_specs=[pl.BlockSpec((None, N_HEADS, LANE), lambda b: (b, 0, 0)),
                   pl.BlockSpec((None, N_HEADS, LANE), lambda b: (b, 0, 0))],
        out_shape=[jax.ShapeDtypeStruct((dbsz, N_HEADS, LANE), F32),
                   jax.ShapeDtypeStruct((dbsz, N_HEADS, LANE), jnp.int32)],
        compiler_params=_params(("parallel",)),
        name="nsa_sample_cmp",
    )(hbuf, *[w[k] for k in names], q8, tcs, ovs)


def _nsa_sample_sel_kernel(idx_ref, pt_ref, *refs, qblk, n_win):
    nblk = KV_HEADS * N_SEL_BLOCKS
    blocks = refs[:nblk]
    (q_ref, slcn_ref, winn_ref, state_ref, gs_ref, oc_ref, tnear_ref, b0_ref, tws_ref,
     o_ref, wout_ref) = refs[nblk:]
    b = pl.program_id(0)
    q8f = q_ref[...]
    q8 = q8f.astype(BF16)
    hrow = lax.broadcasted_iota(jnp.int32, (N_HEADS, LANE), 0)
    b0 = b0_ref[:, 0:1]

    slcn = slcn_ref[0:1, :]
    s_new = jnp.sum(q8f * slcn[:, 0:LANE], axis=1, keepdims=True) + b0
    halves = PAGE // SEL_LEN
    lane = lax.broadcasted_iota(jnp.int32, (N_HEADS, PAGE), 1)
    o_sel = []
    for g in range(KV_HEADS):
        ss = []
        for r in range(N_SEL_BLOCKS):
            blk = idx_ref[b, g * N_SEL_BLOCKS + r]
            s = _dot(q8, blocks[g * N_SEL_BLOCKS + r][0].astype(BF16))
            near = jnp.where(blk // halves == qblk // halves - 1, tnear_ref[...], 0.0)
            ok = (blk < qblk) & (lane // SEL_LEN == blk % halves)
            ss.append(jnp.where(ok, s + near, NEG))
        s = jnp.concatenate(ss, axis=1)
        p, pn, l = _row_softmax_parts(s, s_new)
        o = pn * slcn[:, LANE:2 * LANE]
        for r in range(N_SEL_BLOCKS):
            pr = p[:, r * PAGE:(r + 1) * PAGE].astype(BF16)
            o = o + _dot_nt(pr, blocks[g * N_SEL_BLOCKS + r][1].astype(BF16))
        o_sel.append(o / l)
    o_s = jnp.where(hrow // HPG == 0, o_sel[0], o_sel[1])

    winn = winn_ref[0:1, :]
    sw = _dot(q8, state_ref[0].astype(BF16)) + tws_ref[...]
    sw_new = jnp.sum(q8f * winn[:, 0:LANE], axis=1, keepdims=True) + b0
    p, pn, l = _row_softmax_parts(sw, sw_new)
    o_w = (_dot_nt(p.astype(BF16), state_ref[1].astype(BF16)) + pn * winn[:, LANE:2 * LANE]) / l

    o_ref[...] = gs_ref[0] * oc_ref[...] + gs_ref[1] * o_s + gs_ref[2] * o_w
    wlane = lax.broadcasted_iota(jnp.int32, (LANE, n_win), 1)
    for kv in range(2):
        newcol = jnp.broadcast_to(winn[:, kv * LANE:(kv + 1) * LANE], (LANE, LANE)).T
        newcol = jnp.concatenate([newcol] * (n_win // LANE), axis=1)
        wout_ref[kv] = jnp.where(wlane == n_win - 1, newcol, pltpu.roll(state_ref[kv], n_win - 1, 1))


def _nsa_sample_sel(idx, page_table, slc_cache_t, q8, slc_new, win_new, state, gs, oc, tnear, b0, tws, *, qblk):
    dbsz, n_pages = page_table.shape
    n_win = state.shape[-1]
    halves = PAGE // SEL_LEN

    def blk_spec(k):
        def imap(b, idx_ref, pt_ref):
            j = idx_ref[b, k]
            return (pt_ref[b, jnp.minimum(j // halves, n_pages - 1)], 0, 0, 0)
        return pl.BlockSpec((None, 2, LANE, PAGE), imap)

    def per_b(shape):
        nd = len(shape)
        return pl.BlockSpec((None,) + tuple(shape[1:]), lambda b, i, p: (b,) + (0,) * (nd - 1))

    def const(shape):
        nd = len(shape)
        return pl.BlockSpec(tuple(shape), lambda b, i, p: (0,) * nd)

    nblk = KV_HEADS * N_SEL_BLOCKS
    grid_spec = pltpu.PrefetchScalarGridSpec(
        num_scalar_prefetch=2, grid=(dbsz,),
        in_specs=[blk_spec(k) for k in range(nblk)]
        + [per_b(q8.shape), per_b(slc_new.shape), per_b(win_new.shape), per_b(state.shape), per_b(gs.shape),
           per_b(oc.shape), const(tnear.shape), const(b0.shape), const(tws.shape)],
        out_specs=[pl.BlockSpec((N_HEADS, LANE), lambda b, i, p: (b, 0)),
                   pl.BlockSpec((None, 2, LANE, n_win), lambda b, i, p: (b, 0, 0, 0))])
    return pl.pallas_call(
        functools.partial(_nsa_sample_sel_kernel, qblk=qblk, n_win=n_win),
        grid_spec=grid_spec,
        out_shape=[jax.ShapeDtypeStruct((dbsz * N_HEADS, LANE), F32),
                   jax.ShapeDtypeStruct((dbsz, 2, LANE, n_win), F32)],
        compiler_params=_params(("arbitrary",)),
        name="nsa_sample_sel",
    )(idx, page_table, *([slc_cache_t] * nblk), q8, slc_new, win_new, state, gs, oc, tnear, b0, tws)


def _merge_kernel(x_ref, o_ref, u_ref, vn_ref, ga_ref, gb_ref, ws_ref, bs_ref, wbn_ref, wbg_ref, wout_ref,
                  lng_ref, x1_ref, *, tm):
    chunk = ws_ref.shape[1]
    keep = (lax.broadcasted_iota(jnp.int32, (chunk, chunk), 0)
            >= lax.broadcasted_iota(jnp.int32, (chunk, chunk), 1))
    vb = vn_ref[...].astype(BF16)
    cols = []
    for gg in range(GM_GROUPS):
        wt = jnp.where(keep, ws_ref[gg], 0.0).astype(BF16)
        rows = [_dot(wt, vb[c * chunk:(c + 1) * chunk, gg * LANE:(gg + 1) * LANE]) for c in range(tm // chunk)]
        cols.append(jnp.concatenate(rows, axis=0) if len(rows) > 1 else rows[0])
    s = jnp.concatenate(cols, axis=1) + jnp.concatenate([bs_ref[...]] * (tm // chunk), axis=0)
    o_gm = (u_ref[...].astype(F32) * s).astype(BF16)
    if len(wbn_ref.shape) == 2:
        a = _dot(o_ref[...].astype(BF16), wbn_ref[...])
    else:
        a = jnp.zeros((tm, D_MODEL), F32)
        for hh in range(N_HEADS):
            a += _dot(o_ref[pl.ds(hh, tm, stride=N_HEADS), :].astype(BF16), wbn_ref[hh])
    bm = _dot(o_gm, wbg_ref[...])
    mixed = (ga_ref[...].astype(F32) * a + gb_ref[...].astype(F32) * bm).astype(BF16)
    x1_ref[...] = x_ref[...] + _rms(_dot(mixed, wout_ref[...]), lng_ref[...])


def _merge(x, o_nsa, u, vn, ga, gb, ws, bs, wbn, w, *, tm, name):
    n = x.shape[0]
    row = lambda i: (i, 0)
    c2 = lambda i: (0, 0)
    c3 = lambda i: (0, 0, 0)
    ws_spec = pl.BlockSpec(ws.shape, c3)
    o_rows = o_nsa.shape[0] // n
    return pl.pallas_call(
        functools.partial(_merge_kernel, tm=tm),
        grid=(n // tm,),
        in_specs=[pl.BlockSpec((tm, D_MODEL), row), pl.BlockSpec((tm * o_rows, o_nsa.shape[1]), row),
                  pl.BlockSpec((tm, GM_DIM), row), pl.BlockSpec((tm, GM_DIM), row),
                  pl.BlockSpec((tm, D_MODEL), row), pl.BlockSpec((tm, D_MODEL), row),
                  ws_spec, pl.BlockSpec(bs.shape, c2),
                  pl.BlockSpec(wbn.shape, c2 if wbn.ndim == 2 else c3), pl.BlockSpec(w["wbg"].shape, c2),
                  pl.BlockSpec(w["wout"].shape, c2), pl.BlockSpec((1, D_MODEL), c2)],
        out_specs=pl.BlockSpec((tm, D_MODEL), row),
        out_shape=jax.ShapeDtypeStruct((n, D_MODEL), F32),
        compiler_params=_params(("parallel",)),
        name=name,
    )(x, o_nsa, u, vn, ga, gb, ws, bs, wbn, w["wbg"], w["wout"], w["ln_mix_post"])


def _ffn_kernel(x_ref, p_ref, lnf_ref, wg_ref, wu_ref, wd_ref, lnp_ref, wple_ref, wpg_ref, lne_ref, y_ref,
                h2_ref, acc_ref):
    j = pl.program_id(1)

    @pl.when(j == 0)
    def _():
        h2_ref[...] = _rms(x_ref[...], lnf_ref[...]).astype(BF16)
        acc_ref[...] = jnp.zeros(acc_ref.shape, F32)

    h2 = h2_ref[...]
    act = (jax.nn.silu(_dot(h2, wg_ref[...])) * _dot(h2, wu_ref[...])).astype(BF16)
    acc_ref[...] += _dot(act, wd_ref[...])

    @pl.when(j == pl.num_programs(1) - 1)
    def _():
        x2 = x_ref[...] + _rms(acc_ref[...], lnp_ref[...])
        e = _dot(p_ref[...].astype(BF16), wple_ref[...])
        gp = jax.nn.sigmoid(_dot(x2.astype(BF16), wpg_ref[...]))
        y_ref[...] = x2 + _rms(gp * e, lne_ref[...])


def _ffn(x1, ple, w, *, tm, name):
    n = x1.shape[0]
    row = lambda i, j: (i, 0)
    c2 = lambda i, j: (0, 0)
    return pl.pallas_call(
        _ffn_kernel,
        grid=(n // tm, D_FF // FF_CHUNK),
        in_specs=[pl.BlockSpec((tm, D_MODEL), row), pl.BlockSpec((tm, PLE_DIM), row),
                  pl.BlockSpec((1, D_MODEL), c2),
                  pl.BlockSpec((D_MODEL, FF_CHUNK), lambda i, j: (0, j)),
                  pl.BlockSpec((D_MODEL, FF_CHUNK), lambda i, j: (0, j)),
                  pl.BlockSpec((FF_CHUNK, D_MODEL), lambda i, j: (j, 0)),
                  pl.BlockSpec((1, D_MODEL), c2), pl.BlockSpec(w["wple"].shape, c2),
                  pl.BlockSpec(w["wpg"].shape, c2), pl.BlockSpec((1, D_MODEL), c2)],
        out_specs=pl.BlockSpec((tm, D_MODEL), row),
        out_shape=jax.ShapeDtypeStruct((n, D_MODEL), F32),
        scratch_shapes=[pltpu.VMEM((tm, D_MODEL), BF16), pltpu.VMEM((tm, D_MODEL), F32)],
        compiler_params=_params(("parallel", "arbitrary")),
        name=name,
    )(x1, ple, w["ln_ffn_pre"], w["wfg"], w["wfu"], w["wfd"], w["ln_ffn_post"], w["wple"], w["wpg"],
      w["ln_ple_post"])


def _prep_weights(i, ln_mix_pre, w_in, cmp_pe_k, cmp_w1_k, cmp_w2_k, cmp_pe_v, cmp_w1_v, cmp_w2_v, gm_ln_g,
                  gm_ln_b, w_branch_nsa, w_branch_gm, w_out, ln_mix_post, ln_ffn_pre, w_ffn_gate, w_ffn_up,
                  w_ffn_down, ln_ffn_post, w_ple, w_ple_gate, ln_ple_post):
    q_dim = N_HEADS * HEAD_DIM
    kv3 = 6 * KV_HEADS * HEAD_DIM
    n_gate = 3 * N_HEADS
    win = w_in[i]
    o = 0
    wq = win[:, o:o + q_dim]; o += q_dim
    wkv = win[:, o:o + kv3]; o += kv3
    wg = win[:, o:o + n_gate]; o += n_gate
    wuv = win[:, o:o + 2 * GM_DIM]; o += 2 * GM_DIM
    wmg = win[:, o:o + 2 * D_MODEL]
    wq4 = wq.reshape(D_MODEL, N_HEADS, 1, HEAD_DIM)
    half = (np.arange(N_HEADS) // HPG)[None, :, None, None] == np.arange(KV_HEADS)[None, None, :, None]
    wq_pad = jnp.where(half, wq4, 0.0).reshape(D_MODEL, N_HEADS * LANE)
    wg_pad = jnp.pad(wg, ((0, 0), (0, LANE - n_gate)))

    def blockdiag(w1):
        w1s = w1.reshape(CMP_LEN // CMP_STRIDE, CMP_STRIDE, HEAD_DIM, CMP_HID).transpose(1, 2, 0, 3)
        w1s = w1s.reshape(CMP_STRIDE, HEAD_DIM, 2 * CMP_HID)
        z = jnp.zeros_like(w1s)
        return jnp.concatenate([jnp.concatenate([w1s, z], axis=2), jnp.concatenate([z, w1s], axis=2)], axis=1)

    wbn4 = w_branch_nsa[i].reshape(N_HEADS, 1, HEAD_DIM, D_MODEL)
    wbn_heads = jnp.where(half.reshape(N_HEADS, KV_HEADS, 1, 1), wbn4, 0.0).reshape(N_HEADS, LANE, D_MODEL)

    zk = jnp.zeros_like(cmp_w2_k[i])
    zv = jnp.zeros_like(cmp_w2_v[i])
    b = lambda a: a.astype(BF16)
    r = lambda a: a[i][None, :].astype(F32)
    return dict(
        ln_mix_pre=r(ln_mix_pre), wq=b(wq_pad), wkv=b(wkv), wg=b(wg_pad), wuv=b(wuv), wmg=b(wmg),
        gm_ln_g=r(gm_ln_g), gm_ln_b=r(gm_ln_b),
        wbk=b(blockdiag(cmp_w1_k[i])), wbv=b(blockdiag(cmp_w1_v[i])),
        pek=jnp.broadcast_to(cmp_pe_k[i].reshape(1, -1), (8, CMP_LEN * HEAD_DIM)).astype(F32), w1k=b(cmp_w1_k[i]),
        pev=jnp.broadcast_to(cmp_pe_v[i].reshape(1, -1), (8, CMP_LEN * HEAD_DIM)).astype(F32), w1v=b(cmp_w1_v[i]),
        w2ka=b(jnp.concatenate([cmp_w2_k[i], zk], axis=1)), w2kb=b(jnp.concatenate([zk, cmp_w2_k[i]], axis=1)),
        w2va=b(jnp.concatenate([cmp_w2_v[i], zv], axis=1)), w2vb=b(jnp.concatenate([zv, cmp_w2_v[i]], axis=1)),
        w2vt=b(cmp_w2_v[i].T),
        wbn=b(w_branch_nsa[i]), wbn_heads=b(wbn_heads), wbg=b(w_branch_gm[i]), wout=b(w_out[i]), ln_mix_post=r(ln_mix_post),
        ln_ffn_pre=r(ln_ffn_pre), wfg=b(w_ffn_gate[i]), wfu=b(w_ffn_up[i]), wfd=b(w_ffn_down[i]),
        ln_ffn_post=r(ln_ffn_post), wple=b(w_ple[i]), wpg=b(w_ple_gate[i]), ln_ple_post=r(ln_ple_post),
    )


def _gate_perm():
    return np.array([hg * 3 + br for br in range(3) for hg in range(N_HEADS)])


def _prompt_layer(x, ple, w, rel_bias, gm_ws, gm_bs):
    bsz, seq, _ = x.shape
    assert seq % SEL_TILE == 0 and seq // SEL_LEN <= NJ and seq >= WINDOW
    n = bsz * seq
    tm = 256
    xf = x.reshape(n, D_MODEL)
    wp = dict(w)
    perm = _gate_perm()
    wp["wg"] = jnp.concatenate([w["wg"][:, perm], w["wg"][:, len(perm):]], axis=1)
    (qhm, cmp, cmpt, slct, wint, kaug, vts, kwin, vtw, gt, u, vn, ga, gb) = _inproj(
        xf, wp, prompt=True, seq=seq, tm=tm)
    kc, vct = _compress_prompt(cmp, w, bsz=bsz, seq=seq)

    o_nsa = _nsa_prompt(qhm, kc, vct, kaug, vts, kwin, vtw, gt, _bias_tables(rel_bias), bsz=bsz, seq=seq)

    bs_tile = jnp.repeat(gm_bs.T, LANE, axis=1).astype(F32)
    x1 = _merge(xf, o_nsa, u, vn, ga, gb, gm_ws.astype(F32), bs_tile, w["wbn"], w, tm=512, name="merge_prompt")
    y = _ffn(x1, ple.reshape(n, PLE_DIM), w, tm=512, name="ffn_prompt")

    def rows_form(at):
        return at.reshape(bsz, 2, KV_HEADS, HEAD_DIM, at.shape[-1]).transpose(0, 4, 1, 2, 3)

    return (y.reshape(bsz, seq, D_MODEL), rows_form(cmpt), rows_form(slct),
            rows_form(wint[:, :, seq - min(WINDOW, seq):]))


def _sample_layer(x, ple, w, rel_bias, gm_ws, gm_bs, cache_cmp, cache_slc, win_buf, page_table):
    dbsz, nq, _ = x.shape
    assert nq == 1
    n_pool = cache_cmp.shape[0]
    n_pages = page_table.shape[1]
    past = n_pages * PAGE
    n_win = win_buf.shape[1]
    assert past % GM_CHUNK == 0 and n_win == WINDOW and past >= WINDOW and dbsz % 8 == 0
    xf = x.reshape(dbsz, D_MODEL)
    q, cmp, slc, win, gate, u, vn, ga, gb = _inproj(xf, w, prompt=False, seq=1, tm=dbsz)

    pps = min(16, n_pages)
    hbuf = _compress_paged(_pages_t(cache_cmp), page_table, w, pps=pps)
    n_sub = past // CMP_STRIDE
    qblk = past // SEL_LEN
    n_sel_pad = -(-(qblk + 1) // LANE) * LANE
    sh = _shifted_bias(rel_bias)
    i = np.arange(n_sub)
    d_c = past - (CMP_STRIDE * i + CMP_LEN - 1)
    tcs = _sample_table(sh, d_c, i < n_sub - 1)
    ovs = jnp.asarray(_overlap_t(n_sub, n_sel_pad).T)
    q8 = q.reshape(dbsz, N_HEADS, LANE)
    oc, idx = _nsa_sample_cmp(hbuf, q8, w, tcs, ovs, qblk=qblk)
    idx2 = idx[:, 0:KV_HEADS, 0:N_SEL_BLOCKS].reshape(dbsz, KV_HEADS * N_SEL_BLOCKS)

    l = np.arange(PAGE)
    tnear = _sample_table(sh, PAGE - l, l >= 0)
    b0 = jnp.broadcast_to(sh[0][:, None] * LOG2E, (N_HEADS, LANE)).astype(F32)
    kpos = np.arange(n_win)
    tws = _sample_table(sh, n_win - kpos, kpos >= 0)
    gs = gate[:, 0:3 * N_HEADS].reshape(dbsz, N_HEADS, 3).transpose(0, 2, 1)
    gs = jnp.broadcast_to(gs[..., None], (dbsz, 3, N_HEADS, LANE))
    o8, win_out = _nsa_sample_sel(
        idx2, page_table, _pages_t(cache_slc), q8,
        jnp.broadcast_to(slc[:, None, :], (dbsz, 8, 256)), jnp.broadcast_to(win[:, None, :], (dbsz, 8, 256)),
        _pages_t(win_buf), gs, oc, tnear, b0, tws, qblk=qblk)
    win_out = win_out.reshape(dbsz, 2, KV_HEADS, HEAD_DIM, n_win).transpose(0, 4, 1, 2, 3)

    ws_diag = gm_ws[:, 0, 0][:, None, None] * jnp.eye(dbsz, dtype=F32)[None]
    bs_tile = jnp.broadcast_to(jnp.repeat(gm_bs[:, 0], LANE)[None, :], (dbsz, GM_GROUPS * LANE)).astype(F32)
    x1 = _merge(xf, o8, u, vn, ga, gb, ws_diag, bs_tile, w["wbn_heads"], w, tm=dbsz, name="merge_sample")
    y = _ffn(x1, ple.reshape(dbsz, PLE_DIM), w, tm=dbsz, name="ffn_sample")
    kvshape = (dbsz, 1, 2, KV_HEADS, HEAD_DIM)
    return (y.reshape(dbsz, 1, D_MODEL), cmp.reshape(kvshape), slc.reshape(kvshape),
            win_out, vn.reshape(dbsz, 1, GM_DIM))


def kernel(x_prompt, x_sample, cache_cmp_kv, cache_slc_kv, state_win_kv, page_table, p_prompt, p_sample, rel_bias,
           ln_mix_pre, w_in, cmp_pe_k, cmp_w1_k, cmp_w2_k, cmp_pe_v, cmp_w1_v, cmp_w2_v, gm_ln_g, gm_ln_b, gm_ws,
           gm_bs, w_branch_nsa, w_branch_gm, w_out, ln_mix_post, ln_ffn_pre, w_ffn_gate, w_ffn_up, w_ffn_down,
           ln_ffn_post, w_ple, w_ple_gate, ln_ple_post):
    depth = w_in.shape[0]
    xp, xs = x_prompt, x_sample
    outs = [[] for _ in range(7)]
    for i in range(depth):
        w = _prep_weights(i, ln_mix_pre, w_in, cmp_pe_k, cmp_w1_k, cmp_w2_k, cmp_pe_v, cmp_w1_v, cmp_w2_v, gm_ln_g,
                          gm_ln_b, w_branch_nsa, w_branch_gm, w_out, ln_mix_post, ln_ffn_pre, w_ffn_gate, w_ffn_up,
                          w_ffn_down, ln_ffn_post, w_ple, w_ple_gate, ln_ple_post)
        xp, c_p, s_p, w_p = _prompt_layer(xp, p_prompt[i], w, rel_bias, gm_ws[i], gm_bs[i])
        xs, c_s, s_s, w_s, v_s = _sample_layer(xs, p_sample[i], w, rel_bias, gm_ws[i], gm_bs[i], cache_cmp_kv[i],
                                               cache_slc_kv[i], state_win_kv[i], page_table)
        for lst, val in zip(outs, (c_p, s_p, w_p, c_s, s_s, w_s, v_s)):
            lst.append(val)
    return (xp, xs) + tuple(jnp.stack(o) for o in outs)
```

```python
import functools
import math

import jax
import jax.numpy as jnp
import numpy as np
from jax import lax
from jax.experimental import pallas as pl
from jax.experimental.pallas import tpu as pltpu

F32 = jnp.float32
BF16 = jnp.bfloat16

D_MODEL = 1024
N_HEADS = 8
KV_HEADS = 2
HPG = N_HEADS // KV_HEADS
HEAD_DIM = 64
CMP_LEN = 32
CMP_STRIDE = 16
CMP_HID = 2 * HEAD_DIM
SEL_LEN = 64
N_SEL_BLOCKS = 16
WINDOW = 512
Q_BLOCK = 128
GM_GROUPS = 4
GM_CHUNK = 128
GM_DIM = D_MODEL // 2
D_FF = -(-8 * D_MODEL // (3 * 256)) * 256
PLE_DIM = 256
N_BUCKETS = 32
MAX_DISTANCE = 128
EPS = 1e-6
NEG = -1e30
LOG2E = 1.4426950408889634
FORCE = 1e6
PAGE = 128

LANE = 128
NJ = 128
SEL_TILE = 256
FF_CHUNK = 256
VMEM_LIMIT = 56 * 1024 * 1024


def _dot(a, b):
    return jnp.dot(a, b, preferred_element_type=F32)


def _dot_nt(a, b):
    return lax.dot_general(a, b, (((1,), (1,)), ((), ())), preferred_element_type=F32)


def _dot_f32(a, b):
    return jnp.dot(a, b, preferred_element_type=F32, precision=lax.Precision.HIGHEST)


def _rms(x, g):
    return x * lax.rsqrt(jnp.mean(x * x, axis=-1, keepdims=True) + EPS) * g


def _params(sem):
    return pltpu.CompilerParams(dimension_semantics=sem, vmem_limit_bytes=VMEM_LIMIT)


def _t5_bucket_np(d):
    d = np.maximum(d, 0)
    max_exact = N_BUCKETS // 2
    ratio = (np.log(np.maximum(d, 1).astype(np.float32) / np.float32(max_exact))
             / np.float32(math.log(MAX_DISTANCE / max_exact)))
    large = np.minimum(max_exact + (ratio * np.float32(N_BUCKETS - max_exact)).astype(np.int32), N_BUCKETS - 1)
    return np.where(d < max_exact, d, large)


_BUCKET = _t5_bucket_np(np.arange(MAX_DISTANCE + 1))
assert _BUCKET[MAX_DISTANCE] == N_BUCKETS - 1


def _shifted_bias(rel_bias):
    return rel_bias[_BUCKET] - rel_bias[N_BUCKETS - 1][None, :]


def _sample_table(sh, dist, valid):
    t = sh[np.minimum(np.maximum(dist, 0), MAX_DISTANCE)] * LOG2E
    return jnp.where(valid[:, None], t, NEG).T.astype(F32)


_BUCKET_START = [int(np.argmax(_BUCKET >= k)) for k in range(N_BUCKETS)]
_BAND_ROWS = 3 * (Q_BLOCK // CMP_STRIDE)


def _bias_tables_kernel(rb_ref, tc_ref, ts_ref, tw_ref):
    def fill(store, rows, chunk, c0, stride, dmax):
        def body(c, carry):
            r0 = pl.multiple_of(c * chunk, chunk)
            r = r0 + lax.broadcasted_iota(jnp.int32, (chunk, Q_BLOCK), 0)
            d = c0 + lax.broadcasted_iota(jnp.int32, (chunk, Q_BLOCK), 1) - stride * r
            ok = (d >= 0) & (d <= dmax)
            dc = jnp.minimum(d, MAX_DISTANCE)
            for hd in range(N_HEADS):
                v = jnp.full((chunk, Q_BLOCK), rb_ref[0, hd], F32)
                for k in range(1, N_BUCKETS):
                    v = jnp.where(dc >= _BUCKET_START[k], rb_ref[k, hd], v)
                v = (v - rb_ref[N_BUCKETS - 1, hd]) * LOG2E
                store(hd // HPG, r0, chunk, hd % HPG, jnp.where(ok, v, NEG))
            return carry
        lax.fori_loop(0, rows // chunk, body, 0)

    def lanes(hh):
        return slice(hh * Q_BLOCK, (hh + 1) * Q_BLOCK)

    def st_c(g, r0, n, hh, v):
        tc_ref[g, pl.ds(r0, n), lanes(hh)] = v

    def st_w(g, r0, n, hh, v):
        tw_ref[g, pl.ds(r0, n), lanes(hh)] = v

    big = 1 << 30
    fill(st_c, _BAND_ROWS, 8, 2 * Q_BLOCK - (CMP_LEN - 1), CMP_STRIDE, big)
    for par in range(SEL_TILE // Q_BLOCK):
        def st_s(g, r0, n, hh, v, par=par):
            ts_ref[g, par, pl.ds(r0, n), lanes(hh)] = v
        fill(st_s, 2 * SEL_TILE, 64, SEL_TILE + Q_BLOCK * par, 1, big)
    fill(st_w, WINDOW + Q_BLOCK, 64, WINDOW, 1, WINDOW)


def _bias_tables(rel_bias):
    nl = HPG * Q_BLOCK
    return pl.pallas_call(
        _bias_tables_kernel,
        in_specs=[pl.BlockSpec(memory_space=pltpu.SMEM)],
        out_shape=[jax.ShapeDtypeStruct((KV_HEADS, _BAND_ROWS, nl), F32),
                   jax.ShapeDtypeStruct((KV_HEADS, SEL_TILE // Q_BLOCK, 2 * SEL_TILE, nl), F32),
                   jax.ShapeDtypeStruct((KV_HEADS, WINDOW + Q_BLOCK, nl), F32)],
        compiler_params=pltpu.CompilerParams(vmem_limit_bytes=VMEM_LIMIT),
        name="bias_tables",
    )(rel_bias.astype(F32))


def _overlap_t(n_cmp, n_sel_pad):
    ci = np.arange(n_cmp)[None, :] * CMP_STRIDE
    sj = np.arange(n_sel_pad)[:, None] * SEL_LEN
    return ((ci <= sj + SEL_LEN - 1) & (ci + CMP_LEN - 1 >= sj)).astype(np.float32)


def _overlap_offsets():
    per = SEL_LEN // CMP_STRIDE
    ov = _overlap_t(16 * per, 16)
    jj, ii = np.nonzero(ov)
    offs = sorted({int(i - per * j) for j, i in zip(jj, ii)})
    band = np.isin(np.arange(16 * per)[None, :] - per * np.arange(16)[:, None], offs)
    assert (band == (ov > 0)).all() and min(offs) >= -8
    return offs


_OVERLAP_OFFSETS = _overlap_offsets()


def _inproj_kernel(x_ref, lng_ref, wq_ref, wkv_ref, wg_ref, wuv_ref, wmg_ref, glg_ref, glb_ref, *outs,
                   prompt, seq, tm):
    h = _rms(x_ref[...], lng_ref[...]).astype(BF16)
    q = _dot(h, wq_ref[...]) * (HEAD_DIM ** -0.5 * LOG2E)
    kv = _dot(h, wkv_ref[...])
    gate = jax.nn.sigmoid(_dot(h, wg_ref[...]))
    uv = jax.nn.gelu(_dot(h, wuv_ref[...]))
    u = uv[:, :GM_DIM]
    v = uv[:, GM_DIM:]
    mu = jnp.mean(v, axis=-1, keepdims=True)
    var = jnp.mean(jnp.square(v - mu), axis=-1, keepdims=True)
    vn = (v - mu) * lax.rsqrt(var + EPS) * glg_ref[...] + glb_ref[...]
    mg = jax.nn.sigmoid(_dot(h, wmg_ref[...]))
    cmp, slc, win = kv[:, 0:256], kv[:, 256:512], kv[:, 512:768]
    if prompt:
        (q_ref, cmp_ref, cmpt_ref, slct_ref, wint_ref, kaug_ref, vts_ref, kwin_ref, vtw_ref, gt_ref,
         u_ref, vn_ref, ga_ref, gb_ref) = outs
        for hh in range(N_HEADS):
            q_ref[hh] = q[:, hh * LANE:(hh + 1) * LANE].astype(BF16)
        pos = (pl.program_id(0) * tm + lax.broadcasted_iota(jnp.int32, (tm, NJ), 0)) % seq
        onehot = (lax.broadcasted_iota(jnp.int32, (tm, NJ), 1) == pos // SEL_LEN)
        kaug_ref[:, 0:NJ] = onehot.astype(BF16)
        kaug_ref[:, NJ:NJ + LANE] = slc[:, 0:LANE].astype(BF16)
        kwin_ref[...] = win[:, 0:LANE].astype(BF16)
        slc_t = slc.T
        win_t = win.T
        for g in range(KV_HEADS):
            r0 = LANE + HEAD_DIM * g
            for c in range(tm // SEL_TILE):
                vts_ref[g, c] = slc_t[r0:r0 + HEAD_DIM, c * SEL_TILE:(c + 1) * SEL_TILE].astype(BF16)
            for c in range(tm // LANE):
                vtw_ref[g, c] = win_t[r0:r0 + HEAD_DIM, c * LANE:(c + 1) * LANE].astype(BF16)
        gt_ref[...] = gate.T[0:32, :]
        cmpt_ref[...] = cmp.T
        slct_ref[...] = slc_t
        wint_ref[...] = win_t
    else:
        q_ref, cmp_ref, slc_ref, win_ref, gate_ref, u_ref, vn_ref, ga_ref, gb_ref = outs
        q_ref[...] = q
        gate_ref[...] = gate
        slc_ref[...] = slc
        win_ref[...] = win
    cmp_ref[...] = cmp
    u_ref[...] = u.astype(u_ref.dtype)
    vn_ref[...] = vn.astype(vn_ref.dtype)
    ga_ref[...] = mg[:, :D_MODEL].astype(ga_ref.dtype)
    gb_ref[...] = mg[:, D_MODEL:].astype(gb_ref.dtype)


def _inproj(x, w, *, prompt, seq, tm):
    n = x.shape[0]
    row = lambda i: (i, 0)
    const = lambda i: (0, 0)
    in_specs = [pl.BlockSpec((tm, D_MODEL), row), pl.BlockSpec((1, D_MODEL), const),
                pl.BlockSpec(w["wq"].shape, const), pl.BlockSpec(w["wkv"].shape, const),
                pl.BlockSpec(w["wg"].shape, const), pl.BlockSpec(w["wuv"].shape, const),
                pl.BlockSpec(w["wmg"].shape, const), pl.BlockSpec((1, GM_DIM), const),
                pl.BlockSpec((1, GM_DIM), const)]
    kv_shapes = [jax.ShapeDtypeStruct((n, 256), F32)] * 3
    kv_specs = [pl.BlockSpec((tm, 256), row)] * 3
    if prompt:
        act = BF16
        tiles = seq // tm
        kvt_shapes = [jax.ShapeDtypeStruct((n // seq, 256, seq), F32)] * 3
        kvt_specs = [pl.BlockSpec((None, 256, tm), lambda i: (i // tiles, 0, i % tiles))] * 3
        out_shape = ([jax.ShapeDtypeStruct((N_HEADS, n, LANE), BF16)] + kv_shapes[:1] + kvt_shapes + [
            jax.ShapeDtypeStruct((n, NJ + LANE), BF16),
            jax.ShapeDtypeStruct((KV_HEADS, n // SEL_TILE, HEAD_DIM, SEL_TILE), BF16),
            jax.ShapeDtypeStruct((n, LANE), BF16),
            jax.ShapeDtypeStruct((KV_HEADS, n // LANE, HEAD_DIM, LANE), BF16),
            jax.ShapeDtypeStruct((32, n), F32)])
        out_specs = ([pl.BlockSpec((N_HEADS, tm, LANE), lambda i: (0, i, 0))] + kv_specs[:1] + kvt_specs + [
            pl.BlockSpec((tm, NJ + LANE), row),
            pl.BlockSpec((KV_HEADS, tm // SEL_TILE, HEAD_DIM, SEL_TILE), lambda i: (0, i, 0, 0)),
            pl.BlockSpec((tm, LANE), row),
            pl.BlockSpec((KV_HEADS, tm // LANE, HEAD_DIM, LANE), lambda i: (0, i, 0, 0)),
            pl.BlockSpec((32, tm), lambda i: (0, i))])
    else:
        act = F32
        out_shape = ([jax.ShapeDtypeStruct((n, N_HEADS * LANE), F32)] + kv_shapes
                     + [jax.ShapeDtypeStruct((n, LANE), F32)])
        out_specs = [pl.BlockSpec((tm, N_HEADS * LANE), row)] + kv_specs + [pl.BlockSpec((tm, LANE), row)]
    out_shape += [jax.ShapeDtypeStruct((n, GM_DIM), act)] * 2 + [jax.ShapeDtypeStruct((n, D_MODEL), act)] * 2
    out_specs += [pl.BlockSpec((tm, GM_DIM), row)] * 2 + [pl.BlockSpec((tm, D_MODEL), row)] * 2
    return pl.pallas_call(
        functools.partial(_inproj_kernel, prompt=prompt, seq=seq, tm=tm),
        grid=(n // tm,), in_specs=in_specs, out_specs=out_specs, out_shape=out_shape,
        compiler_params=_params(("parallel",)),
        name="inproj_prompt" if prompt else "inproj_sample",
    )(x, w["ln_mix_pre"], w["wq"], w["wkv"], w["wg"], w["wuv"], w["wmg"], w["gm_ln_g"], w["gm_ln_b"])


def _row_pair(x_ref, r, n_sub):
    return jnp.concatenate([x_ref[pl.ds(r, n_sub, stride=CMP_STRIDE), :],
                            x_ref[pl.ds(r + 1, n_sub, stride=CMP_STRIDE), :]], axis=1).astype(BF16)


def _compress_finish(hk, hv, pek_ref, w1k_ref, pev_ref, w1v_ref, w2ka_ref, w2kb_ref, n_sub):
    pwk = _dot(pek_ref[...].astype(BF16), w1k_ref[...])[0:1, :]
    pwv = _dot(pev_ref[...].astype(BF16), w1v_ref[...])[0:1, :]

    def act(hh, g, pw):
        a0 = hh[:, 256 * g:256 * g + CMP_HID]
        a1 = hh[:, 256 * g + CMP_HID:256 * g + 2 * CMP_HID]
        return jax.nn.silu(a0 + pltpu.roll(a1, n_sub - 1, 0) + pw).astype(BF16)

    return [act(hk, 0, pwk), act(hk, 1, pwk)], [act(hv, 0, pwv), act(hv, 1, pwv)]


def _compress_prompt_kernel(cmpk_ref, cmpv_ref, wbk_ref, wbv_ref, pek_ref, w1k_ref, pev_ref, w1v_ref,
                            w2ka_ref, w2kb_ref, w2vt_ref, kc_ref, vct_ref, *, n_sub):
    hk = jnp.zeros((n_sub, 4 * CMP_HID), F32)
    hv = jnp.zeros((n_sub, 4 * CMP_HID), F32)
    for r in range(0, CMP_STRIDE, 2):
        hk += _dot(_row_pair(cmpk_ref, r, n_sub), wbk_ref[r // 2])
        hv += _dot(_row_pair(cmpv_ref, r, n_sub), wbv_ref[r // 2])
    ak, av = _compress_finish(hk, hv, pek_ref, w1k_ref, pev_ref, w1v_ref, w2ka_ref, w2kb_ref, n_sub)
    kc_ref[...] = (_dot(ak[0], w2ka_ref[...]) + _dot(ak[1], w2kb_ref[...])).astype(BF16)
    for g in range(KV_HEADS):
        vct_ref[g] = _dot_nt(w2vt_ref[...], av[g]).astype(BF16)


def _compress_prompt(cmp, w, *, bsz, seq):
    n_sub = seq // CMP_STRIDE
    const2 = lambda b: (0, 0)
    const3 = lambda b: (0, 0, 0)
    return pl.pallas_call(
        functools.partial(_compress_prompt_kernel, n_sub=n_sub),
        grid=(bsz,),
        in_specs=[pl.BlockSpec((seq, LANE), lambda b: (b, 0)), pl.BlockSpec((seq, LANE), lambda b: (b, 1)),
                  pl.BlockSpec(w["wbk"].shape, const3), pl.BlockSpec(w["wbv"].shape, const3),
                  pl.BlockSpec(w["pek"].shape, const2), pl.BlockSpec(w["w1k"].shape, const2),
                  pl.BlockSpec(w["pev"].shape, const2), pl.BlockSpec(w["w1v"].shape, const2),
                  pl.BlockSpec(w["w2ka"].shape, const2), pl.BlockSpec(w["w2kb"].shape, const2),
                  pl.BlockSpec(w["w2vt"].shape, const2)],
        out_specs=[pl.BlockSpec((None, n_sub, LANE), lambda b: (b, 0, 0)),
                   pl.BlockSpec((None, KV_HEADS, HEAD_DIM, n_sub), lambda b: (b, 0, 0, 0))],
        out_shape=[jax.ShapeDtypeStruct((bsz, n_sub, LANE), BF16),
                   jax.ShapeDtypeStruct((bsz, KV_HEADS, HEAD_DIM, n_sub), BF16)],
        compiler_params=_params(("parallel",)),
        name="compress_prompt",
    )(cmp, cmp, w["wbk"], w["wbv"], w["pek"], w["w1k"], w["pev"], w["w1v"], w["w2ka"], w["w2kb"], w["w2vt"])


def _compress_paged_kernel(pt_ref, *refs, pps):
    kpages, vpages = refs[:pps], refs[pps:2 * pps]
    wbk_ref, wbv_ref, h_ref, xk_ref, xv_ref = refs[2 * pps:]
    sub_per_page = PAGE // CMP_STRIDE
    n = pps * sub_per_page
    for k in range(pps):
        xk_ref[k * PAGE:(k + 1) * PAGE, :] = kpages[k][...].T
        xv_ref[k * PAGE:(k + 1) * PAGE, :] = vpages[k][...].T
    hk = jnp.zeros((n, 4 * CMP_HID), F32)
    hv = jnp.zeros((n, 4 * CMP_HID), F32)
    for r in range(0, CMP_STRIDE, 2):
        hk += _dot(_row_pair(xk_ref, r, n), wbk_ref[r // 2])
        hv += _dot(_row_pair(xv_ref, r, n), wbv_ref[r // 2])
    h_ref[:, 0:4 * CMP_HID] = hk
    h_ref[:, 4 * CMP_HID:8 * CMP_HID] = hv


def _compress_paged(cache_t, page_table, w, *, pps):
    dbsz, n_pages = page_table.shape
    sub_per_page = PAGE // CMP_STRIDE
    n_sub = n_pages * sub_per_page

    def page_spec(k, c):
        return pl.BlockSpec((None, None, LANE, PAGE), lambda b, t, pt: (pt[b, t * pps + k], c, 0, 0))

    const3 = lambda b, t, pt: (0, 0, 0)
    grid_spec = pltpu.PrefetchScalarGridSpec(
        num_scalar_prefetch=1, grid=(dbsz, n_pages // pps),
        in_specs=[page_spec(k, c) for c in range(2) for k in range(pps)]
        + [pl.BlockSpec(w["wbk"].shape, const3), pl.BlockSpec(w["wbv"].shape, const3)],
        out_specs=pl.BlockSpec((None, pps * sub_per_page, 8 * CMP_HID), lambda b, t, pt: (b, t, 0)),
        scratch_shapes=[pltpu.VMEM((pps * PAGE, LANE), F32), pltpu.VMEM((pps * PAGE, LANE), F32)])
    return pl.pallas_call(
        functools.partial(_compress_paged_kernel, pps=pps),
        grid_spec=grid_spec,
        out_shape=jax.ShapeDtypeStruct((dbsz, n_sub, 8 * CMP_HID), F32),
        compiler_params=_params(("parallel", "arbitrary")),
        name="compress_paged",
    )(page_table, *([cache_t] * (2 * pps)), w["wbk"], w["wbv"])


def _pages_t(cache):
    n, npos = cache.shape[0], cache.shape[1]
    return cache.transpose(0, 2, 3, 4, 1).reshape(n, 2, KV_HEADS * HEAD_DIM, npos)


def _col_softmax(s):
    m = jnp.max(s, axis=0, keepdims=True)
    p = jnp.exp2(s - m)
    return p * jnp.where(m > 0.5 * NEG, 1.0 / jnp.sum(p, axis=0, keepdims=True), 0.0)


def _select_blocks(score, jio, axis):
    selneg = jnp.full(score.shape, NEG, F32)
    picked = []
    for _ in range(N_SEL_BLOCKS):
        mx = jnp.max(score, axis=axis, keepdims=True)
        jm = jnp.min(jnp.where(score == mx, jio, 1 << 20), axis=axis, keepdims=True)
        pick = jio == jm
        selneg = jnp.where(pick, 0.0, selneg)
        score = jnp.where(pick, -3e38, score)
        picked.append(jm)
    return selneg, picked


def _nsa_prompt_kernel(q_ref, kc_ref, vct_ref, kaug_ref, vts_ref, kwin_ref, vtw_ref, gt_ref,
                       tc_ref, ts_ref, tw_ref, o_ref,
                       sc_ref, ps_ref, qaug_ref, m_ref, l_ref, acc_ref, sa_ref, sb_ref, *, n_cmp):
    g = pl.program_id(1)
    qb = pl.program_id(2)
    nl = HPG * Q_BLOCK
    q2 = q_ref[...].reshape(nl, LANE)

    pad = 2 * (Q_BLOCK // CMP_STRIDE)
    band = 3 * (Q_BLOCK // CMP_STRIDE)
    sc_ref[0:pad, :] = jnp.zeros((pad, nl), F32)
    sc_ref[pad:pad + n_cmp, :] = _dot_nt(kc_ref[...], q2)
    w0 = pl.multiple_of(qb * (Q_BLOCK // CMP_STRIDE), 8)
    sc_ref[pl.ds(w0, band), :] += tc_ref[...]
    sc = sc_ref[pad:pad + n_cmp, :]
    row = lax.broadcasted_iota(jnp.int32, (n_cmp, nl), 0)
    sc = jnp.where(row < (qb + 1) * (Q_BLOCK // CMP_STRIDE), sc, NEG)
    pc = _col_softmax(sc)
    o_c = _dot(vct_ref[...], pc.astype(BF16))

    n_wt = WINDOW // Q_BLOCK + 1
    tix = [jnp.maximum(qb - (n_wt - 1) + t, 0) for t in range(n_wt)]
    kw = jnp.concatenate([kwin_ref[pl.ds(pl.multiple_of(t * Q_BLOCK, Q_BLOCK), Q_BLOCK), :] for t in tix], axis=0)
    sw = _dot_nt(kw, q2) + tw_ref[...]
    wrow = lax.broadcasted_iota(jnp.int32, (n_wt * Q_BLOCK, nl), 0) + (qb - (n_wt - 1)) * Q_BLOCK
    pw = _col_softmax(jnp.where(wrow >= 0, sw, NEG))
    vw = jnp.concatenate([vtw_ref[t] for t in tix], axis=1)
    o_w = _dot(vw, pw.astype(BF16))

    psum = pc[:, 0:Q_BLOCK]
    for hh in range(1, HPG):
        psum = psum + pc[:, hh * Q_BLOCK:(hh + 1) * Q_BLOCK]
    per_sel = SEL_LEN // CMP_STRIDE
    n_sel = n_cmp // per_sel
    ps_ref[0:8, :] = jnp.zeros((8, Q_BLOCK), F32)
    ps_ref[8:8 + n_cmp, :] = psum
    imp_t = ps_ref[pl.ds(8 + _OVERLAP_OFFSETS[0], n_sel, stride=per_sel), :]
    for off in _OVERLAP_OFFSETS[1:]:
        imp_t = imp_t + ps_ref[pl.ds(8 + off, n_sel, stride=per_sel), :]
    if n_sel < NJ:
        imp_t = jnp.concatenate([imp_t, jnp.zeros((NJ - n_sel, Q_BLOCK), F32)], axis=0)
    jio = lax.broadcasted_iota(jnp.int32, (NJ, Q_BLOCK), 0)
    qi = lax.broadcasted_iota(jnp.int32, (NJ, Q_BLOCK), 1)
    qblk = qb * (Q_BLOCK // SEL_LEN) + qi // SEL_LEN
    forced = (jio == 0) | (jio == qblk) | (jio == qblk - 1)
    score = jnp.where(forced, FORCE, jnp.where(jio <= qblk, imp_t, -FORCE))
    selneg_t, _ = _select_blocks(score, jio, 0)
    selneg = selneg_t.T.astype(BF16)
    for hh in range(HPG):
        qaug_ref[hh * Q_BLOCK:(hh + 1) * Q_BLOCK, 0:NJ] = selneg
    qaug_ref[:, NJ:NJ + LANE] = q2

    m_ref[...] = jnp.full((1, nl), -1e38, F32)
    l_ref[...] = jnp.zeros((1, nl), F32)
    acc_ref[...] = jnp.zeros((HEAD_DIM, nl), F32)

    def scores(u0, ntile):
        rows = ntile * SEL_TILE
        ka = kaug_ref[pl.ds(pl.multiple_of(u0 * SEL_TILE, SEL_TILE), rows), :]
        return _dot_nt(ka, qaug_ref[...])

    def fold(s, u0, ntile):
        m_old = m_ref[...]
        m_new = jnp.maximum(m_old, jnp.max(s, axis=0, keepdims=True))
        alpha = jnp.exp2(m_old - m_new)
        p = jnp.exp2(s - m_new)
        l_ref[...] = alpha * l_ref[...] + jnp.sum(p, axis=0, keepdims=True)
        p = p.astype(BF16)
        pv = _dot(vts_ref[u0], p[0:SEL_TILE])
        for t in range(1, ntile):
            pv = pv + _dot(vts_ref[u0 + t], p[t * SEL_TILE:(t + 1) * SEL_TILE])
        acc_ref[...] = alpha * acc_ref[...] + pv
        m_ref[...] = m_new

    tiles_per_q = SEL_TILE // Q_BLOCK
    u_last = qb // tiles_per_q
    par = qb % tiles_per_q
    n_far = jnp.maximum(u_last - 1, 0)
    n_pair = n_far // 2

    @pl.when(n_pair >= 1)
    def _():
        sa_ref[...] = scores(0, 2)

    def far(i, carry):
        sb_ref[...] = scores(4 * i + 2, 2)
        fold(sa_ref[...], 4 * i, 2)
        sa_ref[...] = scores(2 * jnp.minimum(2 * i + 2, n_pair - 1), 2)
        fold(sb_ref[...], 4 * i + 2, 2)
        return carry

    lax.fori_loop(0, n_pair // 2, far, 0)

    @pl.when(n_pair % 2 == 1)
    def _():
        fold(sa_ref[...], 2 * (n_pair - 1), 2)

    @pl.when(n_far % 2 == 1)
    def _():
        fold(scores(n_far - 1, 1), n_far - 1, 1)

    @pl.when(u_last >= 1)
    def _():
        fold(scores(u_last - 1, 2) + ts_ref[par], u_last - 1, 2)

    @pl.when(u_last == 0)
    def _():
        fold(scores(0, 1) + ts_ref[par, SEL_TILE:2 * SEL_TILE, :], 0, 1)

    o_s = acc_ref[...] / l_ref[...]

    def gate_row(br):
        rows = gt_ref[br * N_HEADS:(br + 1) * N_HEADS, :]
        rows = jnp.where(g == 0, rows[0:HPG], rows[HPG:2 * HPG])
        return jnp.concatenate([rows[hh:hh + 1, :] for hh in range(HPG)], axis=1)

    o_t = gate_row(0) * o_c + gate_row(1) * o_s + gate_row(2) * o_w
    stack = jnp.concatenate([o_t[:, hh * Q_BLOCK:(hh + 1) * Q_BLOCK] for hh in range(HPG)], axis=0)
    o_ref[...] = stack.T.astype(o_ref.dtype)


def _nsa_prompt(qhm, kc, vct, kaug, vts, kwin, vtw, gt, tabs, *, bsz, seq):
    n = bsz * seq
    nqb = seq // Q_BLOCK
    n_cmp = seq // CMP_STRIDE
    nl = HPG * Q_BLOCK
    tc, ts, tw = tabs
    in_specs = [
        pl.BlockSpec((HPG, Q_BLOCK, LANE), lambda b, g, i: (g, b * nqb + i, 0)),
        pl.BlockSpec((None, n_cmp, LANE), lambda b, g, i: (b, 0, 0)),
        pl.BlockSpec((None, None, HEAD_DIM, n_cmp), lambda b, g, i: (b, g, 0, 0)),
        pl.BlockSpec((seq, NJ + LANE), lambda b, g, i: (b, 0)),
        pl.BlockSpec((None, seq // SEL_TILE, HEAD_DIM, SEL_TILE), lambda b, g, i: (g, b, 0, 0)),
        pl.BlockSpec((seq, LANE), lambda b, g, i: (b, 0)),
        pl.BlockSpec((None, seq // LANE, HEAD_DIM, LANE), lambda b, g, i: (g, b, 0, 0)),
        pl.BlockSpec((32, Q_BLOCK), lambda b, g, i: (0, b * nqb + i)),
        pl.BlockSpec((None,) + tc.shape[1:], lambda b, g, i: (g, 0, 0)),
        pl.BlockSpec((None,) + ts.shape[1:], lambda b, g, i: (g, 0, 0, 0)),
        pl.BlockSpec((None,) + tw.shape[1:], lambda b, g, i: (g, 0, 0)),
    ]
    pad = 2 * (Q_BLOCK // CMP_STRIDE)
    return pl.pallas_call(
        functools.partial(_nsa_prompt_kernel, n_cmp=n_cmp),
        grid=(bsz, KV_HEADS, nqb),
        in_specs=in_specs,
        out_specs=pl.BlockSpec((Q_BLOCK, HPG * HEAD_DIM), lambda b, g, i: (b * nqb + i, g)),
        out_shape=jax.ShapeDtypeStruct((n, N_HEADS * HEAD_DIM), BF16),
        scratch_shapes=[pltpu.VMEM((pad + n_cmp + pad, nl), F32), pltpu.VMEM((8 + n_cmp, Q_BLOCK), F32),
                        pltpu.VMEM((nl, NJ + LANE), BF16),
                        pltpu.VMEM((1, nl), F32), pltpu.VMEM((1, nl), F32), pltpu.VMEM((HEAD_DIM, nl), F32),
                        pltpu.VMEM((2 * SEL_TILE, nl), F32), pltpu.VMEM((2 * SEL_TILE, nl), F32)],
        compiler_params=_params(("parallel", "parallel", "arbitrary")),
        name="nsa_prompt",
    )(qhm, kc, vct, kaug, vts, kwin, vtw, gt, tc, ts, tw)


def _row_softmax_parts(s, s_new):
    m = jnp.maximum(jnp.max(s, axis=1, keepdims=True), s_new)
    p = jnp.exp2(s - m)
    pn = jnp.exp2(s_new - m)
    return p, pn, jnp.sum(p, axis=1, keepdims=True) + pn


def _nsa_sample_cmp_kernel(h_ref, pek_ref, w1k_ref, pev_ref, w1v_ref, w2ka_ref, w2kb_ref, w2va_ref, w2vb_ref,
                           q_ref, tcs_ref, ovs_ref, oc_ref, idx_ref, *, n_sub, qblk):
    hh = h_ref[...]
    ak, av = _compress_finish(hh[:, 0:4 * CMP_HID], hh[:, 4 * CMP_HID:8 * CMP_HID],
                              pek_ref, w1k_ref, pev_ref, w1v_ref, w2ka_ref, w2kb_ref, n_sub)
    kc = (_dot(ak[0], w2ka_ref[...]) + _dot(ak[1], w2kb_ref[...])).astype(BF16)
    vc = (_dot(av[0], w2va_ref[...]) + _dot(av[1], w2vb_ref[...])).astype(BF16)
    q8 = q_ref[...].astype(BF16)
    s = _dot_nt(q8, kc) + tcs_ref[...]
    m = jnp.max(s, axis=1, keepdims=True)
    p = jnp.exp2(s - m)
    pc = p / jnp.sum(p, axis=1, keepdims=True)
    oc_ref[...] = _dot(pc.astype(BF16), vc)
    hrow = lax.broadcasted_iota(jnp.int32, (N_HEADS, n_sub), 0)
    rows = []
    for g in range(KV_HEADS):
        rows.append(jnp.sum(jnp.where(hrow // HPG == g, pc, 0.0), axis=0, keepdims=True))
    psum = jnp.concatenate(rows + [jnp.zeros((N_HEADS - KV_HEADS, n_sub), F32)], axis=0)
    imp = _dot_f32(psum, ovs_ref[...])
    jl = lax.broadcasted_iota(jnp.int32, imp.shape, 1)
    forced = (jl == 0) | (jl == qblk) | (jl == qblk - 1)
    score = jnp.where(forced, FORCE, jnp.where(jl <= qblk, imp, -FORCE))
    _, picked = _select_blocks(score, jl, 1)
    lane = lax.broadcasted_iota(jnp.int32, (N_HEADS, LANE), 1)
    out = jnp.zeros((N_HEADS, LANE), jnp.int32)
    for r, jm in enumerate(picked):
        out = jnp.where(lane == r, jm, out)
    idx_ref[...] = out


def _nsa_sample_cmp(hbuf, q8, w, tcs, ovs, *, qblk):
    dbsz, n_sub, _ = hbuf.shape
    c2 = lambda b: (0, 0)
    names = ["pek", "w1k", "pev", "w1v", "w2ka", "w2kb", "w2va", "w2vb"]
    return pl.pallas_call(
        functools.partial(_nsa_sample_cmp_kernel, n_sub=n_sub, qblk=qblk),
        grid=(dbsz,),
        in_specs=[pl.BlockSpec((None, n_sub, 8 * CMP_HID), lambda b: (b, 0, 0))]
        + [pl.BlockSpec(w[k].shape, c2) for k in names]
        + [pl.BlockSpec((None, N_HEADS, LANE), lambda b: (b, 0, 0)),
           pl.BlockSpec(tcs.shape, c2), pl.BlockSpec(ovs.shape, c2)],
        out_specs=[pl.BlockSpec((None, N_HEADS, LANE), lambda b: (b, 0, 0)),
                   pl.BlockSpec((None, N_HEADS, LANE), lambda b: (b, 0, 0))],
        out_shape=[jax.ShapeDtypeStruct((dbsz, N_HEADS, LANE), F32),
                   jax.ShapeDtypeStruct((dbsz, N_HEADS, LANE), jnp.int32)],
        compiler_params=_params(("parallel",)),
        name="nsa_sample_cmp",
    )(hbuf, *[w[k] for k in names], q8, tcs, ovs)


def _nsa_sample_sel_kernel(idx_ref, pt_ref, *refs, qblk, n_win):
    nblk = KV_HEADS * N_SEL_BLOCKS
    blocks = refs[:nblk]
    (q_ref, slcn_ref, winn_ref, state_ref, gs_ref, oc_ref, tnear_ref, b0_ref, tws_ref,
     o_ref, wout_ref) = refs[nblk:]
    b = pl.program_id(0)
    q8f = q_ref[...]
    q8 = q8f.astype(BF16)
    hrow = lax.broadcasted_iota(jnp.int32, (N_HEADS, LANE), 0)
    b0 = b0_ref[:, 0:1]

    slcn = slcn_ref[0:1, :]
    s_new = jnp.sum(q8f * slcn[:, 0:LANE], axis=1, keepdims=True) + b0
    halves = PAGE // SEL_LEN
    lane = lax.broadcasted_iota(jnp.int32, (N_HEADS, PAGE), 1)
    o_sel = []
    for g in range(KV_HEADS):
        ss = []
        for r in range(N_SEL_BLOCKS):
            blk = idx_ref[b, g * N_SEL_BLOCKS + r]
            s = _dot(q8, blocks[g * N_SEL_BLOCKS + r][0].astype(BF16))
            near = jnp.where(blk // halves == qblk // halves - 1, tnear_ref[...], 0.0)
            ok = (blk < qblk) & (lane // SEL_LEN == blk % halves)
            ss.append(jnp.where(ok, s + near, NEG))
        s = jnp.concatenate(ss, axis=1)
        p, pn, l = _row_softmax_parts(s, s_new)
        o = pn * slcn[:, LANE:2 * LANE]
        for r in range(N_SEL_BLOCKS):
            pr = p[:, r * PAGE:(r + 1) * PAGE].astype(BF16)
            o = o + _dot_nt(pr, blocks[g * N_SEL_BLOCKS + r][1].astype(BF16))
        o_sel.append(o / l)
    o_s = jnp.where(hrow // HPG == 0, o_sel[0], o_sel[1])

    winn = winn_ref[0:1, :]
    sw = _dot(q8, state_ref[0].astype(BF16)) + tws_ref[...]
    sw_new = jnp.sum(q8f * winn[:, 0:LANE], axis=1, keepdims=True) + b0
    p, pn, l = _row_softmax_parts(sw, sw_new)
    o_w = (_dot_nt(p.astype(BF16), state_ref[1].astype(BF16)) + pn * winn[:, LANE:2 * LANE]) / l

    o_ref[...] = gs_ref[0] * oc_ref[...] + gs_ref[1] * o_s + gs_ref[2] * o_w
    wlane = lax.broadcasted_iota(jnp.int32, (LANE, n_win), 1)
    for kv in range(2):
        newcol = jnp.broadcast_to(winn[:, kv * LANE:(kv + 1) * LANE], (LANE, LANE)).T
        newcol = jnp.concatenate([newcol] * (n_win // LANE), axis=1)
        wout_ref[kv] = jnp.where(wlane == n_win - 1, newcol, pltpu.roll(state_ref[kv], n_win - 1, 1))


def _nsa_sample_sel(idx, page_table, slc_cache_t, q8, slc_new, win_new, state, gs, oc, tnear, b0, tws, *, qblk):
    dbsz, n_pages = page_table.shape
    n_win = state.shape[-1]
    halves = PAGE // SEL_LEN

    def blk_spec(k):
        def imap(b, idx_ref, pt_ref):
            j = idx_ref[b, k]
            return (pt_ref[b, jnp.minimum(j // halves, n_pages - 1)], 0, 0, 0)
        return pl.BlockSpec((None, 2, LANE, PAGE), imap)

    def per_b(shape):
        nd = len(shape)
        return pl.BlockSpec((None,) + tuple(shape[1:]), lambda b, i, p: (b,) + (0,) * (nd - 1))

    def const(shape):
        nd = len(shape)
        return pl.BlockSpec(tuple(shape), lambda b, i, p: (0,) * nd)

    nblk = KV_HEADS * N_SEL_BLOCKS
    grid_spec = pltpu.PrefetchScalarGridSpec(
        num_scalar_prefetch=2, grid=(dbsz,),
        in_specs=[blk_spec(k) for k in range(nblk)]
        + [per_b(q8.shape), per_b(slc_new.shape), per_b(win_new.shape), per_b(state.shape), per_b(gs.shape),
           per_b(oc.shape), const(tnear.shape), const(b0.shape), const(tws.shape)],
        out_specs=[pl.BlockSpec((N_HEADS, LANE), lambda b, i, p: (b, 0)),
                   pl.BlockSpec((None, 2, LANE, n_win), lambda b, i, p: (b, 0, 0, 0))])
    return pl.pallas_call(
        functools.partial(_nsa_sample_sel_kernel, qblk=qblk, n_win=n_win),
        grid_spec=grid_spec,
        out_shape=[jax.ShapeDtypeStruct((dbsz * N_HEADS, LANE), F32),
                   jax.ShapeDtypeStruct((dbsz, 2, LANE, n_win), F32)],
        compiler_params=_params(("arbitrary",)),
        name="nsa_sample_sel",
    )(idx, page_table, *([slc_cache_t] * nblk), q8, slc_new, win_new, state, gs, oc, tnear, b0, tws)


def _merge_kernel(x_ref, o_ref, u_ref, vn_ref, ga_ref, gb_ref, ws_ref, bs_ref, wbn_ref, wbg_ref, wout_ref,
                  lng_ref, x1_ref, *, tm):
    chunk = ws_ref.shape[1]
    keep = (lax.broadcasted_iota(jnp.int32, (chunk, chunk), 0)
            >= lax.broadcasted_iota(jnp.int32, (chunk, chunk), 1))
    vb = vn_ref[...].astype(BF16)
    cols = []
    for gg in range(GM_GROUPS):
        wt = jnp.where(keep, ws_ref[gg], 0.0).astype(BF16)
        rows = [_dot(wt, vb[c * chunk:(c + 1) * chunk, gg * LANE:(gg + 1) * LANE]) for c in range(tm // chunk)]
        cols.append(jnp.concatenate(rows, axis=0) if len(rows) > 1 else rows[0])
    s = jnp.concatenate(cols, axis=1) + jnp.concatenate([bs_ref[...]] * (tm // chunk), axis=0)
    o_gm = (u_ref[...].astype(F32) * s).astype(BF16)
    if len(wbn_ref.shape) == 2:
        a = _dot(o_ref[...].astype(BF16), wbn_ref[...])
    else:
        a = jnp.zeros((tm, D_MODEL), F32)
        for hh in range(N_HEADS):
            a += _dot(o_ref[pl.ds(hh, tm, stride=N_HEADS), :].astype(BF16), wbn_ref[hh])
    bm = _dot(o_gm, wbg_ref[...])
    mixed = (ga_ref[...].astype(F32) * a + gb_ref[...].astype(F32) * bm).astype(BF16)
    x1_ref[...] = x_ref[...] + _rms(_dot(mixed, wout_ref[...]), lng_ref[...])


def _merge(x, o_nsa, u, vn, ga, gb, ws, bs, wbn, w, *, tm, name):
    n = x.shape[0]
    row = lambda i: (i, 0)
    c2 = lambda i: (0, 0)
    c3 = lambda i: (0, 0, 0)
    ws_spec = pl.BlockSpec(ws.shape, c3)
    o_rows = o_nsa.shape[0] // n
    return pl.pallas_call(
        functools.partial(_merge_kernel, tm=tm),
        grid=(n // tm,),
        in_specs=[pl.BlockSpec((tm, D_MODEL), row), pl.BlockSpec((tm * o_rows, o_nsa.shape[1]), row),
                  pl.BlockSpec((tm, GM_DIM), row), pl.BlockSpec((tm, GM_DIM), row),
                  pl.BlockSpec((tm, D_MODEL), row), pl.BlockSpec((tm, D_MODEL), row),
                  ws_spec, pl.BlockSpec(bs.shape, c2),
                  pl.BlockSpec(wbn.shape, c2 if wbn.ndim == 2 else c3), pl.BlockSpec(w["wbg"].shape, c2),
                  pl.BlockSpec(w["wout"].shape, c2), pl.BlockSpec((1, D_MODEL), c2)],
        out_specs=pl.BlockSpec((tm, D_MODEL), row),
        out_shape=jax.ShapeDtypeStruct((n, D_MODEL), F32),
        compiler_params=_params(("parallel",)),
        name=name,
    )(x, o_nsa, u, vn, ga, gb, ws, bs, wbn, w["wbg"], w["wout"], w["ln_mix_post"])


def _ffn_kernel(x_ref, p_ref, lnf_ref, wg_ref, wu_ref, wd_ref, lnp_ref, wple_ref, wpg_ref, lne_ref, y_ref,
                h2_ref, acc_ref):
    j = pl.program_id(1)

    @pl.when(j == 0)
    def _():
        h2_ref[...] = _rms(x_ref[...], lnf_ref[...]).astype(BF16)
        acc_ref[...] = jnp.zeros(acc_ref.shape, F32)

    h2 = h2_ref[...]
    act = (jax.nn.silu(_dot(h2, wg_ref[...])) * _dot(h2, wu_ref[...])).astype(BF16)
    acc_ref[...] += _dot(act, wd_ref[...])

    @pl.when(j == pl.num_programs(1) - 1)
    def _():
        x2 = x_ref[...] + _rms(acc_ref[...], lnp_ref[...])
        e = _dot(p_ref[...].astype(BF16), wple_ref[...])
        gp = jax.nn.sigmoid(_dot(x2.astype(BF16), wpg_ref[...]))
        y_ref[...] = x2 + _rms(gp * e, lne_ref[...])


def _ffn(x1, ple, w, *, tm, name):
    n = x1.shape[0]
    row = lambda i, j: (i, 0)
    c2 = lambda i, j: (0, 0)
    return pl.pallas_call(
        _ffn_kernel,
        grid=(n // tm, D_FF // FF_CHUNK),
        in_specs=[pl.BlockSpec((tm, D_MODEL), row), pl.BlockSpec((tm, PLE_DIM), row),
                  pl.BlockSpec((1, D_MODEL), c2),
                  pl.BlockSpec((D_MODEL, FF_CHUNK), lambda i, j: (0, j)),
                  pl.BlockSpec((D_MODEL, FF_CHUNK), lambda i, j: (0, j)),
                  pl.BlockSpec((FF_CHUNK, D_MODEL), lambda i, j: (j, 0)),
                  pl.BlockSpec((1, D_MODEL), c2), pl.BlockSpec(w["wple"].shape, c2),
                  pl.BlockSpec(w["wpg"].shape, c2), pl.BlockSpec((1, D_MODEL), c2)],
        out_specs=pl.BlockSpec((tm, D_MODEL), row),
        out_shape=jax.ShapeDtypeStruct((n, D_MODEL), F32),
        scratch_shapes=[pltpu.VMEM((tm, D_MODEL), BF16), pltpu.VMEM((tm, D_MODEL), F32)],
        compiler_params=_params(("parallel", "arbitrary")),
        name=name,
    )(x1, ple, w["ln_ffn_pre"], w["wfg"], w["wfu"], w["wfd"], w["ln_ffn_post"], w["wple"], w["wpg"],
      w["ln_ple_post"])


def _prep_weights(i, ln_mix_pre, w_in, cmp_pe_k, cmp_w1_k, cmp_w2_k, cmp_pe_v, cmp_w1_v, cmp_w2_v, gm_ln_g,
                  gm_ln_b, w_branch_nsa, w_branch_gm, w_out, ln_mix_post, ln_ffn_pre, w_ffn_gate, w_ffn_up,
                  w_ffn_down, ln_ffn_post, w_ple, w_ple_gate, ln_ple_post):
    q_dim = N_HEADS * HEAD_DIM
    kv3 = 6 * KV_HEADS * HEAD_DIM
    n_gate = 3 * N_HEADS
    win = w_in[i]
    o = 0
    wq = win[:, o:o + q_dim]; o += q_dim
    wkv = win[:, o:o + kv3]; o += kv3
    wg = win[:, o:o + n_gate]; o += n_gate
    wuv = win[:, o:o + 2 * GM_DIM]; o += 2 * GM_DIM
    wmg = win[:, o:o + 2 * D_MODEL]
    wq4 = wq.reshape(D_MODEL, N_HEADS, 1, HEAD_DIM)
    half = (np.arange(N_HEADS) // HPG)[None, :, None, None] == np.arange(KV_HEADS)[None, None, :, None]
    wq_pad = jnp.where(half, wq4, 0.0).reshape(D_MODEL, N_HEADS * LANE)
    wg_pad = jnp.pad(wg, ((0, 0), (0, LANE - n_gate)))

    def blockdiag(w1):
        w1s = w1.reshape(CMP_LEN // CMP_STRIDE, CMP_STRIDE, HEAD_DIM, CMP_HID).transpose(1, 2, 0, 3)
        w1s = w1s.reshape(CMP_STRIDE, HEAD_DIM, 2 * CMP_HID)
        z = jnp.zeros_like(w1s)
        bd = jnp.concatenate([jnp.concatenate([w1s, z], axis=2), jnp.concatenate([z, w1s], axis=2)], axis=1)
        return bd.reshape(CMP_STRIDE // 2, 2 * 2 * HEAD_DIM, 4 * CMP_HID)

    wbn4 = w_branch_nsa[i].reshape(N_HEADS, 1, HEAD_DIM, D_MODEL)
    wbn_heads = jnp.where(half.reshape(N_HEADS, KV_HEADS, 1, 1), wbn4, 0.0).reshape(N_HEADS, LANE, D_MODEL)

    zk = jnp.zeros_like(cmp_w2_k[i])
    zv = jnp.zeros_like(cmp_w2_v[i])
    b = lambda a: a.astype(BF16)
    r = lambda a: a[i][None, :].astype(F32)
    return dict(
        ln_mix_pre=r(ln_mix_pre), wq=b(wq_pad), wkv=b(wkv), wg=b(wg_pad), wuv=b(wuv), wmg=b(wmg),
        gm_ln_g=r(gm_ln_g), gm_ln_b=r(gm_ln_b),
        wbk=b(blockdiag(cmp_w1_k[i])), wbv=b(blockdiag(cmp_w1_v[i])),
        pek=jnp.broadcast_to(cmp_pe_k[i].reshape(1, -1), (8, CMP_LEN * HEAD_DIM)).astype(F32), w1k=b(cmp_w1_k[i]),
        pev=jnp.broadcast_to(cmp_pe_v[i].reshape(1, -1), (8, CMP_LEN * HEAD_DIM)).astype(F32), w1v=b(cmp_w1_v[i]),
        w2ka=b(jnp.concatenate([cmp_w2_k[i], zk], axis=1)), w2kb=b(jnp.concatenate([zk, cmp_w2_k[i]], axis=1)),
        w2va=b(jnp.concatenate([cmp_w2_v[i], zv], axis=1)), w2vb=b(jnp.concatenate([zv, cmp_w2_v[i]], axis=1)),
        w2vt=b(cmp_w2_v[i].T),
        wbn=b(w_branch_nsa[i]), wbn_heads=b(wbn_heads), wbg=b(w_branch_gm[i]), wout=b(w_out[i]), ln_mix_post=r(ln_mix_post),
        ln_ffn_pre=r(ln_ffn_pre), wfg=b(w_ffn_gate[i]), wfu=b(w_ffn_up[i]), wfd=b(w_ffn_down[i]),
        ln_ffn_post=r(ln_ffn_post), wple=b(w_ple[i]), wpg=b(w_ple_gate[i]), ln_ple_post=r(ln_ple_post),
    )


def _gate_perm():
    return np.array([hg * 3 + br for br in range(3) for hg in range(N_HEADS)])


def _prompt_layer(x, ple, w, rel_bias, gm_ws, gm_bs):
    bsz, seq, _ = x.shape
    assert seq % SEL_TILE == 0 and seq // SEL_LEN <= NJ and seq >= WINDOW
    n = bsz * seq
    tm = 256
    xf = x.reshape(n, D_MODEL)
    wp = dict(w)
    perm = _gate_perm()
    wp["wg"] = jnp.concatenate([w["wg"][:, perm], w["wg"][:, len(perm):]], axis=1)
    (qhm, cmp, cmpt, slct, wint, kaug, vts, kwin, vtw, gt, u, vn, ga, gb) = _inproj(
        xf, wp, prompt=True, seq=seq, tm=tm)
    kc, vct = _compress_prompt(cmp, w, bsz=bsz, seq=seq)

    o_nsa = _nsa_prompt(qhm, kc, vct, kaug, vts, kwin, vtw, gt, _bias_tables(rel_bias), bsz=bsz, seq=seq)

    bs_tile = jnp.repeat(gm_bs.T, LANE, axis=1).astype(F32)
    x1 = _merge(xf, o_nsa, u, vn, ga, gb, gm_ws.astype(F32), bs_tile, w["wbn"], w, tm=512, name="merge_prompt")
    y = _ffn(x1, ple.reshape(n, PLE_DIM), w, tm=512, name="ffn_prompt")

    def rows_form(at):
        return at.reshape(bsz, 2, KV_HEADS, HEAD_DIM, at.shape[-1]).transpose(0, 4, 1, 2, 3)

    return (y.reshape(bsz, seq, D_MODEL), rows_form(cmpt), rows_form(slct),
            rows_form(wint[:, :, seq - min(WINDOW, seq):]))


def _sample_layer(x, ple, w, rel_bias, gm_ws, gm_bs, cache_cmp, cache_slc, win_buf, page_table):
    dbsz, nq, _ = x.shape
    assert nq == 1
    n_pool = cache_cmp.shape[0]
    n_pages = page_table.shape[1]
    past = n_pages * PAGE
    n_win = win_buf.shape[1]
    assert past % GM_CHUNK == 0 and n_win == WINDOW and past >= WINDOW and dbsz % 8 == 0
    xf = x.reshape(dbsz, D_MODEL)
    q, cmp, slc, win, gate, u, vn, ga, gb = _inproj(xf, w, prompt=False, seq=1, tm=dbsz)

    pps = min(16, n_pages)
    hbuf = _compress_paged(_pages_t(cache_cmp), page_table, w, pps=pps)
    n_sub = past // CMP_STRIDE
    qblk = past // SEL_LEN
    n_sel_pad = -(-(qblk + 1) // LANE) * LANE
    sh = _shifted_bias(rel_bias)
    i = np.arange(n_sub)
    d_c = past - (CMP_STRIDE * i + CMP_LEN - 1)
    tcs = _sample_table(sh, d_c, i < n_sub - 1)
    ovs = jnp.asarray(_overlap_t(n_sub, n_sel_pad).T)
    q8 = q.reshape(dbsz, N_HEADS, LANE)
    oc, idx = _nsa_sample_cmp(hbuf, q8, w, tcs, ovs, qblk=qblk)
    idx2 = idx[:, 0:KV_HEADS, 0:N_SEL_BLOCKS].reshape(dbsz, KV_HEADS * N_SEL_BLOCKS)

    l = np.arange(PAGE)
    tnear = _sample_table(sh, PAGE - l, l >= 0)
    b0 = jnp.broadcast_to(sh[0][:, None] * LOG2E, (N_HEADS, LANE)).astype(F32)
    kpos = np.arange(n_win)
    tws = _sample_table(sh, n_win - kpos, kpos >= 0)
    gs = gate[:, 0:3 * N_HEADS].reshape(dbsz, N_HEADS, 3).transpose(0, 2, 1)
    gs = jnp.broadcast_to(gs[..., None], (dbsz, 3, N_HEADS, LANE))
    o8, win_out = _nsa_sample_sel(
        idx2, page_table, _pages_t(cache_slc), q8,
        jnp.broadcast_to(slc[:, None, :], (dbsz, 8, 256)), jnp.broadcast_to(win[:, None, :], (dbsz, 8, 256)),
        _pages_t(win_buf), gs, oc, tnear, b0, tws, qblk=qblk)
    win_out = win_out.reshape(dbsz, 2, KV_HEADS, HEAD_DIM, n_win).transpose(0, 4, 1, 2, 3)

    ws_diag = gm_ws[:, 0, 0][:, None, None] * jnp.eye(dbsz, dtype=F32)[None]
    bs_tile = jnp.broadcast_to(jnp.repeat(gm_bs[:, 0], LANE)[None, :], (dbsz, GM_GROUPS * LANE)).astype(F32)
    x1 = _merge(xf, o8, u, vn, ga, gb, ws_diag, bs_tile, w["wbn_heads"], w, tm=dbsz, name="merge_sample")
    y = _ffn(x1, ple.reshape(dbsz, PLE_DIM), w, tm=dbsz, name="ffn_sample")
    kvshape = (dbsz, 1, 2, KV_HEADS, HEAD_DIM)
    return (y.reshape(dbsz, 1, D_MODEL), cmp.reshape(kvshape), slc.reshape(kvshape),
            win_out, vn.reshape(dbsz, 1, GM_DIM))


def kernel(x_prompt, x_sample, cache_cmp_kv, cache_slc_kv, state_win_kv, page_table, p_prompt, p_sample, rel_bias,
           ln_mix_pre, w_in, cmp_pe_k, cmp_w1_k, cmp_w2_k, cmp_pe_v, cmp_w1_v, cmp_w2_v, gm_ln_g, gm_ln_b, gm_ws,
           gm_bs, w_branch_nsa, w_branch_gm, w_out, ln_mix_post, ln_ffn_pre, w_ffn_gate, w_ffn_up, w_ffn_down,
           ln_ffn_post, w_ple, w_ple_gate, ln_ple_post):
    depth = w_in.shape[0]
    xp, xs = x_prompt, x_sample
    outs = [[] for _ in range(7)]
    for i in range(depth):
        w = _prep_weights(i, ln_mix_pre, w_in, cmp_pe_k, cmp_w1_k, cmp_w2_k, cmp_pe_v, cmp_w1_v, cmp_w2_v, gm_ln_g,
                          gm_ln_b, w_branch_nsa, w_branch_gm, w_out, ln_mix_post, ln_ffn_pre, w_ffn_gate, w_ffn_up,
                          w_ffn_down, ln_ffn_post, w_ple, w_ple_gate, ln_ple_post)
        xp, c_p, s_p, w_p = _prompt_layer(xp, p_prompt[i], w, rel_bias, gm_ws[i], gm_bs[i])
        xs, c_s, s_s, w_s, v_s = _sample_layer(xs, p_sample[i], w, rel_bias, gm_ws[i], gm_bs[i], cache_cmp_kv[i],
                                               cache_slc_kv[i], state_win_kv[i], page_table)
        for lst, val in zip(outs, (c_p, s_p, w_p, c_s, s_s, w_s, v_s)):
            lst.append(val)
    return (xp, xs) + tuple(jnp.stack(o) for o in outs)
```

```python
import functools
import math

import jax
import jax.numpy as jnp
import numpy as np
from jax import lax
from jax.experimental import pallas as pl
from jax.experimental.pallas import tpu as pltpu

F32 = jnp.float32
BF16 = jnp.bfloat16

D_MODEL = 1024
N_HEADS = 8
KV_HEADS = 2
HPG = N_HEADS // KV_HEADS
HEAD_DIM = 64
CMP_LEN = 32
CMP_STRIDE = 16
CMP_HID = 2 * HEAD_DIM
SEL_LEN = 64
N_SEL_BLOCKS = 16
WINDOW = 512
Q_BLOCK = 128
GM_GROUPS = 4
GM_CHUNK = 128
GM_DIM = D_MODEL // 2
D_FF = -(-8 * D_MODEL // (3 * 256)) * 256
PLE_DIM = 256
N_BUCKETS = 32
MAX_DISTANCE = 128
EPS = 1e-6
NEG = -1e30
LOG2E = 1.4426950408889634
FORCE = 1e6
PAGE = 128

LANE = 128
NJ = 128
SEL_TILE = 256
FF_CHUNK = D_FF // 2
VMEM_LIMIT = 56 * 1024 * 1024


def _dot(a, b):
    return jnp.dot(a, b, preferred_element_type=F32)


def _dot_nt(a, b):
    return lax.dot_general(a, b, (((1,), (1,)), ((), ())), preferred_element_type=F32)


def _dot_f32(a, b):
    return jnp.dot(a, b, preferred_element_type=F32, precision=lax.Precision.HIGHEST)


def _rms(x, g):
    return x * lax.rsqrt(jnp.mean(x * x, axis=-1, keepdims=True) + EPS) * g


def _params(sem):
    return pltpu.CompilerParams(dimension_semantics=sem, vmem_limit_bytes=VMEM_LIMIT)


def _t5_bucket_np(d):
    d = np.maximum(d, 0)
    max_exact = N_BUCKETS // 2
    ratio = (np.log(np.maximum(d, 1).astype(np.float32) / np.float32(max_exact))
             / np.float32(math.log(MAX_DISTANCE / max_exact)))
    large = np.minimum(max_exact + (ratio * np.float32(N_BUCKETS - max_exact)).astype(np.int32), N_BUCKETS - 1)
    return np.where(d < max_exact, d, large)


_BUCKET = _t5_bucket_np(np.arange(MAX_DISTANCE + 1))
assert _BUCKET[MAX_DISTANCE] == N_BUCKETS - 1


def _shifted_bias(rel_bias):
    return rel_bias[_BUCKET] - rel_bias[N_BUCKETS - 1][None, :]


def _sample_table(sh, dist, valid):
    t = sh[np.minimum(np.maximum(dist, 0), MAX_DISTANCE)] * LOG2E
    return jnp.where(valid[:, None], t, NEG).T.astype(F32)


_BUCKET_START = [int(np.argmax(_BUCKET >= k)) for k in range(N_BUCKETS)]
_BAND_ROWS = 3 * (Q_BLOCK // CMP_STRIDE)


def _bias_tables_kernel(rb_ref, tc_ref, ts_ref, tw_ref):
    def fill(store, rows, chunk, c0, stride, dmax):
        def body(c, carry):
            r0 = pl.multiple_of(c * chunk, chunk)
            r = r0 + lax.broadcasted_iota(jnp.int32, (chunk, Q_BLOCK), 0)
            d = c0 + lax.broadcasted_iota(jnp.int32, (chunk, Q_BLOCK), 1) - stride * r
            ok = (d >= 0) & (d <= dmax)
            dc = jnp.minimum(d, MAX_DISTANCE)
            for hd in range(N_HEADS):
                v = jnp.full((chunk, Q_BLOCK), rb_ref[0, hd], F32)
                for k in range(1, N_BUCKETS):
                    v = jnp.where(dc >= _BUCKET_START[k], rb_ref[k, hd], v)
                v = (v - rb_ref[N_BUCKETS - 1, hd]) * LOG2E
                store(hd // HPG, r0, chunk, hd % HPG, jnp.where(ok, v, NEG))
            return carry
        lax.fori_loop(0, rows // chunk, body, 0)

    def lanes(hh):
        return slice(hh * Q_BLOCK, (hh + 1) * Q_BLOCK)

    def st_c(g, r0, n, hh, v):
        tc_ref[g, pl.ds(r0, n), lanes(hh)] = v

    def st_w(g, r0, n, hh, v):
        tw_ref[g, pl.ds(r0, n), lanes(hh)] = v

    big = 1 << 30
    fill(st_c, _BAND_ROWS, 8, 2 * Q_BLOCK - (CMP_LEN - 1), CMP_STRIDE, big)
    for par in range(SEL_TILE // Q_BLOCK):
        def st_s(g, r0, n, hh, v, par=par):
            ts_ref[g, par, pl.ds(r0, n), lanes(hh)] = v
        fill(st_s, 2 * SEL_TILE, 64, SEL_TILE + Q_BLOCK * par, 1, big)
    fill(st_w, WINDOW + Q_BLOCK, 64, WINDOW, 1, WINDOW)


def _bias_tables(rel_bias):
    nl = HPG * Q_BLOCK
    return pl.pallas_call(
        _bias_tables_kernel,
        in_specs=[pl.BlockSpec(memory_space=pltpu.SMEM)],
        out_shape=[jax.ShapeDtypeStruct((KV_HEADS, _BAND_ROWS, nl), F32),
                   jax.ShapeDtypeStruct((KV_HEADS, SEL_TILE // Q_BLOCK, 2 * SEL_TILE, nl), F32),
                   jax.ShapeDtypeStruct((KV_HEADS, WINDOW + Q_BLOCK, nl), F32)],
        compiler_params=pltpu.CompilerParams(vmem_limit_bytes=VMEM_LIMIT),
        name="bias_tables",
    )(rel_bias.astype(F32))


def _overlap_t(n_cmp, n_sel_pad):
    ci = np.arange(n_cmp)[None, :] * CMP_STRIDE
    sj = np.arange(n_sel_pad)[:, None] * SEL_LEN
    return ((ci <= sj + SEL_LEN - 1) & (ci + CMP_LEN - 1 >= sj)).astype(np.float32)


def _overlap_offsets():
    per = SEL_LEN // CMP_STRIDE
    ov = _overlap_t(16 * per, 16)
    jj, ii = np.nonzero(ov)
    offs = sorted({int(i - per * j) for j, i in zip(jj, ii)})
    band = np.isin(np.arange(16 * per)[None, :] - per * np.arange(16)[:, None], offs)
    assert (band == (ov > 0)).all() and min(offs) >= -8
    return offs


_OVERLAP_OFFSETS = _overlap_offsets()


def _inproj_kernel(x_ref, lng_ref, wq_ref, wkv_ref, wg_ref, wuv_ref, wmg_ref, glg_ref, glb_ref, *outs,
                   prompt, seq, tm):
    h = _rms(x_ref[...], lng_ref[...]).astype(BF16)
    q = _dot(h, wq_ref[...]) * (HEAD_DIM ** -0.5 * LOG2E)
    kv = _dot(h, wkv_ref[...])
    gate = jax.nn.sigmoid(_dot(h, wg_ref[...]))
    uv = jax.nn.gelu(_dot(h, wuv_ref[...]))
    u = uv[:, :GM_DIM]
    v = uv[:, GM_DIM:]
    mu = jnp.mean(v, axis=-1, keepdims=True)
    var = jnp.mean(jnp.square(v - mu), axis=-1, keepdims=True)
    vn = (v - mu) * lax.rsqrt(var + EPS) * glg_ref[...] + glb_ref[...]
    mg = jax.nn.sigmoid(_dot(h, wmg_ref[...]))
    cmp, slc, win = kv[:, 0:256], kv[:, 256:512], kv[:, 512:768]
    if prompt:
        (q_ref, cmp_ref, cmpt_ref, slct_ref, wint_ref, kaug_ref, vts_ref, kwin_ref, vtw_ref, gt_ref,
         u_ref, vn_ref, ga_ref, gb_ref) = outs
        for hh in range(N_HEADS):
            q_ref[hh] = q[:, hh * LANE:(hh + 1) * LANE].astype(BF16)
        pos = (pl.program_id(0) * tm + lax.broadcasted_iota(jnp.int32, (tm, NJ), 0)) % seq
        onehot = (lax.broadcasted_iota(jnp.int32, (tm, NJ), 1) == pos // SEL_LEN)
        kaug_ref[:, 0:NJ] = onehot.astype(BF16)
        kaug_ref[:, NJ:NJ + LANE] = slc[:, 0:LANE].astype(BF16)
        kwin_ref[...] = win[:, 0:LANE].astype(BF16)
        slc_t = slc.T
        win_t = win.T
        for g in range(KV_HEADS):
            r0 = LANE + HEAD_DIM * g
            for c in range(tm // SEL_TILE):
                vts_ref[g, c] = slc_t[r0:r0 + HEAD_DIM, c * SEL_TILE:(c + 1) * SEL_TILE].astype(BF16)
            for c in range(tm // LANE):
                vtw_ref[g, c] = win_t[r0:r0 + HEAD_DIM, c * LANE:(c + 1) * LANE].astype(BF16)
        gt_ref[...] = gate.T[0:32, :]
        cmpt_ref[...] = cmp.T
        slct_ref[...] = slc_t
        wint_ref[...] = win_t
    else:
        q_ref, cmp_ref, slc_ref, win_ref, gate_ref, u_ref, vn_ref, ga_ref, gb_ref = outs
        q_ref[...] = q
        gate_ref[...] = gate
        slc_ref[...] = slc
        win_ref[...] = win
    cmp_ref[...] = cmp
    u_ref[...] = u.astype(u_ref.dtype)
    vn_ref[...] = vn.astype(vn_ref.dtype)
    ga_ref[...] = mg[:, :D_MODEL].astype(ga_ref.dtype)
    gb_ref[...] = mg[:, D_MODEL:].astype(gb_ref.dtype)


def _inproj(x, w, *, prompt, seq, tm):
    n = x.shape[0]
    row = lambda i: (i, 0)
    const = lambda i: (0, 0)
    in_specs = [pl.BlockSpec((tm, D_MODEL), row), pl.BlockSpec((1, D_MODEL), const),
                pl.BlockSpec(w["wq"].shape, const), pl.BlockSpec(w["wkv"].shape, const),
                pl.BlockSpec(w["wg"].shape, const), pl.BlockSpec(w["wuv"].shape, const),
                pl.BlockSpec(w["wmg"].shape, const), pl.BlockSpec((1, GM_DIM), const),
                pl.BlockSpec((1, GM_DIM), const)]
    kv_shapes = [jax.ShapeDtypeStruct((n, 256), F32)] * 3
    kv_specs = [pl.BlockSpec((tm, 256), row)] * 3
    if prompt:
        act = BF16
        tiles = seq // tm
        kvt_shapes = [jax.ShapeDtypeStruct((n // seq, 256, seq), F32)] * 3
        kvt_specs = [pl.BlockSpec((None, 256, tm), lambda i: (i // tiles, 0, i % tiles))] * 3
        out_shape = ([jax.ShapeDtypeStruct((N_HEADS, n, LANE), BF16)] + kv_shapes[:1] + kvt_shapes + [
            jax.ShapeDtypeStruct((n, NJ + LANE), BF16),
            jax.ShapeDtypeStruct((KV_HEADS, n // SEL_TILE, HEAD_DIM, SEL_TILE), BF16),
            jax.ShapeDtypeStruct((n, LANE), BF16),
            jax.ShapeDtypeStruct((KV_HEADS, n // LANE, HEAD_DIM, LANE), BF16),
            jax.ShapeDtypeStruct((32, n), F32)])
        out_specs = ([pl.BlockSpec((N_HEADS, tm, LANE), lambda i: (0, i, 0))] + kv_specs[:1] + kvt_specs + [
            pl.BlockSpec((tm, NJ + LANE), row),
            pl.BlockSpec((KV_HEADS, tm // SEL_TILE, HEAD_DIM, SEL_TILE), lambda i: (0, i, 0, 0)),
            pl.BlockSpec((tm, LANE), row),
            pl.BlockSpec((KV_HEADS, tm // LANE, HEAD_DIM, LANE), lambda i: (0, i, 0, 0)),
            pl.BlockSpec((32, tm), lambda i: (0, i))])
    else:
        act = F32
        out_shape = ([jax.ShapeDtypeStruct((n, N_HEADS * LANE), F32)] + kv_shapes
                     + [jax.ShapeDtypeStruct((n, LANE), F32)])
        out_specs = [pl.BlockSpec((tm, N_HEADS * LANE), row)] + kv_specs + [pl.BlockSpec((tm, LANE), row)]
    out_shape += [jax.ShapeDtypeStruct((n, GM_DIM), act)] * 2 + [jax.ShapeDtypeStruct((n, D_MODEL), act)] * 2
    out_specs += [pl.BlockSpec((tm, GM_DIM), row)] * 2 + [pl.BlockSpec((tm, D_MODEL), row)] * 2
    return pl.pallas_call(
        functools.partial(_inproj_kernel, prompt=prompt, seq=seq, tm=tm),
        grid=(n // tm,), in_specs=in_specs, out_specs=out_specs, out_shape=out_shape,
        compiler_params=_params(("parallel",)),
        name="inproj_prompt" if prompt else "inproj_sample",
    )(x, w["ln_mix_pre"], w["wq"], w["wkv"], w["wg"], w["wuv"], w["wmg"], w["gm_ln_g"], w["gm_ln_b"])


def _row_pair(x_ref, r, n_sub):
    return jnp.concatenate([x_ref[pl.ds(r, n_sub, stride=CMP_STRIDE), :],
                            x_ref[pl.ds(r + 1, n_sub, stride=CMP_STRIDE), :]], axis=1).astype(BF16)


def _compress_finish(hk, hv, pek_ref, w1k_ref, pev_ref, w1v_ref, w2ka_ref, w2kb_ref, n_sub):
    pwk = _dot(pek_ref[...].astype(BF16), w1k_ref[...])[0:1, :]
    pwv = _dot(pev_ref[...].astype(BF16), w1v_ref[...])[0:1, :]

    def act(hh, g, pw):
        a0 = hh[:, 256 * g:256 * g + CMP_HID]
        a1 = hh[:, 256 * g + CMP_HID:256 * g + 2 * CMP_HID]
        return jax.nn.silu(a0 + pltpu.roll(a1, n_sub - 1, 0) + pw).astype(BF16)

    return [act(hk, 0, pwk), act(hk, 1, pwk)], [act(hv, 0, pwv), act(hv, 1, pwv)]


def _compress_prompt_kernel(cmpk_ref, cmpv_ref, wbk_ref, wbv_ref, pek_ref, w1k_ref, pev_ref, w1v_ref,
                            w2ka_ref, w2kb_ref, w2vt_ref, kc_ref, vct_ref, *, n_sub):
    hk = jnp.zeros((n_sub, 4 * CMP_HID), F32)
    hv = jnp.zeros((n_sub, 4 * CMP_HID), F32)
    for r in range(0, CMP_STRIDE, 2):
        hk += _dot(_row_pair(cmpk_ref, r, n_sub), wbk_ref[r // 2])
        hv += _dot(_row_pair(cmpv_ref, r, n_sub), wbv_ref[r // 2])
    ak, av = _compress_finish(hk, hv, pek_ref, w1k_ref, pev_ref, w1v_ref, w2ka_ref, w2kb_ref, n_sub)
    kc_ref[...] = (_dot(ak[0], w2ka_ref[...]) + _dot(ak[1], w2kb_ref[...])).astype(BF16)
    for g in range(KV_HEADS):
        vct_ref[g] = _dot_nt(w2vt_ref[...], av[g]).astype(BF16)


def _compress_prompt(cmp, w, *, bsz, seq):
    n_sub = seq // CMP_STRIDE
    const2 = lambda b: (0, 0)
    const3 = lambda b: (0, 0, 0)
    return pl.pallas_call(
        functools.partial(_compress_prompt_kernel, n_sub=n_sub),
        grid=(bsz,),
        in_specs=[pl.BlockSpec((seq, LANE), lambda b: (b, 0)), pl.BlockSpec((seq, LANE), lambda b: (b, 1)),
                  pl.BlockSpec(w["wbk"].shape, const3), pl.BlockSpec(w["wbv"].shape, const3),
                  pl.BlockSpec(w["pek"].shape, const2), pl.BlockSpec(w["w1k"].shape, const2),
                  pl.BlockSpec(w["pev"].shape, const2), pl.BlockSpec(w["w1v"].shape, const2),
                  pl.BlockSpec(w["w2ka"].shape, const2), pl.BlockSpec(w["w2kb"].shape, const2),
                  pl.BlockSpec(w["w2vt"].shape, const2)],
        out_specs=[pl.BlockSpec((None, n_sub, LANE), lambda b: (b, 0, 0)),
                   pl.BlockSpec((None, KV_HEADS, HEAD_DIM, n_sub), lambda b: (b, 0, 0, 0))],
        out_shape=[jax.ShapeDtypeStruct((bsz, n_sub, LANE), BF16),
                   jax.ShapeDtypeStruct((bsz, KV_HEADS, HEAD_DIM, n_sub), BF16)],
        compiler_params=_params(("parallel",)),
        name="compress_prompt",
    )(cmp, cmp, w["wbk"], w["wbv"], w["pek"], w["w1k"], w["pev"], w["w1v"], w["w2ka"], w["w2kb"], w["w2vt"])


def _compress_paged_kernel(pt_ref, *refs, pps):
    kpages, vpages = refs[:pps], refs[pps:2 * pps]
    wbk_ref, wbv_ref, h_ref, xk_ref, xv_ref = refs[2 * pps:]
    sub_per_page = PAGE // CMP_STRIDE
    n = pps * sub_per_page
    for k in range(pps):
        xk_ref[k * PAGE:(k + 1) * PAGE, :] = kpages[k][...].T
        xv_ref[k * PAGE:(k + 1) * PAGE, :] = vpages[k][...].T
    hk = jnp.zeros((n, 4 * CMP_HID), F32)
    hv = jnp.zeros((n, 4 * CMP_HID), F32)
    for r in range(0, CMP_STRIDE, 2):
        hk += _dot(_row_pair(xk_ref, r, n), wbk_ref[r // 2])
        hv += _dot(_row_pair(xv_ref, r, n), wbv_ref[r // 2])
    h_ref[:, 0:4 * CMP_HID] = hk
    h_ref[:, 4 * CMP_HID:8 * CMP_HID] = hv


def _compress_paged(cache_t, page_table, w, *, pps):
    dbsz, n_pages = page_table.shape
    sub_per_page = PAGE // CMP_STRIDE
    n_sub = n_pages * sub_per_page

    def page_spec(k, c):
        return pl.BlockSpec((None, None, LANE, PAGE), lambda b, t, pt: (pt[b, t * pps + k], c, 0, 0))

    const3 = lambda b, t, pt: (0, 0, 0)
    grid_spec = pltpu.PrefetchScalarGridSpec(
        num_scalar_prefetch=1, grid=(dbsz, n_pages // pps),
        in_specs=[page_spec(k, c) for c in range(2) for k in range(pps)]
        + [pl.BlockSpec(w["wbk"].shape, const3), pl.BlockSpec(w["wbv"].shape, const3)],
        out_specs=pl.BlockSpec((None, pps * sub_per_page, 8 * CMP_HID), lambda b, t, pt: (b, t, 0)),
        scratch_shapes=[pltpu.VMEM((pps * PAGE, LANE), F32), pltpu.VMEM((pps * PAGE, LANE), F32)])
    return pl.pallas_call(
        functools.partial(_compress_paged_kernel, pps=pps),
        grid_spec=grid_spec,
        out_shape=jax.ShapeDtypeStruct((dbsz, n_sub, 8 * CMP_HID), F32),
        compiler_params=_params(("parallel", "arbitrary")),
        name="compress_paged",
    )(page_table, *([cache_t] * (2 * pps)), w["wbk"], w["wbv"])


def _pages_t(cache):
    n, npos = cache.shape[0], cache.shape[1]
    return cache.transpose(0, 2, 3, 4, 1).reshape(n, 2, KV_HEADS * HEAD_DIM, npos)


def _col_softmax(s):
    m = jnp.max(s, axis=0, keepdims=True)
    p = jnp.exp2(s - m)
    return p * jnp.where(m > 0.5 * NEG, 1.0 / jnp.sum(p, axis=0, keepdims=True), 0.0)


N_FORCED = 3


def _select_blocks(score, forced, jio, axis):
    selneg = jnp.where(forced, 0.0, NEG)
    score = jnp.where(forced, -3e38, score)
    picked = []
    for _ in range(N_SEL_BLOCKS - N_FORCED):
        mx = jnp.max(score, axis=axis, keepdims=True)
        jm = jnp.min(jnp.where(score == mx, jio, 1 << 20), axis=axis, keepdims=True)
        pick = jio == jm
        selneg = jnp.where(pick, 0.0, selneg)
        score = jnp.where(pick, -3e38, score)
        picked.append(jm)
    return selneg, picked


def _nsa_prompt_kernel(q_ref, kc_ref, vct_ref, kaug_ref, vts_ref, kwin_ref, vtw_ref, gt_ref,
                       tc_ref, ts_ref, tw_ref, o_ref,
                       sc_ref, ps_ref, qaug_ref, m_ref, l_ref, acc_ref, sa_ref, sb_ref, *, n_cmp):
    g = pl.program_id(1)
    qb = pl.program_id(2)
    nl = HPG * Q_BLOCK
    q2 = q_ref[...].reshape(nl, LANE)

    pad = 2 * (Q_BLOCK // CMP_STRIDE)
    band = 3 * (Q_BLOCK // CMP_STRIDE)
    sc_ref[0:pad, :] = jnp.zeros((pad, nl), F32)
    sc_ref[pad:pad + n_cmp, :] = _dot_nt(kc_ref[...], q2)
    w0 = pl.multiple_of(qb * (Q_BLOCK // CMP_STRIDE), 8)
    sc_ref[pl.ds(w0, band), :] += tc_ref[...]
    sc = sc_ref[pad:pad + n_cmp, :]
    row = lax.broadcasted_iota(jnp.int32, (n_cmp, nl), 0)
    sc = jnp.where(row < (qb + 1) * (Q_BLOCK // CMP_STRIDE), sc, NEG)
    pc = _col_softmax(sc)
    o_c = _dot(vct_ref[...], pc.astype(BF16))

    n_wt = WINDOW // Q_BLOCK + 1
    tix = [jnp.maximum(qb - (n_wt - 1) + t, 0) for t in range(n_wt)]
    kw = jnp.concatenate([kwin_ref[pl.ds(pl.multiple_of(t * Q_BLOCK, Q_BLOCK), Q_BLOCK), :] for t in tix], axis=0)
    sw = _dot_nt(kw, q2) + tw_ref[...]
    wrow = lax.broadcasted_iota(jnp.int32, (n_wt * Q_BLOCK, nl), 0) + (qb - (n_wt - 1)) * Q_BLOCK
    pw = _col_softmax(jnp.where(wrow >= 0, sw, NEG))
    vw = jnp.concatenate([vtw_ref[t] for t in tix], axis=1)
    o_w = _dot(vw, pw.astype(BF16))

    psum = pc[:, 0:Q_BLOCK]
    for hh in range(1, HPG):
        psum = psum + pc[:, hh * Q_BLOCK:(hh + 1) * Q_BLOCK]
    per_sel = SEL_LEN // CMP_STRIDE
    n_sel = n_cmp // per_sel
    ps_ref[0:8, :] = jnp.zeros((8, Q_BLOCK), F32)
    ps_ref[8:8 + n_cmp, :] = psum
    imp_t = ps_ref[pl.ds(8 + _OVERLAP_OFFSETS[0], n_sel, stride=per_sel), :]
    for off in _OVERLAP_OFFSETS[1:]:
        imp_t = imp_t + ps_ref[pl.ds(8 + off, n_sel, stride=per_sel), :]
    if n_sel < NJ:
        imp_t = jnp.concatenate([imp_t, jnp.zeros((NJ - n_sel, Q_BLOCK), F32)], axis=0)
    jio = lax.broadcasted_iota(jnp.int32, (NJ, Q_BLOCK), 0)
    qi = lax.broadcasted_iota(jnp.int32, (NJ, Q_BLOCK), 1)
    qblk = qb * (Q_BLOCK // SEL_LEN) + qi // SEL_LEN
    forced = (jio == 0) | (jio == qblk) | (jio == qblk - 1)
    score = jnp.where(forced, FORCE, jnp.where(jio <= qblk, imp_t, -FORCE))
    selneg_t, _ = _select_blocks(score, forced, jio, 0)
    selneg = selneg_t.T.astype(BF16)
    for hh in range(HPG):
        qaug_ref[hh * Q_BLOCK:(hh + 1) * Q_BLOCK, 0:NJ] = selneg
    qaug_ref[:, NJ:NJ + LANE] = q2

    m_ref[...] = jnp.full((1, nl), -1e38, F32)
    l_ref[...] = jnp.zeros((1, nl), F32)
    acc_ref[...] = jnp.zeros((HEAD_DIM, nl), F32)

    def scores(u0, ntile):
        rows = ntile * SEL_TILE
        ka = kaug_ref[pl.ds(pl.multiple_of(u0 * SEL_TILE, SEL_TILE), rows), :]
        return _dot_nt(ka, qaug_ref[...])

    def fold(s, u0, ntile):
        m_old = m_ref[...]
        m_new = jnp.maximum(m_old, jnp.max(s, axis=0, keepdims=True))
        alpha = jnp.exp2(m_old - m_new)
        p = jnp.exp2(s - m_new)
        l_ref[...] = alpha * l_ref[...] + jnp.sum(p, axis=0, keepdims=True)
        p = p.astype(BF16)
        pv = _dot(vts_ref[u0], p[0:SEL_TILE])
        for t in range(1, ntile):
            pv = pv + _dot(vts_ref[u0 + t], p[t * SEL_TILE:(t + 1) * SEL_TILE])
        acc_ref[...] = alpha * acc_ref[...] + pv
        m_ref[...] = m_new

    tiles_per_q = SEL_TILE // Q_BLOCK
    u_last = qb // tiles_per_q
    par = qb % tiles_per_q
    n_far = jnp.maximum(u_last - 1, 0)
    n_pair = n_far // 2

    @pl.when(n_pair >= 1)
    def _():
        sa_ref[...] = scores(0, 2)

    def far(i, carry):
        sb_ref[...] = scores(4 * i + 2, 2)
        fold(sa_ref[...], 4 * i, 2)
        sa_ref[...] = scores(2 * jnp.minimum(2 * i + 2, n_pair - 1), 2)
        fold(sb_ref[...], 4 * i + 2, 2)
        return carry

    lax.fori_loop(0, n_pair // 2, far, 0)

    @pl.when(n_pair % 2 == 1)
    def _():
        fold(sa_ref[...], 2 * (n_pair - 1), 2)

    @pl.when(n_far % 2 == 1)
    def _():
        fold(scores(n_far - 1, 1), n_far - 1, 1)

    @pl.when(u_last >= 1)
    def _():
        fold(scores(u_last - 1, 2) + ts_ref[par], u_last - 1, 2)

    @pl.when(u_last == 0)
    def _():
        fold(scores(0, 1) + ts_ref[par, SEL_TILE:2 * SEL_TILE, :], 0, 1)

    o_s = acc_ref[...] / l_ref[...]

    def gate_row(br):
        rows = gt_ref[br * N_HEADS:(br + 1) * N_HEADS, :]
        rows = jnp.where(g == 0, rows[0:HPG], rows[HPG:2 * HPG])
        return jnp.concatenate([rows[hh:hh + 1, :] for hh in range(HPG)], axis=1)

    o_t = gate_row(0) * o_c + gate_row(1) * o_s + gate_row(2) * o_w
    stack = jnp.concatenate([o_t[:, hh * Q_BLOCK:(hh + 1) * Q_BLOCK] for hh in range(HPG)], axis=0)
    o_ref[...] = stack.T.astype(o_ref.dtype)


def _nsa_prompt(qhm, kc, vct, kaug, vts, kwin, vtw, gt, tabs, *, bsz, seq):
    n = bsz * seq
    nqb = seq // Q_BLOCK
    n_cmp = seq // CMP_STRIDE
    nl = HPG * Q_BLOCK
    tc, ts, tw = tabs
    in_specs = [
        pl.BlockSpec((HPG, Q_BLOCK, LANE), lambda b, g, i: (g, b * nqb + i, 0)),
        pl.BlockSpec((None, n_cmp, LANE), lambda b, g, i: (b, 0, 0)),
        pl.BlockSpec((None, None, HEAD_DIM, n_cmp), lambda b, g, i: (b, g, 0, 0)),
        pl.BlockSpec((seq, NJ + LANE), lambda b, g, i: (b, 0)),
        pl.BlockSpec((None, seq // SEL_TILE, HEAD_DIM, SEL_TILE), lambda b, g, i: (g, b, 0, 0)),
        pl.BlockSpec((seq, LANE), lambda b, g, i: (b, 0)),
        pl.BlockSpec((None, seq // LANE, HEAD_DIM, LANE), lambda b, g, i: (g, b, 0, 0)),
        pl.BlockSpec((32, Q_BLOCK), lambda b, g, i: (0, b * nqb + i)),
        pl.BlockSpec((None,) + tc.shape[1:], lambda b, g, i: (g, 0, 0)),
        pl.BlockSpec((None,) + ts.shape[1:], lambda b, g, i: (g, 0, 0, 0)),
        pl.BlockSpec((None,) + tw.shape[1:], lambda b, g, i: (g, 0, 0)),
    ]
    pad = 2 * (Q_BLOCK // CMP_STRIDE)
    return pl.pallas_call(
        functools.partial(_nsa_prompt_kernel, n_cmp=n_cmp),
        grid=(bsz, KV_HEADS, nqb),
        in_specs=in_specs,
        out_specs=pl.BlockSpec((Q_BLOCK, HPG * HEAD_DIM), lambda b, g, i: (b * nqb + i, g)),
        out_shape=jax.ShapeDtypeStruct((n, N_HEADS * HEAD_DIM), BF16),
        scratch_shapes=[pltpu.VMEM((pad + n_cmp + pad, nl), F32), pltpu.VMEM((8 + n_cmp, Q_BLOCK), F32),
                        pltpu.VMEM((nl, NJ + LANE), BF16),
                        pltpu.VMEM((1, nl), F32), pltpu.VMEM((1, nl), F32), pltpu.VMEM((HEAD_DIM, nl), F32),
                        pltpu.VMEM((2 * SEL_TILE, nl), F32), pltpu.VMEM((2 * SEL_TILE, nl), F32)],
        compiler_params=_params(("parallel", "parallel", "arbitrary")),
        name="nsa_prompt",
    )(qhm, kc, vct, kaug, vts, kwin, vtw, gt, tc, ts, tw)


def _row_softmax_parts(s, s_new):
    m = jnp.maximum(jnp.max(s, axis=1, keepdims=True), s_new)
    p = jnp.exp2(s - m)
    pn = jnp.exp2(s_new - m)
    return p, pn, jnp.sum(p, axis=1, keepdims=True) + pn


def _nsa_sample_cmp_kernel(h_ref, pek_ref, w1k_ref, pev_ref, w1v_ref, w2ka_ref, w2kb_ref, w2va_ref, w2vb_ref,
                           q_ref, tcs_ref, ovs_ref, oc_ref, idx_ref, *, n_sub, qblk):
    hh = h_ref[...]
    ak, av = _compress_finish(hh[:, 0:4 * CMP_HID], hh[:, 4 * CMP_HID:8 * CMP_HID],
                              pek_ref, w1k_ref, pev_ref, w1v_ref, w2ka_ref, w2kb_ref, n_sub)
    kc = (_dot(ak[0], w2ka_ref[...]) + _dot(ak[1], w2kb_ref[...])).astype(BF16)
    vc = (_dot(av[0], w2va_ref[...]) + _dot(av[1], w2vb_ref[...])).astype(BF16)
    q8 = q_ref[...].astype(BF16)
    s = _dot_nt(q8, kc) + tcs_ref[...]
    m = jnp.max(s, axis=1, keepdims=True)
    p = jnp.exp2(s - m)
    pc = p / jnp.sum(p, axis=1, keepdims=True)
    oc_ref[...] = _dot(pc.astype(BF16), vc)
    hrow = lax.broadcasted_iota(jnp.int32, (N_HEADS, n_sub), 0)
    rows = []
    for g in range(KV_HEADS):
        rows.append(jnp.sum(jnp.where(hrow // HPG == g, pc, 0.0), axis=0, keepdims=True))
    psum = jnp.concatenate(rows + [jnp.zeros((N_HEADS - KV_HEADS, n_sub), F32)], axis=0)
    imp = _dot_f32(psum, ovs_ref[...])
    jl = lax.broadcasted_iota(jnp.int32, imp.shape, 1)
    forced = (jl == 0) | (jl == qblk) | (jl == qblk - 1)
    score = jnp.where(forced, FORCE, jnp.where(jl <= qblk, imp, -FORCE))
    _, picked = _select_blocks(score, forced, jl, 1)
    lane = lax.broadcasted_iota(jnp.int32, (N_HEADS, LANE), 1)
    out = jnp.zeros((N_HEADS, LANE), jnp.int32)
    for r, jm in enumerate([0, qblk - 1, qblk] + picked):
        out = jnp.where(lane == r, jm, out)
    idx_ref[...] = out


def _nsa_sample_cmp(hbuf, q8, w, tcs, ovs, *, qblk):
    dbsz, n_sub, _ = hbuf.shape
    c2 = lambda b: (0, 0)
    names = ["pek", "w1k", "pev", "w1v", "w2ka", "w2kb", "w2va", "w2vb"]
    return pl.pallas_call(
        functools.partial(_nsa_sample_cmp_kernel, n_sub=n_sub, qblk=qblk),
        grid=(dbsz,),
        in_specs=[pl.BlockSpec((None, n_sub, 8 * CMP_HID), lambda b: (b, 0, 0))]
        + [pl.BlockSpec(w[k].shape, c2) for k in names]
        + [pl.BlockSpec((None, N_HEADS, LANE), lambda b: (b, 0, 0)),
           pl.BlockSpec(tcs.shape, c2), pl.BlockSpec(ovs.shape, c2)],
        out_specs=[pl.BlockSpec((None, N_HEADS, LANE), lambda b: (b, 0, 0)),
                   pl.BlockSpec((None, N_HEADS, LANE), lambda b: (b, 0, 0))],
        out_shape=[jax.ShapeDtypeStruct((dbsz, N_HEADS, LANE), F32),
                   jax.ShapeDtypeStruct((dbsz, N_HEADS, LANE), jnp.int32)],
        compiler_params=_params(("parallel",)),
        name="nsa_sample_cmp",
    )(hbuf, *[w[k] for k in names], q8, tcs, ovs)


def _nsa_sample_sel_kernel(idx_ref, pt_ref, *refs, qblk, n_win):
    nblk = KV_HEADS * N_SEL_BLOCKS
    blocks = refs[:nblk]
    (q_ref, slcn_ref, winn_ref, state_ref, gs_ref, oc_ref, tnear_ref, b0_ref, tws_ref,
     o_ref, wout_ref) = refs[nblk:]
    b = pl.program_id(0)
    q8f = q_ref[...]
    q8 = q8f.astype(BF16)
    hrow = lax.broadcasted_iota(jnp.int32, (N_HEADS, LANE), 0)
    b0 = b0_ref[:, 0:1]

    slcn = slcn_ref[0:1, :]
    s_new = jnp.sum(q8f * slcn[:, 0:LANE], axis=1, keepdims=True) + b0
    halves = PAGE // SEL_LEN
    lane = lax.broadcasted_iota(jnp.int32, (N_HEADS, PAGE), 1)
    o_sel = []
    for g in range(KV_HEADS):
        ss = []
        for r in range(N_SEL_BLOCKS):
            blk = idx_ref[b, g * N_SEL_BLOCKS + r]
            s = _dot(q8, blocks[g * N_SEL_BLOCKS + r][0].astype(BF16))
            near = jnp.where(blk // halves == qblk // halves - 1, tnear_ref[...], 0.0)
            ok = (blk < qblk) & (lane // SEL_LEN == blk % halves)
            ss.append(jnp.where(ok, s + near, NEG))
        s = jnp.concatenate(ss, axis=1)
        p, pn, l = _row_softmax_parts(s, s_new)
        o = pn * slcn[:, LANE:2 * LANE]
        for r in range(N_SEL_BLOCKS):
            pr = p[:, r * PAGE:(r + 1) * PAGE].astype(BF16)
            o = o + _dot_nt(pr, blocks[g * N_SEL_BLOCKS + r][1].astype(BF16))
        o_sel.append(o / l)
    o_s = jnp.where(hrow // HPG == 0, o_sel[0], o_sel[1])

    winn = winn_ref[0:1, :]
    sw = _dot(q8, state_ref[0].astype(BF16)) + tws_ref[...]
    sw_new = jnp.sum(q8f * winn[:, 0:LANE], axis=1, keepdims=True) + b0
    p, pn, l = _row_softmax_parts(sw, sw_new)
    o_w = (_dot_nt(p.astype(BF16), state_ref[1].astype(BF16)) + pn * winn[:, LANE:2 * LANE]) / l

    o_ref[...] = gs_ref[0] * oc_ref[...] + gs_ref[1] * o_s + gs_ref[2] * o_w
    wlane = lax.broadcasted_iota(jnp.int32, (LANE, n_win), 1)
    for kv in range(2):
        newcol = jnp.broadcast_to(winn[:, kv * LANE:(kv + 1) * LANE], (LANE, LANE)).T
        newcol = jnp.concatenate([newcol] * (n_win // LANE), axis=1)
        wout_ref[kv] = jnp.where(wlane == n_win - 1, newcol, pltpu.roll(state_ref[kv], n_win - 1, 1))


def _nsa_sample_sel(idx, page_table, slc_cache_t, q8, slc_new, win_new, state, gs, oc, tnear, b0, tws, *, qblk):
    dbsz, n_pages = page_table.shape
    n_win = state.shape[-1]
    halves = PAGE // SEL_LEN

    def blk_spec(k):
        def imap(b, idx_ref, pt_ref):
            j = idx_ref[b, k]
            return (pt_ref[b, jnp.minimum(j // halves, n_pages - 1)], 0, 0, 0)
        return pl.BlockSpec((None, 2, LANE, PAGE), imap)

    def per_b(shape):
        nd = len(shape)
        return pl.BlockSpec((None,) + tuple(shape[1:]), lambda b, i, p: (b,) + (0,) * (nd - 1))

    def const(shape):
        nd = len(shape)
        return pl.BlockSpec(tuple(shape), lambda b, i, p: (0,) * nd)

    nblk = KV_HEADS * N_SEL_BLOCKS
    grid_spec = pltpu.PrefetchScalarGridSpec(
        num_scalar_prefetch=2, grid=(dbsz,),
        in_specs=[blk_spec(k) for k in range(nblk)]
        + [per_b(q8.shape), per_b(slc_new.shape), per_b(win_new.shape), per_b(state.shape), per_b(gs.shape),
           per_b(oc.shape), const(tnear.shape), const(b0.shape), const(tws.shape)],
        out_specs=[pl.BlockSpec((N_HEADS, LANE), lambda b, i, p: (b, 0)),
                   pl.BlockSpec((None, 2, LANE, n_win), lambda b, i, p: (b, 0, 0, 0))])
    return pl.pallas_call(
        functools.partial(_nsa_sample_sel_kernel, qblk=qblk, n_win=n_win),
        grid_spec=grid_spec,
        out_shape=[jax.ShapeDtypeStruct((dbsz * N_HEADS, LANE), F32),
                   jax.ShapeDtypeStruct((dbsz, 2, LANE, n_win), F32)],
        compiler_params=_params(("arbitrary",)),
        name="nsa_sample_sel",
    )(idx, page_table, *([slc_cache_t] * nblk), q8, slc_new, win_new, state, gs, oc, tnear, b0, tws)


def _merge_kernel(x_ref, o_ref, u_ref, vn_ref, ga_ref, gb_ref, ws_ref, bs_ref, wbn_ref, wbg_ref, wout_ref,
                  lng_ref, x1_ref, *, tm):
    chunk = ws_ref.shape[1]
    keep = (lax.broadcasted_iota(jnp.int32, (chunk, chunk), 0)
            >= lax.broadcasted_iota(jnp.int32, (chunk, chunk), 1))
    vb = vn_ref[...].astype(BF16)
    cols = []
    for gg in range(GM_GROUPS):
        wt = jnp.where(keep, ws_ref[gg], 0.0).astype(BF16)
        rows = [_dot(wt, vb[c * chunk:(c + 1) * chunk, gg * LANE:(gg + 1) * LANE]) for c in range(tm // chunk)]
        cols.append(jnp.concatenate(rows, axis=0) if len(rows) > 1 else rows[0])
    s = jnp.concatenate(cols, axis=1) + jnp.concatenate([bs_ref[...]] * (tm // chunk), axis=0)
    o_gm = (u_ref[...].astype(F32) * s).astype(BF16)
    if len(wbn_ref.shape) == 2:
        a = _dot(o_ref[...].astype(BF16), wbn_ref[...])
    else:
        a = jnp.zeros((tm, D_MODEL), F32)
        for hh in range(N_HEADS):
            a += _dot(o_ref[pl.ds(hh, tm, stride=N_HEADS), :].astype(BF16), wbn_ref[hh])
    bm = _dot(o_gm, wbg_ref[...])
    mixed = (ga_ref[...].astype(F32) * a + gb_ref[...].astype(F32) * bm).astype(BF16)
    x1_ref[...] = x_ref[...] + _rms(_dot(mixed, wout_ref[...]), lng_ref[...])


def _merge(x, o_nsa, u, vn, ga, gb, ws, bs, wbn, w, *, tm, name):
    n = x.shape[0]
    row = lambda i: (i, 0)
    c2 = lambda i: (0, 0)
    c3 = lambda i: (0, 0, 0)
    ws_spec = pl.BlockSpec(ws.shape, c3)
    o_rows = o_nsa.shape[0] // n
    return pl.pallas_call(
        functools.partial(_merge_kernel, tm=tm),
        grid=(n // tm,),
        in_specs=[pl.BlockSpec((tm, D_MODEL), row), pl.BlockSpec((tm * o_rows, o_nsa.shape[1]), row),
                  pl.BlockSpec((tm, GM_DIM), row), pl.BlockSpec((tm, GM_DIM), row),
                  pl.BlockSpec((tm, D_MODEL), row), pl.BlockSpec((tm, D_MODEL), row),
                  ws_spec, pl.BlockSpec(bs.shape, c2),
                  pl.BlockSpec(wbn.shape, c2 if wbn.ndim == 2 else c3), pl.BlockSpec(w["wbg"].shape, c2),
                  pl.BlockSpec(w["wout"].shape, c2), pl.BlockSpec((1, D_MODEL), c2)],
        out_specs=pl.BlockSpec((tm, D_MODEL), row),
        out_shape=jax.ShapeDtypeStruct((n, D_MODEL), F32),
        compiler_params=_params(("parallel",)),
        name=name,
    )(x, o_nsa, u, vn, ga, gb, ws, bs, wbn, w["wbg"], w["wout"], w["ln_mix_post"])


def _ffn_kernel(x_ref, p_ref, lnf_ref, wg_ref, wu_ref, wd_ref, lnp_ref, wple_ref, wpg_ref, lne_ref, y_ref,
                h2_ref, acc_ref):
    j = pl.program_id(1)

    @pl.when(j == 0)
    def _():
        h2_ref[...] = _rms(x_ref[...], lnf_ref[...]).astype(BF16)
        acc_ref[...] = jnp.zeros(acc_ref.shape, F32)

    h2 = h2_ref[...]
    act = (jax.nn.silu(_dot(h2, wg_ref[...])) * _dot(h2, wu_ref[...])).astype(BF16)
    acc_ref[...] += _dot(act, wd_ref[...])

    @pl.when(j == pl.num_programs(1) - 1)
    def _():
        x2 = x_ref[...] + _rms(acc_ref[...], lnp_ref[...])
        e = _dot(p_ref[...].astype(BF16), wple_ref[...])
        gp = jax.nn.sigmoid(_dot(x2.astype(BF16), wpg_ref[...]))
        y_ref[...] = x2 + _rms(gp * e, lne_ref[...])


def _ffn(x1, ple, w, *, tm, name):
    n = x1.shape[0]
    row = lambda i, j: (i, 0)
    c2 = lambda i, j: (0, 0)
    return pl.pallas_call(
        _ffn_kernel,
        grid=(n // tm, D_FF // FF_CHUNK),
        in_specs=[pl.BlockSpec((tm, D_MODEL), row), pl.BlockSpec((tm, PLE_DIM), row),
                  pl.BlockSpec((1, D_MODEL), c2),
                  pl.BlockSpec((D_MODEL, FF_CHUNK), lambda i, j: (0, j)),
                  pl.BlockSpec((D_MODEL, FF_CHUNK), lambda i, j: (0, j)),
                  pl.BlockSpec((FF_CHUNK, D_MODEL), lambda i, j: (j, 0)),
                  pl.BlockSpec((1, D_MODEL), c2), pl.BlockSpec(w["wple"].shape, c2),
                  pl.BlockSpec(w["wpg"].shape, c2), pl.BlockSpec((1, D_MODEL), c2)],
        out_specs=pl.BlockSpec((tm, D_MODEL), row),
        out_shape=jax.ShapeDtypeStruct((n, D_MODEL), F32),
        scratch_shapes=[pltpu.VMEM((tm, D_MODEL), BF16), pltpu.VMEM((tm, D_MODEL), F32)],
        compiler_params=_params(("parallel", "arbitrary")),
        name=name,
    )(x1, ple, w["ln_ffn_pre"], w["wfg"], w["wfu"], w["wfd"], w["ln_ffn_post"], w["wple"], w["wpg"],
      w["ln_ple_post"])


def _prep_weights(i, ln_mix_pre, w_in, cmp_pe_k, cmp_w1_k, cmp_w2_k, cmp_pe_v, cmp_w1_v, cmp_w2_v, gm_ln_g,
                  gm_ln_b, w_branch_nsa, w_branch_gm, w_out, ln_mix_post, ln_ffn_pre, w_ffn_gate, w_ffn_up,
                  w_ffn_down, ln_ffn_post, w_ple, w_ple_gate, ln_ple_post):
    q_dim = N_HEADS * HEAD_DIM
    kv3 = 6 * KV_HEADS * HEAD_DIM
    n_gate = 3 * N_HEADS
    win = w_in[i]
    o = 0
    wq = win[:, o:o + q_dim]; o += q_dim
    wkv = win[:, o:o + kv3]; o += kv3
    wg = win[:, o:o + n_gate]; o += n_gate
    wuv = win[:, o:o + 2 * GM_DIM]; o += 2 * GM_DIM
    wmg = win[:, o:o + 2 * D_MODEL]
    wq4 = wq.reshape(D_MODEL, N_HEADS, 1, HEAD_DIM)
    half = (np.arange(N_HEADS) // HPG)[None, :, None, None] == np.arange(KV_HEADS)[None, None, :, None]
    wq_pad = jnp.where(half, wq4, 0.0).reshape(D_MODEL, N_HEADS * LANE)
    wg_pad = jnp.pad(wg, ((0, 0), (0, LANE - n_gate)))

    def blockdiag(w1):
        w1s = w1.reshape(CMP_LEN // CMP_STRIDE, CMP_STRIDE, HEAD_DIM, CMP_HID).transpose(1, 2, 0, 3)
        w1s = w1s.reshape(CMP_STRIDE, HEAD_DIM, 2 * CMP_HID)
        z = jnp.zeros_like(w1s)
        bd = jnp.concatenate([jnp.concatenate([w1s, z], axis=2), jnp.concatenate([z, w1s], axis=2)], axis=1)
        return bd.reshape(CMP_STRIDE // 2, 2 * 2 * HEAD_DIM, 4 * CMP_HID)

    wbn4 = w_branch_nsa[i].reshape(N_HEADS, 1, HEAD_DIM, D_MODEL)
    wbn_heads = jnp.where(half.reshape(N_HEADS, KV_HEADS, 1, 1), wbn4, 0.0).reshape(N_HEADS, LANE, D_MODEL)

    zk = jnp.zeros_like(cmp_w2_k[i])
    zv = jnp.zeros_like(cmp_w2_v[i])
    b = lambda a: a.astype(BF16)
    r = lambda a: a[i][None, :].astype(F32)
    return dict(
        ln_mix_pre=r(ln_mix_pre), wq=b(wq_pad), wkv=b(wkv), wg=b(wg_pad), wuv=b(wuv), wmg=b(wmg),
        gm_ln_g=r(gm_ln_g), gm_ln_b=r(gm_ln_b),
        wbk=b(blockdiag(cmp_w1_k[i])), wbv=b(blockdiag(cmp_w1_v[i])),
        pek=jnp.broadcast_to(cmp_pe_k[i].reshape(1, -1), (8, CMP_LEN * HEAD_DIM)).astype(F32), w1k=b(cmp_w1_k[i]),
        pev=jnp.broadcast_to(cmp_pe_v[i].reshape(1, -1), (8, CMP_LEN * HEAD_DIM)).astype(F32), w1v=b(cmp_w1_v[i]),
        w2ka=b(jnp.concatenate([cmp_w2_k[i], zk], axis=1)), w2kb=b(jnp.concatenate([zk, cmp_w2_k[i]], axis=1)),
        w2va=b(jnp.concatenate([cmp_w2_v[i], zv], axis=1)), w2vb=b(jnp.concatenate([zv, cmp_w2_v[i]], axis=1)),
        w2vt=b(cmp_w2_v[i].T),
        wbn=b(w_branch_nsa[i]), wbn_heads=b(wbn_heads), wbg=b(w_branch_gm[i]), wout=b(w_out[i]), ln_mix_post=r(ln_mix_post),
        ln_ffn_pre=r(ln_ffn_pre), wfg=b(w_ffn_gate[i]), wfu=b(w_ffn_up[i]), wfd=b(w_ffn_down[i]),
        ln_ffn_post=r(ln_ffn_post), wple=b(w_ple[i]), wpg=b(w_ple_gate[i]), ln_ple_post=r(ln_ple_post),
    )


def _gate_perm():
    return np.array([hg * 3 + br for br in range(3) for hg in range(N_HEADS)])


def _prompt_layer(x, ple, w, rel_bias, gm_ws, gm_bs):
    bsz, seq, _ = x.shape
    assert seq % SEL_TILE == 0 and seq // SEL_LEN <= NJ and seq >= WINDOW
    n = bsz * seq
    tm = 256
    xf = x.reshape(n, D_MODEL)
    wp = dict(w)
    perm = _gate_perm()
    wp["wg"] = jnp.concatenate([w["wg"][:, perm], w["wg"][:, len(perm):]], axis=1)
    (qhm, cmp, cmpt, slct, wint, kaug, vts, kwin, vtw, gt, u, vn, ga, gb) = _inproj(
        xf, wp, prompt=True, seq=seq, tm=tm)
    kc, vct = _compress_prompt(cmp, w, bsz=bsz, seq=seq)

    o_nsa = _nsa_prompt(qhm, kc, vct, kaug, vts, kwin, vtw, gt, _bias_tables(rel_bias), bsz=bsz, seq=seq)

    bs_tile = jnp.repeat(gm_bs.T, LANE, axis=1).astype(F32)
    x1 = _merge(xf, o_nsa, u, vn, ga, gb, gm_ws.astype(F32), bs_tile, w["wbn"], w, tm=512, name="merge_prompt")
    y = _ffn(x1, ple.reshape(n, PLE_DIM), w, tm=512, name="ffn_prompt")

    def rows_form(at):
        return at.reshape(bsz, 2, KV_HEADS, HEAD_DIM, at.shape[-1]).transpose(0, 4, 1, 2, 3)

    return (y.reshape(bsz, seq, D_MODEL), rows_form(cmpt), rows_form(slct),
            rows_form(wint[:, :, seq - min(WINDOW, seq):]))


def _sample_layer(x, ple, w, rel_bias, gm_ws, gm_bs, cache_cmp, cache_slc, win_buf, page_table):
    dbsz, nq, _ = x.shape
    assert nq == 1
    n_pool = cache_cmp.shape[0]
    n_pages = page_table.shape[1]
    past = n_pages * PAGE
    n_win = win_buf.shape[1]
    assert past % GM_CHUNK == 0 and n_win == WINDOW and past >= WINDOW and dbsz % 8 == 0
    xf = x.reshape(dbsz, D_MODEL)
    q, cmp, slc, win, gate, u, vn, ga, gb = _inproj(xf, w, prompt=False, seq=1, tm=dbsz)

    pps = min(16, n_pages)
    hbuf = _compress_paged(_pages_t(cache_cmp), page_table, w, pps=pps)
    n_sub = past // CMP_STRIDE
    qblk = past // SEL_LEN
    n_sel_pad = -(-(qblk + 1) // LANE) * LANE
    sh = _shifted_bias(rel_bias)
    i = np.arange(n_sub)
    d_c = past - (CMP_STRIDE * i + CMP_LEN - 1)
    tcs = _sample_table(sh, d_c, i < n_sub - 1)
    ovs = jnp.asarray(_overlap_t(n_sub, n_sel_pad).T)
    q8 = q.reshape(dbsz, N_HEADS, LANE)
    oc, idx = _nsa_sample_cmp(hbuf, q8, w, tcs, ovs, qblk=qblk)
    idx2 = idx[:, 0:KV_HEADS, 0:N_SEL_BLOCKS].reshape(dbsz, KV_HEADS * N_SEL_BLOCKS)

    l = np.arange(PAGE)
    tnear = _sample_table(sh, PAGE - l, l >= 0)
    b0 = jnp.broadcast_to(sh[0][:, None] * LOG2E, (N_HEADS, LANE)).astype(F32)
    kpos = np.arange(n_win)
    tws = _sample_table(sh, n_win - kpos, kpos >= 0)
    gs = gate[:, 0:3 * N_HEADS].reshape(dbsz, N_HEADS, 3).transpose(0, 2, 1)
    gs = jnp.broadcast_to(gs[..., None], (dbsz, 3, N_HEADS, LANE))
    o8, win_out = _nsa_sample_sel(
        idx2, page_table, _pages_t(cache_slc), q8,
        jnp.broadcast_to(slc[:, None, :], (dbsz, 8, 256)), jnp.broadcast_to(win[:, None, :], (dbsz, 8, 256)),
        _pages_t(win_buf), gs, oc, tnear, b0, tws, qblk=qblk)
    win_out = win_out.reshape(dbsz, 2, KV_HEADS, HEAD_DIM, n_win).transpose(0, 4, 1, 2, 3)

    ws_diag = gm_ws[:, 0, 0][:, None, None] * jnp.eye(dbsz, dtype=F32)[None]
    bs_tile = jnp.broadcast_to(jnp.repeat(gm_bs[:, 0], LANE)[None, :], (dbsz, GM_GROUPS * LANE)).astype(F32)
    x1 = _merge(xf, o8, u, vn, ga, gb, ws_diag, bs_tile, w["wbn_heads"], w, tm=dbsz, name="merge_sample")
    y = _ffn(x1, ple.reshape(dbsz, PLE_DIM), w, tm=dbsz, name="ffn_sample")
    kvshape = (dbsz, 1, 2, KV_HEADS, HEAD_DIM)
    return (y.reshape(dbsz, 1, D_MODEL), cmp.reshape(kvshape), slc.reshape(kvshape),
            win_out, vn.reshape(dbsz, 1, GM_DIM))


def kernel(x_prompt, x_sample, cache_cmp_kv, cache_slc_kv, state_win_kv, page_table, p_prompt, p_sample, rel_bias,
           ln_mix_pre, w_in, cmp_pe_k, cmp_w1_k, cmp_w2_k, cmp_pe_v, cmp_w1_v, cmp_w2_v, gm_ln_g, gm_ln_b, gm_ws,
           gm_bs, w_branch_nsa, w_branch_gm, w_out, ln_mix_post, ln_ffn_pre, w_ffn_gate, w_ffn_up, w_ffn_down,
           ln_ffn_post, w_ple, w_ple_gate, ln_ple_post):
    depth = w_in.shape[0]
    xp, xs = x_prompt, x_sample
    outs = [[] for _ in range(7)]
    for i in range(depth):
        w = _prep_weights(i, ln_mix_pre, w_in, cmp_pe_k, cmp_w1_k, cmp_w2_k, cmp_pe_v, cmp_w1_v, cmp_w2_v, gm_ln_g,
                          gm_ln_b, w_branch_nsa, w_branch_gm, w_out, ln_mix_post, ln_ffn_pre, w_ffn_gate, w_ffn_up,
                          w_ffn_down, ln_ffn_post, w_ple, w_ple_gate, ln_ple_post)
        xp, c_p, s_p, w_p = _prompt_layer(xp, p_prompt[i], w, rel_bias, gm_ws[i], gm_bs[i])
        xs, c_s, s_s, w_s, v_s = _sample_layer(xs, p_sample[i], w, rel_bias, gm_ws[i], gm_bs[i], cache_cmp_kv[i],
                                               cache_slc_kv[i], state_win_kv[i], page_table)
        for lst, val in zip(outs, (c_p, s_p, w_p, c_s, s_s, w_s, v_s)):
            lst.append(val)
    return (xp, xs) + tuple(jnp.stack(o) for o in outs)
```

```python
import functools
import math

import jax
import jax.numpy as jnp
import numpy as np
from jax import lax
from jax.experimental import pallas as pl
from jax.experimental.pallas import tpu as pltpu

F32 = jnp.float32
BF16 = jnp.bfloat16

D_MODEL = 1024
N_HEADS = 8
KV_HEADS = 2
HPG = N_HEADS // KV_HEADS
HEAD_DIM = 64
CMP_LEN = 32
CMP_STRIDE = 16
CMP_HID = 2 * HEAD_DIM
SEL_LEN = 64
N_SEL_BLOCKS = 16
WINDOW = 512
Q_BLOCK = 128
GM_GROUPS = 4
GM_CHUNK = 128
GM_DIM = D_MODEL // 2
D_FF = -(-8 * D_MODEL // (3 * 256)) * 256
PLE_DIM = 256
N_BUCKETS = 32
MAX_DISTANCE = 128
EPS = 1e-6
NEG = -1e30
LOG2E = 1.4426950408889634
FORCE = 1e6
PAGE = 128

LANE = 128
NJ = 128
SEL_TILE = 256
FF_CHUNK = D_FF // 2
VMEM_LIMIT = 56 * 1024 * 1024


def _dot(a, b):
    return jnp.dot(a, b, preferred_element_type=F32)


def _dot_nt(a, b):
    return lax.dot_general(a, b, (((1,), (1,)), ((), ())), preferred_element_type=F32)


def _dot_f32(a, b):
    return jnp.dot(a, b, preferred_element_type=F32, precision=lax.Precision.HIGHEST)


def _rms(x, g):
    return x * lax.rsqrt(jnp.mean(x * x, axis=-1, keepdims=True) + EPS) * g


def _params(sem):
    return pltpu.CompilerParams(dimension_semantics=sem, vmem_limit_bytes=VMEM_LIMIT)


def _t5_bucket_np(d):
    d = np.maximum(d, 0)
    max_exact = N_BUCKETS // 2
    ratio = (np.log(np.maximum(d, 1).astype(np.float32) / np.float32(max_exact))
             / np.float32(math.log(MAX_DISTANCE / max_exact)))
    large = np.minimum(max_exact + (ratio * np.float32(N_BUCKETS - max_exact)).astype(np.int32), N_BUCKETS - 1)
    return np.where(d < max_exact, d, large)


_BUCKET = _t5_bucket_np(np.arange(MAX_DISTANCE + 1))
assert _BUCKET[MAX_DISTANCE] == N_BUCKETS - 1


def _shifted_bias(rel_bias):
    return rel_bias[_BUCKET] - rel_bias[N_BUCKETS - 1][None, :]


def _sample_table(sh, dist, valid):
    t = sh[np.minimum(np.maximum(dist, 0), MAX_DISTANCE)] * LOG2E
    return jnp.where(valid[:, None], t, NEG).T.astype(F32)


_BUCKET_START = [int(np.argmax(_BUCKET >= k)) for k in range(N_BUCKETS)]
_BAND_ROWS = 3 * (Q_BLOCK // CMP_STRIDE)


def _bias_tables_kernel(rb_ref, tc_ref, ts_ref, tw_ref):
    def fill(store, rows, chunk, c0, stride, dmax):
        def body(c, carry):
            r0 = pl.multiple_of(c * chunk, chunk)
            r = r0 + lax.broadcasted_iota(jnp.int32, (chunk, Q_BLOCK), 0)
            d = c0 + lax.broadcasted_iota(jnp.int32, (chunk, Q_BLOCK), 1) - stride * r
            ok = (d >= 0) & (d <= dmax)
            dc = jnp.minimum(d, MAX_DISTANCE)
            for hd in range(N_HEADS):
                v = jnp.full((chunk, Q_BLOCK), rb_ref[0, hd], F32)
                for k in range(1, N_BUCKETS):
                    v = jnp.where(dc >= _BUCKET_START[k], rb_ref[k, hd], v)
                v = (v - rb_ref[N_BUCKETS - 1, hd]) * LOG2E
                store(r0, chunk, hd, jnp.where(ok, v, NEG))
            return carry
        lax.fori_loop(0, rows // chunk, body, 0)

    def lanes(hd):
        return slice(hd * Q_BLOCK, (hd + 1) * Q_BLOCK)

    def st_c(r0, n, hd, v):
        tc_ref[pl.ds(r0, n), lanes(hd)] = v

    def st_w(r0, n, hd, v):
        tw_ref[pl.ds(r0, n), lanes(hd)] = v

    big = 1 << 30
    fill(st_c, _BAND_ROWS, 8, 2 * Q_BLOCK - (CMP_LEN - 1), CMP_STRIDE, big)
    for par in range(SEL_TILE // Q_BLOCK):
        def st_s(r0, n, hd, v, par=par):
            ts_ref[par, pl.ds(r0, n), lanes(hd)] = v
        fill(st_s, 2 * SEL_TILE, 64, SEL_TILE + Q_BLOCK * par, 1, big)
    fill(st_w, WINDOW + Q_BLOCK, 64, WINDOW, 1, WINDOW)


def _bias_tables(rel_bias):
    nl = N_HEADS * Q_BLOCK
    return pl.pallas_call(
        _bias_tables_kernel,
        in_specs=[pl.BlockSpec(memory_space=pltpu.SMEM)],
        out_shape=[jax.ShapeDtypeStruct((_BAND_ROWS, nl), F32),
                   jax.ShapeDtypeStruct((SEL_TILE // Q_BLOCK, 2 * SEL_TILE, nl), F32),
                   jax.ShapeDtypeStruct((WINDOW + Q_BLOCK, nl), F32)],
        compiler_params=pltpu.CompilerParams(vmem_limit_bytes=VMEM_LIMIT),
        name="bias_tables",
    )(rel_bias.astype(F32))


def _overlap_t(n_cmp, n_sel_pad):
    ci = np.arange(n_cmp)[None, :] * CMP_STRIDE
    sj = np.arange(n_sel_pad)[:, None] * SEL_LEN
    return ((ci <= sj + SEL_LEN - 1) & (ci + CMP_LEN - 1 >= sj)).astype(np.float32)


def _overlap_offsets():
    per = SEL_LEN // CMP_STRIDE
    ov = _overlap_t(16 * per, 16)
    jj, ii = np.nonzero(ov)
    offs = sorted({int(i - per * j) for j, i in zip(jj, ii)})
    band = np.isin(np.arange(16 * per)[None, :] - per * np.arange(16)[:, None], offs)
    assert (band == (ov > 0)).all() and min(offs) >= -8
    return offs


_OVERLAP_OFFSETS = _overlap_offsets()


def _inproj_kernel(x_ref, lng_ref, wq_ref, wkv_ref, wg_ref, wuv_ref, wmg_ref, glg_ref, glb_ref, *outs,
                   prompt, seq, tm):
    h = _rms(x_ref[...], lng_ref[...]).astype(BF16)
    q = _dot(h, wq_ref[...]) * (HEAD_DIM ** -0.5 * LOG2E)
    kv = _dot(h, wkv_ref[...])
    gate = jax.nn.sigmoid(_dot(h, wg_ref[...]))
    uv = jax.nn.gelu(_dot(h, wuv_ref[...]))
    u = uv[:, :GM_DIM]
    v = uv[:, GM_DIM:]
    mu = jnp.mean(v, axis=-1, keepdims=True)
    var = jnp.mean(jnp.square(v - mu), axis=-1, keepdims=True)
    vn = (v - mu) * lax.rsqrt(var + EPS) * glg_ref[...] + glb_ref[...]
    mg = jax.nn.sigmoid(_dot(h, wmg_ref[...]))
    cmp, slc, win = kv[:, 0:256], kv[:, 256:512], kv[:, 512:768]
    if prompt:
        (q_ref, cmp_ref, cmpt_ref, slct_ref, wint_ref, kaug_ref, vts_ref, kwin_ref, vtw_ref, gt_ref,
         u_ref, vn_ref, ga_ref, gb_ref) = outs
        for hh in range(N_HEADS):
            q_ref[hh] = q[:, hh * LANE:(hh + 1) * LANE].astype(BF16)
        pos = (pl.program_id(0) * tm + lax.broadcasted_iota(jnp.int32, (tm, NJ), 0)) % seq
        onehot = (lax.broadcasted_iota(jnp.int32, (tm, NJ), 1) == pos // SEL_LEN)
        kaug_ref[:, 0:NJ] = onehot.astype(BF16)
        kaug_ref[:, NJ:NJ + LANE] = slc[:, 0:LANE].astype(BF16)
        kwin_ref[...] = win[:, 0:LANE].astype(BF16)
        slc_t = slc.T
        win_t = win.T
        for g in range(KV_HEADS):
            r0 = LANE + HEAD_DIM * g
            for c in range(tm // SEL_TILE):
                vts_ref[g, c] = slc_t[r0:r0 + HEAD_DIM, c * SEL_TILE:(c + 1) * SEL_TILE].astype(BF16)
            for c in range(tm // LANE):
                vtw_ref[g, c] = win_t[r0:r0 + HEAD_DIM, c * LANE:(c + 1) * LANE].astype(BF16)
        gt_ref[...] = gate.T[0:32, :]
        cmpt_ref[...] = cmp.T
        slct_ref[...] = slc_t
        wint_ref[...] = win_t
    else:
        q_ref, cmp_ref, slc_ref, win_ref, gate_ref, u_ref, vn_ref, ga_ref, gb_ref = outs
        q_ref[...] = q
        gate_ref[...] = gate
        slc_ref[...] = slc
        win_ref[...] = win
    cmp_ref[...] = cmp
    u_ref[...] = u.astype(u_ref.dtype)
    vn_ref[...] = vn.astype(vn_ref.dtype)
    ga_ref[...] = mg[:, :D_MODEL].astype(ga_ref.dtype)
    gb_ref[...] = mg[:, D_MODEL:].astype(gb_ref.dtype)


def _inproj(x, w, *, prompt, seq, tm):
    n = x.shape[0]
    row = lambda i: (i, 0)
    const = lambda i: (0, 0)
    in_specs = [pl.BlockSpec((tm, D_MODEL), row), pl.BlockSpec((1, D_MODEL), const),
                pl.BlockSpec(w["wq"].shape, const), pl.BlockSpec(w["wkv"].shape, const),
                pl.BlockSpec(w["wg"].shape, const), pl.BlockSpec(w["wuv"].shape, const),
                pl.BlockSpec(w["wmg"].shape, const), pl.BlockSpec((1, GM_DIM), const),
                pl.BlockSpec((1, GM_DIM), const)]
    kv_shapes = [jax.ShapeDtypeStruct((n, 256), F32)] * 3
    kv_specs = [pl.BlockSpec((tm, 256), row)] * 3
    if prompt:
        act = BF16
        tiles = seq // tm
        kvt_shapes = [jax.ShapeDtypeStruct((n // seq, 256, seq), F32)] * 3
        kvt_specs = [pl.BlockSpec((None, 256, tm), lambda i: (i // tiles, 0, i % tiles))] * 3
        out_shape = ([jax.ShapeDtypeStruct((N_HEADS, n, LANE), BF16)] + kv_shapes[:1] + kvt_shapes + [
            jax.ShapeDtypeStruct((n, NJ + LANE), BF16),
            jax.ShapeDtypeStruct((KV_HEADS, n // SEL_TILE, HEAD_DIM, SEL_TILE), BF16),
            jax.ShapeDtypeStruct((n, LANE), BF16),
            jax.ShapeDtypeStruct((KV_HEADS, n // LANE, HEAD_DIM, LANE), BF16),
            jax.ShapeDtypeStruct((32, n), F32)])
        out_specs = ([pl.BlockSpec((N_HEADS, tm, LANE), lambda i: (0, i, 0))] + kv_specs[:1] + kvt_specs + [
            pl.BlockSpec((tm, NJ + LANE), row),
            pl.BlockSpec((KV_HEADS, tm // SEL_TILE, HEAD_DIM, SEL_TILE), lambda i: (0, i, 0, 0)),
            pl.BlockSpec((tm, LANE), row),
            pl.BlockSpec((KV_HEADS, tm // LANE, HEAD_DIM, LANE), lambda i: (0, i, 0, 0)),
            pl.BlockSpec((32, tm), lambda i: (0, i))])
    else:
        act = F32
        out_shape = ([jax.ShapeDtypeStruct((n, N_HEADS * LANE), F32)] + kv_shapes
                     + [jax.ShapeDtypeStruct((n, LANE), F32)])
        out_specs = [pl.BlockSpec((tm, N_HEADS * LANE), row)] + kv_specs + [pl.BlockSpec((tm, LANE), row)]
    out_shape += [jax.ShapeDtypeStruct((n, GM_DIM), act)] * 2 + [jax.ShapeDtypeStruct((n, D_MODEL), act)] * 2
    out_specs += [pl.BlockSpec((tm, GM_DIM), row)] * 2 + [pl.BlockSpec((tm, D_MODEL), row)] * 2
    return pl.pallas_call(
        functools.partial(_inproj_kernel, prompt=prompt, seq=seq, tm=tm),
        grid=(n // tm,), in_specs=in_specs, out_specs=out_specs, out_shape=out_shape,
        compiler_params=_params(("parallel",)),
        name="inproj_prompt" if prompt else "inproj_sample",
    )(x, w["ln_mix_pre"], w["wq"], w["wkv"], w["wg"], w["wuv"], w["wmg"], w["gm_ln_g"], w["gm_ln_b"])


def _row_pair(x_ref, r, n_sub):
    return jnp.concatenate([x_ref[pl.ds(r, n_sub, stride=CMP_STRIDE), :],
                            x_ref[pl.ds(r + 1, n_sub, stride=CMP_STRIDE), :]], axis=1).astype(BF16)


def _compress_finish(hk, hv, pek_ref, w1k_ref, pev_ref, w1v_ref, w2ka_ref, w2kb_ref, n_sub):
    pwk = _dot(pek_ref[...].astype(BF16), w1k_ref[...])[0:1, :]
    pwv = _dot(pev_ref[...].astype(BF16), w1v_ref[...])[0:1, :]

    def act(hh, g, pw):
        a0 = hh[:, 256 * g:256 * g + CMP_HID]
        a1 = hh[:, 256 * g + CMP_HID:256 * g + 2 * CMP_HID]
        return jax.nn.silu(a0 + pltpu.roll(a1, n_sub - 1, 0) + pw).astype(BF16)

    return [act(hk, 0, pwk), act(hk, 1, pwk)], [act(hv, 0, pwv), act(hv, 1, pwv)]


def _compress_prompt_kernel(cmpk_ref, cmpv_ref, wbk_ref, wbv_ref, pek_ref, w1k_ref, pev_ref, w1v_ref,
                            w2ka_ref, w2kb_ref, w2vt_ref, kc_ref, vct_ref, *, n_sub):
    hk = jnp.zeros((n_sub, 4 * CMP_HID), F32)
    hv = jnp.zeros((n_sub, 4 * CMP_HID), F32)
    for r in range(0, CMP_STRIDE, 2):
        hk += _dot(_row_pair(cmpk_ref, r, n_sub), wbk_ref[r // 2])
        hv += _dot(_row_pair(cmpv_ref, r, n_sub), wbv_ref[r // 2])
    ak, av = _compress_finish(hk, hv, pek_ref, w1k_ref, pev_ref, w1v_ref, w2ka_ref, w2kb_ref, n_sub)
    kc_ref[...] = (_dot(ak[0], w2ka_ref[...]) + _dot(ak[1], w2kb_ref[...])).astype(BF16)
    for g in range(KV_HEADS):
        vct_ref[g] = _dot_nt(w2vt_ref[...], av[g]).astype(BF16)


def _compress_prompt(cmp, w, *, bsz, seq):
    n_sub = seq // CMP_STRIDE
    const2 = lambda b: (0, 0)
    const3 = lambda b: (0, 0, 0)
    return pl.pallas_call(
        functools.partial(_compress_prompt_kernel, n_sub=n_sub),
        grid=(bsz,),
        in_specs=[pl.BlockSpec((seq, LANE), lambda b: (b, 0)), pl.BlockSpec((seq, LANE), lambda b: (b, 1)),
                  pl.BlockSpec(w["wbk"].shape, const3), pl.BlockSpec(w["wbv"].shape, const3),
                  pl.BlockSpec(w["pek"].shape, const2), pl.BlockSpec(w["w1k"].shape, const2),
                  pl.BlockSpec(w["pev"].shape, const2), pl.BlockSpec(w["w1v"].shape, const2),
                  pl.BlockSpec(w["w2ka"].shape, const2), pl.BlockSpec(w["w2kb"].shape, const2),
                  pl.BlockSpec(w["w2vt"].shape, const2)],
        out_specs=[pl.BlockSpec((None, n_sub, LANE), lambda b: (b, 0, 0)),
                   pl.BlockSpec((None, KV_HEADS, HEAD_DIM, n_sub), lambda b: (b, 0, 0, 0))],
        out_shape=[jax.ShapeDtypeStruct((bsz, n_sub, LANE), BF16),
                   jax.ShapeDtypeStruct((bsz, KV_HEADS, HEAD_DIM, n_sub), BF16)],
        compiler_params=_params(("parallel",)),
        name="compress_prompt",
    )(cmp, cmp, w["wbk"], w["wbv"], w["pek"], w["w1k"], w["pev"], w["w1v"], w["w2ka"], w["w2kb"], w["w2vt"])


def _compress_paged_kernel(pt_ref, *refs, pps):
    kpages, vpages = refs[:pps], refs[pps:2 * pps]
    wbk_ref, wbv_ref, h_ref, xk_ref, xv_ref = refs[2 * pps:]
    sub_per_page = PAGE // CMP_STRIDE
    n = pps * sub_per_page
    for k in range(pps):
        xk_ref[k * PAGE:(k + 1) * PAGE, :] = kpages[k][...].T
        xv_ref[k * PAGE:(k + 1) * PAGE, :] = vpages[k][...].T
    hk = jnp.zeros((n, 4 * CMP_HID), F32)
    hv = jnp.zeros((n, 4 * CMP_HID), F32)
    for r in range(0, CMP_STRIDE, 2):
        hk += _dot(_row_pair(xk_ref, r, n), wbk_ref[r // 2])
        hv += _dot(_row_pair(xv_ref, r, n), wbv_ref[r // 2])
    h_ref[:, 0:4 * CMP_HID] = hk
    h_ref[:, 4 * CMP_HID:8 * CMP_HID] = hv


def _compress_paged(cache_t, page_table, w, *, pps):
    dbsz, n_pages = page_table.shape
    sub_per_page = PAGE // CMP_STRIDE
    n_sub = n_pages * sub_per_page

    def page_spec(k, c):
        return pl.BlockSpec((None, None, LANE, PAGE), lambda b, t, pt: (pt[b, t * pps + k], c, 0, 0))

    const3 = lambda b, t, pt: (0, 0, 0)
    grid_spec = pltpu.PrefetchScalarGridSpec(
        num_scalar_prefetch=1, grid=(dbsz, n_pages // pps),
        in_specs=[page_spec(k, c) for c in range(2) for k in range(pps)]
        + [pl.BlockSpec(w["wbk"].shape, const3), pl.BlockSpec(w["wbv"].shape, const3)],
        out_specs=pl.BlockSpec((None, pps * sub_per_page, 8 * CMP_HID), lambda b, t, pt: (b, t, 0)),
        scratch_shapes=[pltpu.VMEM((pps * PAGE, LANE), F32), pltpu.VMEM((pps * PAGE, LANE), F32)])
    return pl.pallas_call(
        functools.partial(_compress_paged_kernel, pps=pps),
        grid_spec=grid_spec,
        out_shape=jax.ShapeDtypeStruct((dbsz, n_sub, 8 * CMP_HID), F32),
        compiler_params=_params(("parallel", "arbitrary")),
        name="compress_paged",
    )(page_table, *([cache_t] * (2 * pps)), w["wbk"], w["wbv"])


def _pages_t(cache):
    n, npos = cache.shape[0], cache.shape[1]
    return cache.transpose(0, 2, 3, 4, 1).reshape(n, 2, KV_HEADS * HEAD_DIM, npos)


def _col_softmax(s):
    m = jnp.max(s, axis=0, keepdims=True)
    p = jnp.exp2(s - m)
    return p * jnp.where(m > 0.5 * NEG, 1.0 / jnp.sum(p, axis=0, keepdims=True), 0.0)


N_FORCED = 3


def _select_blocks(score, forced, jio, axis):
    selneg = jnp.where(forced, 0.0, NEG)
    score = jnp.where(forced, -3e38, score)
    picked = []
    for _ in range(N_SEL_BLOCKS - N_FORCED):
        mx = jnp.max(score, axis=axis, keepdims=True)
        jm = jnp.min(jnp.where(score == mx, jio, 1 << 20), axis=axis, keepdims=True)
        pick = jio == jm
        selneg = jnp.where(pick, 0.0, selneg)
        score = jnp.where(pick, -3e38, score)
        picked.append(jm)
    return selneg, picked


def _nsa_prompt_kernel(q_ref, kc_ref, vct_ref, kaug_ref, vts_ref, kwin_ref, vtw_ref, gt_ref,
                       tc_ref, ts_ref, tw_ref, o_ref,
                       sc_ref, ps_ref, qaug_ref, m_ref, l_ref, acc_ref, sa_ref, sb_ref, *, n_cmp):
    qb = pl.program_id(1)
    gl = HPG * Q_BLOCK
    nl = KV_HEADS * gl
    q2 = q_ref[...].reshape(nl, LANE)

    def group_dots(vt, p):
        return jnp.concatenate([_dot(vt(g), p[:, g * gl:(g + 1) * gl]) for g in range(KV_HEADS)], axis=1)

    pad = 2 * (Q_BLOCK // CMP_STRIDE)
    band = 3 * (Q_BLOCK // CMP_STRIDE)
    sc_ref[0:pad, :] = jnp.zeros((pad, nl), F32)
    sc_ref[pad:pad + n_cmp, :] = _dot_nt(kc_ref[...], q2)
    w0 = pl.multiple_of(qb * (Q_BLOCK // CMP_STRIDE), 8)
    sc_ref[pl.ds(w0, band), :] += tc_ref[...]
    sc = sc_ref[pad:pad + n_cmp, :]
    row = lax.broadcasted_iota(jnp.int32, (n_cmp, nl), 0)
    sc = jnp.where(row < (qb + 1) * (Q_BLOCK // CMP_STRIDE), sc, NEG)
    pc = _col_softmax(sc)
    o_c = group_dots(lambda g: vct_ref[g], pc.astype(BF16))

    n_wt = WINDOW // Q_BLOCK + 1
    tix = [jnp.maximum(qb - (n_wt - 1) + t, 0) for t in range(n_wt)]
    kw = jnp.concatenate([kwin_ref[pl.ds(pl.multiple_of(t * Q_BLOCK, Q_BLOCK), Q_BLOCK), :] for t in tix], axis=0)
    sw = _dot_nt(kw, q2) + tw_ref[...]
    wrow = lax.broadcasted_iota(jnp.int32, (n_wt * Q_BLOCK, nl), 0) + (qb - (n_wt - 1)) * Q_BLOCK
    pw = _col_softmax(jnp.where(wrow >= 0, sw, NEG)).astype(BF16)
    o_w = group_dots(lambda g: jnp.concatenate([vtw_ref[g, t] for t in tix], axis=1), pw)

    per_sel = SEL_LEN // CMP_STRIDE
    n_sel = n_cmp // per_sel
    jio = lax.broadcasted_iota(jnp.int32, (NJ, Q_BLOCK), 0)
    qi = lax.broadcasted_iota(jnp.int32, (NJ, Q_BLOCK), 1)
    qblk = qb * (Q_BLOCK // SEL_LEN) + qi // SEL_LEN
    forced = (jio == 0) | (jio == qblk) | (jio == qblk - 1)
    for g in range(KV_HEADS):
        psum = pc[:, g * gl:g * gl + Q_BLOCK]
        for hh in range(1, HPG):
            psum = psum + pc[:, g * gl + hh * Q_BLOCK:g * gl + (hh + 1) * Q_BLOCK]
        ps_ref[g, 0:8, :] = jnp.zeros((8, Q_BLOCK), F32)
        ps_ref[g, 8:8 + n_cmp, :] = psum
        imp_t = ps_ref[g, pl.ds(8 + _OVERLAP_OFFSETS[0], n_sel, stride=per_sel), :]
        for off in _OVERLAP_OFFSETS[1:]:
            imp_t = imp_t + ps_ref[g, pl.ds(8 + off, n_sel, stride=per_sel), :]
        if n_sel < NJ:
            imp_t = jnp.concatenate([imp_t, jnp.zeros((NJ - n_sel, Q_BLOCK), F32)], axis=0)
        score = jnp.where(forced, FORCE, jnp.where(jio <= qblk, imp_t, -FORCE))
        selneg_t, _ = _select_blocks(score, forced, jio, 0)
        selneg = selneg_t.T.astype(BF16)
        for hh in range(HPG):
            qaug_ref[g * gl + hh * Q_BLOCK:g * gl + (hh + 1) * Q_BLOCK, 0:NJ] = selneg
    qaug_ref[:, NJ:NJ + LANE] = q2

    m_ref[...] = jnp.full((1, nl), -1e38, F32)
    l_ref[...] = jnp.zeros((1, nl), F32)
    acc_ref[...] = jnp.zeros((HEAD_DIM, nl), F32)

    def scores(u0, ntile):
        rows = ntile * SEL_TILE
        ka = kaug_ref[pl.ds(pl.multiple_of(u0 * SEL_TILE, SEL_TILE), rows), :]
        return _dot_nt(ka, qaug_ref[...])

    def fold(s, u0, ntile):
        m_old = m_ref[...]
        m_new = jnp.maximum(m_old, jnp.max(s, axis=0, keepdims=True))
        alpha = jnp.exp2(m_old - m_new)
        p = jnp.exp2(s - m_new)
        l_ref[...] = alpha * l_ref[...] + jnp.sum(p, axis=0, keepdims=True)
        p = p.astype(BF16)
        pv = group_dots(lambda g: vts_ref[g, u0], p[0:SEL_TILE])
        for t in range(1, ntile):
            pv = pv + group_dots(lambda g: vts_ref[g, u0 + t], p[t * SEL_TILE:(t + 1) * SEL_TILE])
        acc_ref[...] = alpha * acc_ref[...] + pv
        m_ref[...] = m_new

    tiles_per_q = SEL_TILE // Q_BLOCK
    u_last = qb // tiles_per_q
    par = qb % tiles_per_q
    n_far = jnp.maximum(u_last - 1, 0)
    n_pair = n_far // 2

    @pl.when(n_pair >= 1)
    def _():
        sa_ref[...] = scores(0, 2)

    def far(i, carry):
        sb_ref[...] = scores(4 * i + 2, 2)
        fold(sa_ref[...], 4 * i, 2)
        sa_ref[...] = scores(2 * jnp.minimum(2 * i + 2, n_pair - 1), 2)
        fold(sb_ref[...], 4 * i + 2, 2)
        return carry

    lax.fori_loop(0, n_pair // 2, far, 0)

    @pl.when(n_pair % 2 == 1)
    def _():
        fold(sa_ref[...], 2 * (n_pair - 1), 2)

    @pl.when(n_far % 2 == 1)
    def _():
        fold(scores(n_far - 1, 1), n_far - 1, 1)

    @pl.when(u_last >= 1)
    def _():
        fold(scores(u_last - 1, 2) + ts_ref[par], u_last - 1, 2)

    @pl.when(u_last == 0)
    def _():
        fold(scores(0, 1) + ts_ref[par, SEL_TILE:2 * SEL_TILE, :], 0, 1)

    o_s = acc_ref[...] / l_ref[...]

    def gate_row(br):
        rows = gt_ref[br * N_HEADS:(br + 1) * N_HEADS, :]
        return jnp.concatenate([rows[hh:hh + 1, :] for hh in range(N_HEADS)], axis=1)

    o_t = gate_row(0) * o_c + gate_row(1) * o_s + gate_row(2) * o_w
    stack = jnp.concatenate([o_t[:, hh * Q_BLOCK:(hh + 1) * Q_BLOCK] for hh in range(N_HEADS)], axis=0)
    o_ref[...] = stack.T.astype(o_ref.dtype)


def _nsa_prompt(qhm, kc, vct, kaug, vts, kwin, vtw, gt, tabs, *, bsz, seq):
    n = bsz * seq
    nqb = seq // Q_BLOCK
    n_cmp = seq // CMP_STRIDE
    nl = N_HEADS * Q_BLOCK
    tc, ts, tw = tabs
    in_specs = [
        pl.BlockSpec((N_HEADS, Q_BLOCK, LANE), lambda b, i: (0, b * nqb + i, 0)),
        pl.BlockSpec((None, n_cmp, LANE), lambda b, i: (b, 0, 0)),
        pl.BlockSpec((None, KV_HEADS, HEAD_DIM, n_cmp), lambda b, i: (b, 0, 0, 0)),
        pl.BlockSpec((seq, NJ + LANE), lambda b, i: (b, 0)),
        pl.BlockSpec((KV_HEADS, seq // SEL_TILE, HEAD_DIM, SEL_TILE), lambda b, i: (0, b, 0, 0)),
        pl.BlockSpec((seq, LANE), lambda b, i: (b, 0)),
        pl.BlockSpec((KV_HEADS, seq // LANE, HEAD_DIM, LANE), lambda b, i: (0, b, 0, 0)),
        pl.BlockSpec((32, Q_BLOCK), lambda b, i: (0, b * nqb + i)),
        pl.BlockSpec(tc.shape, lambda b, i: (0, 0)),
        pl.BlockSpec(ts.shape, lambda b, i: (0, 0, 0)),
        pl.BlockSpec(tw.shape, lambda b, i: (0, 0)),
    ]
    pad = 2 * (Q_BLOCK // CMP_STRIDE)
    return pl.pallas_call(
        functools.partial(_nsa_prompt_kernel, n_cmp=n_cmp),
        grid=(bsz, nqb),
        in_specs=in_specs,
        out_specs=pl.BlockSpec((Q_BLOCK, N_HEADS * HEAD_DIM), lambda b, i: (b * nqb + i, 0)),
        out_shape=jax.ShapeDtypeStruct((n, N_HEADS * HEAD_DIM), BF16),
        scratch_shapes=[pltpu.VMEM((pad + n_cmp + pad, nl), F32), pltpu.VMEM((KV_HEADS, 8 + n_cmp, Q_BLOCK), F32),
                        pltpu.VMEM((nl, NJ + LANE), BF16),
                        pltpu.VMEM((1, nl), F32), pltpu.VMEM((1, nl), F32), pltpu.VMEM((HEAD_DIM, nl), F32),
                        pltpu.VMEM((2 * SEL_TILE, nl), F32), pltpu.VMEM((2 * SEL_TILE, nl), F32)],
        compiler_params=_params(("parallel", "arbitrary")),
        name="nsa_prompt",
    )(qhm, kc, vct, kaug, vts, kwin, vtw, gt, tc, ts, tw)


def _row_softmax_parts(s, s_new):
    m = jnp.maximum(jnp.max(s, axis=1, keepdims=True), s_new)
    p = jnp.exp2(s - m)
    pn = jnp.exp2(s_new - m)
    return p, pn, jnp.sum(p, axis=1, keepdims=True) + pn


def _nsa_sample_cmp_kernel(h_ref, pek_ref, w1k_ref, pev_ref, w1v_ref, w2ka_ref, w2kb_ref, w2va_ref, w2vb_ref,
                           q_ref, tcs_ref, ovs_ref, oc_ref, idx_ref, *, n_sub, qblk):
    hh = h_ref[...]
    ak, av = _compress_finish(hh[:, 0:4 * CMP_HID], hh[:, 4 * CMP_HID:8 * CMP_HID],
                              pek_ref, w1k_ref, pev_ref, w1v_ref, w2ka_ref, w2kb_ref, n_sub)
    kc = (_dot(ak[0], w2ka_ref[...]) + _dot(ak[1], w2kb_ref[...])).astype(BF16)
    vc = (_dot(av[0], w2va_ref[...]) + _dot(av[1], w2vb_ref[...])).astype(BF16)
    q8 = q_ref[...].astype(BF16)
    s = _dot_nt(q8, kc) + tcs_ref[...]
    m = jnp.max(s, axis=1, keepdims=True)
    p = jnp.exp2(s - m)
    pc = p / jnp.sum(p, axis=1, keepdims=True)
    oc_ref[...] = _dot(pc.astype(BF16), vc)
    hrow = lax.broadcasted_iota(jnp.int32, (N_HEADS, n_sub), 0)
    rows = []
    for g in range(KV_HEADS):
        rows.append(jnp.sum(jnp.where(hrow // HPG == g, pc, 0.0), axis=0, keepdims=True))
    psum = jnp.concatenate(rows + [jnp.zeros((N_HEADS - KV_HEADS, n_sub), F32)], axis=0)
    imp = _dot_f32(psum, ovs_ref[...])
    jl = lax.broadcasted_iota(jnp.int32, imp.shape, 1)
    forced = (jl == 0) | (jl == qblk) | (jl == qblk - 1)
    score = jnp.where(forced, FORCE, jnp.where(jl <= qblk, imp, -FORCE))
    _, picked = _select_blocks(score, forced, jl, 1)
    lane = lax.broadcasted_iota(jnp.int32, (N_HEADS, LANE), 1)
    out = jnp.zeros((N_HEADS, LANE), jnp.int32)
    for r, jm in enumerate([0, qblk - 1, qblk] + picked):
        out = jnp.where(lane == r, jm, out)
    idx_ref[...] = out


def _nsa_sample_cmp(hbuf, q8, w, tcs, ovs, *, qblk):
    dbsz, n_sub, _ = hbuf.shape
    c2 = lambda b: (0, 0)
    names = ["pek", "w1k", "pev", "w1v", "w2ka", "w2kb", "w2va", "w2vb"]
    return pl.pallas_call(
        functools.partial(_nsa_sample_cmp_kernel, n_sub=n_sub, qblk=qblk),
        grid=(dbsz,),
        in_specs=[pl.BlockSpec((None, n_sub, 8 * CMP_HID), lambda b: (b, 0, 0))]
        + [pl.BlockSpec(w[k].shape, c2) for k in names]
        + [pl.BlockSpec((None, N_HEADS, LANE), lambda b: (b, 0, 0)),
           pl.BlockSpec(tcs.shape, c2), pl.BlockSpec(ovs.shape, c2)],
        out_specs=[pl.BlockSpec((None, N_HEADS, LANE), lambda b: (b, 0, 0)),
                   pl.BlockSpec((None, N_HEADS, LANE), lambda b: (b, 0, 0))],
        out_shape=[jax.ShapeDtypeStruct((dbsz, N_HEADS, LANE), F32),
                   jax.ShapeDtypeStruct((dbsz, N_HEADS, LANE), jnp.int32)],
        compiler_params=_params(("parallel",)),
        name="nsa_sample_cmp",
    )(hbuf, *[w[k] for k in names], q8, tcs, ovs)


def _nsa_sample_sel_kernel(idx_ref, pt_ref, *refs, qblk, n_win):
    nblk = KV_HEADS * N_SEL_BLOCKS
    blocks = refs[:nblk]
    (q_ref, slcn_ref, winn_ref, state_ref, gs_ref, oc_ref, tnear_ref, b0_ref, tws_ref,
     o_ref, wout_ref) = refs[nblk:]
    b = pl.program_id(0)
    q8f = q_ref[...]
    q8 = q8f.astype(BF16)
    hrow = lax.broadcasted_iota(jnp.int32, (N_HEADS, LANE), 0)
    b0 = b0_ref[:, 0:1]

    slcn = slcn_ref[0:1, :]
    s_new = jnp.sum(q8f * slcn[:, 0:LANE], axis=1, keepdims=True) + b0
    halves = PAGE // SEL_LEN
    lane = lax.broadcasted_iota(jnp.int32, (N_HEADS, PAGE), 1)
    o_sel = []
    for g in range(KV_HEADS):
        ss = []
        for r in range(N_SEL_BLOCKS):
            blk = idx_ref[b, g * N_SEL_BLOCKS + r]
            s = _dot(q8, blocks[g * N_SEL_BLOCKS + r][0].astype(BF16))
            near = jnp.where(blk // halves == qblk // halves - 1, tnear_ref[...], 0.0)
            ok = (blk < qblk) & (lane // SEL_LEN == blk % halves)
            ss.append(jnp.where(ok, s + near, NEG))
        s = jnp.concatenate(ss, axis=1)
        p, pn, l = _row_softmax_parts(s, s_new)
        o = pn * slcn[:, LANE:2 * LANE]
        for r in range(N_SEL_BLOCKS):
            pr = p[:, r * PAGE:(r + 1) * PAGE].astype(BF16)
            o = o + _dot_nt(pr, blocks[g * N_SEL_BLOCKS + r][1].astype(BF16))
        o_sel.append(o / l)
    o_s = jnp.where(hrow // HPG == 0, o_sel[0], o_sel[1])

    winn = winn_ref[0:1, :]
    sw = _dot(q8, state_ref[0].astype(BF16)) + tws_ref[...]
    sw_new = jnp.sum(q8f * winn[:, 0:LANE], axis=1, keepdims=True) + b0
    p, pn, l = _row_softmax_parts(sw, sw_new)
    o_w = (_dot_nt(p.astype(BF16), state_ref[1].astype(BF16)) + pn * winn[:, LANE:2 * LANE]) / l

    o_ref[...] = gs_ref[0] * oc_ref[...] + gs_ref[1] * o_s + gs_ref[2] * o_w
    wlane = lax.broadcasted_iota(jnp.int32, (LANE, n_win), 1)
    for kv in range(2):
        newcol = jnp.broadcast_to(winn[:, kv * LANE:(kv + 1) * LANE], (LANE, LANE)).T
        newcol = jnp.concatenate([newcol] * (n_win // LANE), axis=1)
        wout_ref[kv] = jnp.where(wlane == n_win - 1, newcol, pltpu.roll(state_ref[kv], n_win - 1, 1))


def _nsa_sample_sel(idx, page_table, slc_cache_t, q8, slc_new, win_new, state, gs, oc, tnear, b0, tws, *, qblk):
    dbsz, n_pages = page_table.shape
    n_win = state.shape[-1]
    halves = PAGE // SEL_LEN

    def blk_spec(k):
        def imap(b, idx_ref, pt_ref):
            j = idx_ref[b, k]
            return (pt_ref[b, jnp.minimum(j // halves, n_pages - 1)], 0, 0, 0)
        return pl.BlockSpec((None, 2, LANE, PAGE), imap)

    def per_b(shape):
        nd = len(shape)
        return pl.BlockSpec((None,) + tuple(shape[1:]), lambda b, i, p: (b,) + (0,) * (nd - 1))

    def const(shape):
        nd = len(shape)
        return pl.BlockSpec(tuple(shape), lambda b, i, p: (0,) * nd)

    nblk = KV_HEADS * N_SEL_BLOCKS
    grid_spec = pltpu.PrefetchScalarGridSpec(
        num_scalar_prefetch=2, grid=(dbsz,),
        in_specs=[blk_spec(k) for k in range(nblk)]
        + [per_b(q8.shape), per_b(slc_new.shape), per_b(win_new.shape), per_b(state.shape), per_b(gs.shape),
           per_b(oc.shape), const(tnear.shape), const(b0.shape), const(tws.shape)],
        out_specs=[pl.BlockSpec((N_HEADS, LANE), lambda b, i, p: (b, 0)),
                   pl.BlockSpec((None, 2, LANE, n_win), lambda b, i, p: (b, 0, 0, 0))])
    return pl.pallas_call(
        functools.partial(_nsa_sample_sel_kernel, qblk=qblk, n_win=n_win),
        grid_spec=grid_spec,
        out_shape=[jax.ShapeDtypeStruct((dbsz * N_HEADS, LANE), F32),
                   jax.ShapeDtypeStruct((dbsz, 2, LANE, n_win), F32)],
        compiler_params=_params(("arbitrary",)),
        name="nsa_sample_sel",
    )(idx, page_table, *([slc_cache_t] * nblk), q8, slc_new, win_new, state, gs, oc, tnear, b0, tws)


def _merge_kernel(x_ref, o_ref, u_ref, vn_ref, ga_ref, gb_ref, ws_ref, bs_ref, wbn_ref, wbg_ref, wout_ref,
                  lng_ref, x1_ref, *, tm):
    chunk = ws_ref.shape[1]
    keep = (lax.broadcasted_iota(jnp.int32, (chunk, chunk), 0)
            >= lax.broadcasted_iota(jnp.int32, (chunk, chunk), 1))
    vb = vn_ref[...].astype(BF16)
    cols = []
    for gg in range(GM_GROUPS):
        wt = jnp.where(keep, ws_ref[gg], 0.0).astype(BF16)
        rows = [_dot(wt, vb[c * chunk:(c + 1) * chunk, gg * LANE:(gg + 1) * LANE]) for c in range(tm // chunk)]
        cols.append(jnp.concatenate(rows, axis=0) if len(rows) > 1 else rows[0])
    s = jnp.concatenate(cols, axis=1) + jnp.concatenate([bs_ref[...]] * (tm // chunk), axis=0)
    o_gm = (u_ref[...].astype(F32) * s).astype(BF16)
    if len(wbn_ref.shape) == 2:
        a = _dot(o_ref[...].astype(BF16), wbn_ref[...])
    else:
        a = jnp.zeros((tm, D_MODEL), F32)
        for hh in range(N_HEADS):
            a += _dot(o_ref[pl.ds(hh, tm, stride=N_HEADS), :].astype(BF16), wbn_ref[hh])
    bm = _dot(o_gm, wbg_ref[...])
    mixed = (ga_ref[...].astype(F32) * a + gb_ref[...].astype(F32) * bm).astype(BF16)
    x1_ref[...] = x_ref[...] + _rms(_dot(mixed, wout_ref[...]), lng_ref[...])


def _merge(x, o_nsa, u, vn, ga, gb, ws, bs, wbn, w, *, tm, name):
    n = x.shape[0]
    row = lambda i: (i, 0)
    c2 = lambda i: (0, 0)
    c3 = lambda i: (0, 0, 0)
    ws_spec = pl.BlockSpec(ws.shape, c3)
    o_rows = o_nsa.shape[0] // n
    return pl.pallas_call(
        functools.partial(_merge_kernel, tm=tm),
        grid=(n // tm,),
        in_specs=[pl.BlockSpec((tm, D_MODEL), row), pl.BlockSpec((tm * o_rows, o_nsa.shape[1]), row),
                  pl.BlockSpec((tm, GM_DIM), row), pl.BlockSpec((tm, GM_DIM), row),
                  pl.BlockSpec((tm, D_MODEL), row), pl.BlockSpec((tm, D_MODEL), row),
                  ws_spec, pl.BlockSpec(bs.shape, c2),
                  pl.BlockSpec(wbn.shape, c2 if wbn.ndim == 2 else c3), pl.BlockSpec(w["wbg"].shape, c2),
                  pl.BlockSpec(w["wout"].shape, c2), pl.BlockSpec((1, D_MODEL), c2)],
        out_specs=pl.BlockSpec((tm, D_MODEL), row),
        out_shape=jax.ShapeDtypeStruct((n, D_MODEL), F32),
        compiler_params=_params(("parallel",)),
        name=name,
    )(x, o_nsa, u, vn, ga, gb, ws, bs, wbn, w["wbg"], w["wout"], w["ln_mix_post"])


def _ffn_kernel(x_ref, p_ref, lnf_ref, wg_ref, wu_ref, wd_ref, lnp_ref, wple_ref, wpg_ref, lne_ref, y_ref,
                h2_ref, acc_ref):
    j = pl.program_id(1)

    @pl.when(j == 0)
    def _():
        h2_ref[...] = _rms(x_ref[...], lnf_ref[...]).astype(BF16)
        acc_ref[...] = jnp.zeros(acc_ref.shape, F32)

    h2 = h2_ref[...]
    act = (jax.nn.silu(_dot(h2, wg_ref[...])) * _dot(h2, wu_ref[...])).astype(BF16)
    acc_ref[...] += _dot(act, wd_ref[...])

    @pl.when(j == pl.num_programs(1) - 1)
    def _():
        x2 = x_ref[...] + _rms(acc_ref[...], lnp_ref[...])
        e = _dot(p_ref[...].astype(BF16), wple_ref[...])
        gp = jax.nn.sigmoid(_dot(x2.astype(BF16), wpg_ref[...]))
        y_ref[...] = x2 + _rms(gp * e, lne_ref[...])


def _ffn(x1, ple, w, *, tm, name):
    n = x1.shape[0]
    row = lambda i, j: (i, 0)
    c2 = lambda i, j: (0, 0)
    return pl.pallas_call(
        _ffn_kernel,
        grid=(n // tm, D_FF // FF_CHUNK),
        in_specs=[pl.BlockSpec((tm, D_MODEL), row), pl.BlockSpec((tm, PLE_DIM), row),
                  pl.BlockSpec((1, D_MODEL), c2),
                  pl.BlockSpec((D_MODEL, FF_CHUNK), lambda i, j: (0, j)),
                  pl.BlockSpec((D_MODEL, FF_CHUNK), lambda i, j: (0, j)),
                  pl.BlockSpec((FF_CHUNK, D_MODEL), lambda i, j: (j, 0)),
                  pl.BlockSpec((1, D_MODEL), c2), pl.BlockSpec(w["wple"].shape, c2),
                  pl.BlockSpec(w["wpg"].shape, c2), pl.BlockSpec((1, D_MODEL), c2)],
        out_specs=pl.BlockSpec((tm, D_MODEL), row),
        out_shape=jax.ShapeDtypeStruct((n, D_MODEL), F32),
        scratch_shapes=[pltpu.VMEM((tm, D_MODEL), BF16), pltpu.VMEM((tm, D_MODEL), F32)],
        compiler_params=_params(("parallel", "arbitrary")),
        name=name,
    )(x1, ple, w["ln_ffn_pre"], w["wfg"], w["wfu"], w["wfd"], w["ln_ffn_post"], w["wple"], w["wpg"],
      w["ln_ple_post"])


def _prep_weights(i, ln_mix_pre, w_in, cmp_pe_k, cmp_w1_k, cmp_w2_k, cmp_pe_v, cmp_w1_v, cmp_w2_v, gm_ln_g,
                  gm_ln_b, w_branch_nsa, w_branch_gm, w_out, ln_mix_post, ln_ffn_pre, w_ffn_gate, w_ffn_up,
                  w_ffn_down, ln_ffn_post, w_ple, w_ple_gate, ln_ple_post):
    q_dim = N_HEADS * HEAD_DIM
    kv3 = 6 * KV_HEADS * HEAD_DIM
    n_gate = 3 * N_HEADS
    win = w_in[i]
    o = 0
    wq = win[:, o:o + q_dim]; o += q_dim
    wkv = win[:, o:o + kv3]; o += kv3
    wg = win[:, o:o + n_gate]; o += n_gate
    wuv = win[:, o:o + 2 * GM_DIM]; o += 2 * GM_DIM
    wmg = win[:, o:o + 2 * D_MODEL]
    wq4 = wq.reshape(D_MODEL, N_HEADS, 1, HEAD_DIM)
    half = (np.arange(N_HEADS) // HPG)[None, :, None, None] == np.arange(KV_HEADS)[None, None, :, None]
    wq_pad = jnp.where(half, wq4, 0.0).reshape(D_MODEL, N_HEADS * LANE)
    wg_pad = jnp.pad(wg, ((0, 0), (0, LANE - n_gate)))

    def blockdiag(w1):
        w1s = w1.reshape(CMP_LEN // CMP_STRIDE, CMP_STRIDE, HEAD_DIM, CMP_HID).transpose(1, 2, 0, 3)
        w1s = w1s.reshape(CMP_STRIDE, HEAD_DIM, 2 * CMP_HID)
        z = jnp.zeros_like(w1s)
        bd = jnp.concatenate([jnp.concatenate([w1s, z], axis=2), jnp.concatenate([z, w1s], axis=2)], axis=1)
        return bd.reshape(CMP_STRIDE // 2, 2 * 2 * HEAD_DIM, 4 * CMP_HID)

    wbn4 = w_branch_nsa[i].reshape(N_HEADS, 1, HEAD_DIM, D_MODEL)
    wbn_heads = jnp.where(half.reshape(N_HEADS, KV_HEADS, 1, 1), wbn4, 0.0).reshape(N_HEADS, LANE, D_MODEL)

    zk = jnp.zeros_like(cmp_w2_k[i])
    zv = jnp.zeros_like(cmp_w2_v[i])
    b = lambda a: a.astype(BF16)
    r = lambda a: a[i][None, :].astype(F32)
    return dict(
        ln_mix_pre=r(ln_mix_pre), wq=b(wq_pad), wkv=b(wkv), wg=b(wg_pad), wuv=b(wuv), wmg=b(wmg),
        gm_ln_g=r(gm_ln_g), gm_ln_b=r(gm_ln_b),
        wbk=b(blockdiag(cmp_w1_k[i])), wbv=b(blockdiag(cmp_w1_v[i])),
        pek=jnp.broadcast_to(cmp_pe_k[i].reshape(1, -1), (8, CMP_LEN * HEAD_DIM)).astype(F32), w1k=b(cmp_w1_k[i]),
        pev=jnp.broadcast_to(cmp_pe_v[i].reshape(1, -1), (8, CMP_LEN * HEAD_DIM)).astype(F32), w1v=b(cmp_w1_v[i]),
        w2ka=b(jnp.concatenate([cmp_w2_k[i], zk], axis=1)), w2kb=b(jnp.concatenate([zk, cmp_w2_k[i]], axis=1)),
        w2va=b(jnp.concatenate([cmp_w2_v[i], zv], axis=1)), w2vb=b(jnp.concatenate([zv, cmp_w2_v[i]], axis=1)),
        w2vt=b(cmp_w2_v[i].T),
        wbn=b(w_branch_nsa[i]), wbn_heads=b(wbn_heads), wbg=b(w_branch_gm[i]), wout=b(w_out[i]), ln_mix_post=r(ln_mix_post),
        ln_ffn_pre=r(ln_ffn_pre), wfg=b(w_ffn_gate[i]), wfu=b(w_ffn_up[i]), wfd=b(w_ffn_down[i]),
        ln_ffn_post=r(ln_ffn_post), wple=b(w_ple[i]), wpg=b(w_ple_gate[i]), ln_ple_post=r(ln_ple_post),
    )


def _gate_perm():
    return np.array([hg * 3 + br for br in range(3) for hg in range(N_HEADS)])


def _prompt_layer(x, ple, w, rel_bias, gm_ws, gm_bs):
    bsz, seq, _ = x.shape
    assert seq % SEL_TILE == 0 and seq // SEL_LEN <= NJ and seq >= WINDOW
    n = bsz * seq
    tm = 256
    xf = x.reshape(n, D_MODEL)
    wp = dict(w)
    perm = _gate_perm()
    wp["wg"] = jnp.concatenate([w["wg"][:, perm], w["wg"][:, len(perm):]], axis=1)
    (qhm, cmp, cmpt, slct, wint, kaug, vts, kwin, vtw, gt, u, vn, ga, gb) = _inproj(
        xf, wp, prompt=True, seq=seq, tm=tm)
    kc, vct = _compress_prompt(cmp, w, bsz=bsz, seq=seq)

    o_nsa = _nsa_prompt(qhm, kc, vct, kaug, vts, kwin, vtw, gt, _bias_tables(rel_bias), bsz=bsz, seq=seq)

    bs_tile = jnp.repeat(gm_bs.T, LANE, axis=1).astype(F32)
    x1 = _merge(xf, o_nsa, u, vn, ga, gb, gm_ws.astype(F32), bs_tile, w["wbn"], w, tm=512, name="merge_prompt")
    y = _ffn(x1, ple.reshape(n, PLE_DIM), w, tm=512, name="ffn_prompt")

    def rows_form(at):
        return at.reshape(bsz, 2, KV_HEADS, HEAD_DIM, at.shape[-1]).transpose(0, 4, 1, 2, 3)

    return (y.reshape(bsz, seq, D_MODEL), rows_form(cmpt), rows_form(slct),
            rows_form(wint[:, :, seq - min(WINDOW, seq):]))


def _sample_layer(x, ple, w, rel_bias, gm_ws, gm_bs, cache_cmp, cache_slc, win_buf, page_table):
    dbsz, nq, _ = x.shape
    assert nq == 1
    n_pool = cache_cmp.shape[0]
    n_pages = page_table.shape[1]
    past = n_pages * PAGE
    n_win = win_buf.shape[1]
    assert past % GM_CHUNK == 0 and n_win == WINDOW and past >= WINDOW and dbsz % 8 == 0
    xf = x.reshape(dbsz, D_MODEL)
    q, cmp, slc, win, gate, u, vn, ga, gb = _inproj(xf, w, prompt=False, seq=1, tm=dbsz)

    pps = min(16, n_pages)
    hbuf = _compress_paged(_pages_t(cache_cmp), page_table, w, pps=pps)
    n_sub = past // CMP_STRIDE
    qblk = past // SEL_LEN
    n_sel_pad = -(-(qblk + 1) // LANE) * LANE
    sh = _shifted_bias(rel_bias)
    i = np.arange(n_sub)
    d_c = past - (CMP_STRIDE * i + CMP_LEN - 1)
    tcs = _sample_table(sh, d_c, i < n_sub - 1)
    ovs = jnp.asarray(_overlap_t(n_sub, n_sel_pad).T)
    q8 = q.reshape(dbsz, N_HEADS, LANE)
    oc, idx = _nsa_sample_cmp(hbuf, q8, w, tcs, ovs, qblk=qblk)
    idx2 = idx[:, 0:KV_HEADS, 0:N_SEL_BLOCKS].reshape(dbsz, KV_HEADS * N_SEL_BLOCKS)

    l = np.arange(PAGE)
    tnear = _sample_table(sh, PAGE - l, l >= 0)
    b0 = jnp.broadcast_to(sh[0][:, None] * LOG2E, (N_HEADS, LANE)).astype(F32)
    kpos = np.arange(n_win)
    tws = _sample_table(sh, n_win - kpos, kpos >= 0)
    gs = gate[:, 0:3 * N_HEADS].reshape(dbsz, N_HEADS, 3).transpose(0, 2, 1)
    gs = jnp.broadcast_to(gs[..., None], (dbsz, 3, N_HEADS, LANE))
    o8, win_out = _nsa_sample_sel(
        idx2, page_table, _pages_t(cache_slc), q8,
        jnp.broadcast_to(slc[:, None, :], (dbsz, 8, 256)), jnp.broadcast_to(win[:, None, :], (dbsz, 8, 256)),
        _pages_t(win_buf), gs, oc, tnear, b0, tws, qblk=qblk)
    win_out = win_out.reshape(dbsz, 2, KV_HEADS, HEAD_DIM, n_win).transpose(0, 4, 1, 2, 3)

    ws_diag = gm_ws[:, 0, 0][:, None, None] * jnp.eye(dbsz, dtype=F32)[None]
    bs_tile = jnp.broadcast_to(jnp.repeat(gm_bs[:, 0], LANE)[None, :], (dbsz, GM_GROUPS * LANE)).astype(F32)
    x1 = _merge(xf, o8, u, vn, ga, gb, ws_diag, bs_tile, w["wbn_heads"], w, tm=dbsz, name="merge_sample")
    y = _ffn(x1, ple.reshape(dbsz, PLE_DIM), w, tm=dbsz, name="ffn_sample")
    kvshape = (dbsz, 1, 2, KV_HEADS, HEAD_DIM)
    return (y.reshape(dbsz, 1, D_MODEL), cmp.reshape(kvshape), slc.reshape(kvshape),
            win_out, vn.reshape(dbsz, 1, GM_DIM))


def kernel(x_prompt, x_sample, cache_cmp_kv, cache_slc_kv, state_win_kv, page_table, p_prompt, p_sample, rel_bias,
           ln_mix_pre, w_in, cmp_pe_k, cmp_w1_k, cmp_w2_k, cmp_pe_v, cmp_w1_v, cmp_w2_v, gm_ln_g, gm_ln_b, gm_ws,
           gm_bs, w_branch_nsa, w_branch_gm, w_out, ln_mix_post, ln_ffn_pre, w_ffn_gate, w_ffn_up, w_ffn_down,
           ln_ffn_post, w_ple, w_ple_gate, ln_ple_post):
    depth = w_in.shape[0]
    xp, xs = x_prompt, x_sample
    outs = [[] for _ in range(7)]
    for i in range(depth):
        w = _prep_weights(i, ln_mix_pre, w_in, cmp_pe_k, cmp_w1_k, cmp_w2_k, cmp_pe_v, cmp_w1_v, cmp_w2_v, gm_ln_g,
                          gm_ln_b, w_branch_nsa, w_branch_gm, w_out, ln_mix_post, ln_ffn_pre, w_ffn_gate, w_ffn_up,
                          w_ffn_down, ln_ffn_post, w_ple, w_ple_gate, ln_ple_post)
        xp, c_p, s_p, w_p = _prompt_layer(xp, p_prompt[i], w, rel_bias, gm_ws[i], gm_bs[i])
        xs, c_s, s_s, w_s, v_s = _sample_layer(xs, p_sample[i], w, rel_bias, gm_ws[i], gm_bs[i], cache_cmp_kv[i],
                                               cache_slc_kv[i], state_win_kv[i], page_table)
        for lst, val in zip(outs, (c_p, s_p, w_p, c_s, s_s, w_s, v_s)):
            lst.append(val)
    return (xp, xs) + tuple(jnp.stack(o) for o in outs)
```

```python
import functools
import math

import jax
import jax.numpy as jnp
import numpy as np
from jax import lax
from jax.experimental import pallas as pl
from jax.experimental.pallas import tpu as pltpu

F32 = jnp.float32
BF16 = jnp.bfloat16

D_MODEL = 1024
N_HEADS = 8
KV_HEADS = 2
HPG = N_HEADS // KV_HEADS
HEAD_DIM = 64
CMP_LEN = 32
CMP_STRIDE = 16
CMP_HID = 2 * HEAD_DIM
SEL_LEN = 64
N_SEL_BLOCKS = 16
WINDOW = 512
Q_BLOCK = 128
GM_GROUPS = 4
GM_CHUNK = 128
GM_DIM = D_MODEL // 2
D_FF = -(-8 * D_MODEL // (3 * 256)) * 256
PLE_DIM = 256
N_BUCKETS = 32
MAX_DISTANCE = 128
EPS = 1e-6
NEG = -1e30
LOG2E = 1.4426950408889634
FORCE = 1e6
PAGE = 128

LANE = 128
NJ = 128
SEL_TILE = 256
FF_CHUNK = D_FF // 2
VMEM_LIMIT = 56 * 1024 * 1024


def _dot(a, b):
    return jnp.dot(a, b, preferred_element_type=F32)


def _dot_nt(a, b):
    return lax.dot_general(a, b, (((1,), (1,)), ((), ())), preferred_element_type=F32)


def _dot_f32(a, b):
    return jnp.dot(a, b, preferred_element_type=F32, precision=lax.Precision.HIGHEST)


def _rms(x, g):
    return x * lax.rsqrt(jnp.mean(x * x, axis=-1, keepdims=True) + EPS) * g


def _params(sem):
    return pltpu.CompilerParams(dimension_semantics=sem, vmem_limit_bytes=VMEM_LIMIT)


def _t5_bucket_np(d):
    d = np.maximum(d, 0)
    max_exact = N_BUCKETS // 2
    ratio = (np.log(np.maximum(d, 1).astype(np.float32) / np.float32(max_exact))
             / np.float32(math.log(MAX_DISTANCE / max_exact)))
    large = np.minimum(max_exact + (ratio * np.float32(N_BUCKETS - max_exact)).astype(np.int32), N_BUCKETS - 1)
    return np.where(d < max_exact, d, large)


_BUCKET = _t5_bucket_np(np.arange(MAX_DISTANCE + 1))
assert _BUCKET[MAX_DISTANCE] == N_BUCKETS - 1


def _shifted_bias(rel_bias):
    return rel_bias[_BUCKET] - rel_bias[N_BUCKETS - 1][None, :]


def _sample_table(sh, dist, valid):
    t = sh[np.minimum(np.maximum(dist, 0), MAX_DISTANCE)] * LOG2E
    return jnp.where(valid[:, None], t, NEG).T.astype(F32)


_BUCKET_START = [int(np.argmax(_BUCKET >= k)) for k in range(N_BUCKETS)]
_BAND_ROWS = 3 * (Q_BLOCK // CMP_STRIDE)


def _bias_tables_kernel(rb_ref, tc_ref, ts_ref, tw_ref):
    def fill(store, rows, chunk, c0, stride, dmax):
        def body(c, carry):
            r0 = pl.multiple_of(c * chunk, chunk)
            r = r0 + lax.broadcasted_iota(jnp.int32, (chunk, Q_BLOCK), 0)
            d = c0 + lax.broadcasted_iota(jnp.int32, (chunk, Q_BLOCK), 1) - stride * r
            ok = (d >= 0) & (d <= dmax)
            dc = jnp.minimum(d, MAX_DISTANCE)
            for hd in range(N_HEADS):
                v = jnp.full((chunk, Q_BLOCK), rb_ref[0, hd], F32)
                for k in range(1, N_BUCKETS):
                    v = jnp.where(dc >= _BUCKET_START[k], rb_ref[k, hd], v)
                v = (v - rb_ref[N_BUCKETS - 1, hd]) * LOG2E
                store(r0, chunk, hd, jnp.where(ok, v, NEG))
            return carry
        lax.fori_loop(0, rows // chunk, body, 0)

    def lanes(hd):
        return slice(hd * Q_BLOCK, (hd + 1) * Q_BLOCK)

    def st_c(r0, n, hd, v):
        tc_ref[pl.ds(r0, n), lanes(hd)] = v

    def st_w(r0, n, hd, v):
        tw_ref[pl.ds(r0, n), lanes(hd)] = v

    big = 1 << 30
    fill(st_c, _BAND_ROWS, 8, 2 * Q_BLOCK - (CMP_LEN - 1), CMP_STRIDE, big)
    for par in range(SEL_TILE // Q_BLOCK):
        def st_s(r0, n, hd, v, par=par):
            ts_ref[par, pl.ds(r0, n), lanes(hd)] = v
        fill(st_s, 2 * SEL_TILE, 64, SEL_TILE + Q_BLOCK * par, 1, big)
    fill(st_w, WINDOW + Q_BLOCK, 64, WINDOW, 1, WINDOW)


def _bias_tables(rel_bias):
    nl = N_HEADS * Q_BLOCK
    return pl.pallas_call(
        _bias_tables_kernel,
        in_specs=[pl.BlockSpec(memory_space=pltpu.SMEM)],
        out_shape=[jax.ShapeDtypeStruct((_BAND_ROWS, nl), F32),
                   jax.ShapeDtypeStruct((SEL_TILE // Q_BLOCK, 2 * SEL_TILE, nl), F32),
                   jax.ShapeDtypeStruct((WINDOW + Q_BLOCK, nl), F32)],
        compiler_params=pltpu.CompilerParams(vmem_limit_bytes=VMEM_LIMIT),
        name="bias_tables",
    )(rel_bias.astype(F32))


def _overlap_t(n_cmp, n_sel_pad):
    ci = np.arange(n_cmp)[None, :] * CMP_STRIDE
    sj = np.arange(n_sel_pad)[:, None] * SEL_LEN
    return ((ci <= sj + SEL_LEN - 1) & (ci + CMP_LEN - 1 >= sj)).astype(np.float32)


def _overlap_offsets():
    per = SEL_LEN // CMP_STRIDE
    ov = _overlap_t(16 * per, 16)
    jj, ii = np.nonzero(ov)
    offs = sorted({int(i - per * j) for j, i in zip(jj, ii)})
    band = np.isin(np.arange(16 * per)[None, :] - per * np.arange(16)[:, None], offs)
    assert (band == (ov > 0)).all() and min(offs) >= -8
    return offs


_OVERLAP_OFFSETS = _overlap_offsets()


def _inproj_kernel(x_ref, lng_ref, wq_ref, wkv_ref, wg_ref, wuv_ref, wmg_ref, glg_ref, glb_ref, *outs,
                   prompt, seq, tm):
    h = _rms(x_ref[...], lng_ref[...]).astype(BF16)
    q = _dot(h, wq_ref[...]) * (HEAD_DIM ** -0.5 * LOG2E)
    kv = _dot(h, wkv_ref[...])
    gate = jax.nn.sigmoid(_dot(h, wg_ref[...]))
    uv = jax.nn.gelu(_dot(h, wuv_ref[...]))
    u = uv[:, :GM_DIM]
    v = uv[:, GM_DIM:]
    mu = jnp.mean(v, axis=-1, keepdims=True)
    var = jnp.mean(jnp.square(v - mu), axis=-1, keepdims=True)
    vn = (v - mu) * lax.rsqrt(var + EPS) * glg_ref[...] + glb_ref[...]
    mg = jax.nn.sigmoid(_dot(h, wmg_ref[...]))
    cmp, slc, win = kv[:, 0:256], kv[:, 256:512], kv[:, 512:768]
    if prompt:
        (q_ref, cmp_ref, cmpt_ref, slct_ref, wint_ref, kaug_ref, vts_ref, kwin_ref, vtw_ref, gt_ref,
         u_ref, vn_ref, ga_ref, gb_ref) = outs
        for hh in range(N_HEADS):
            q_ref[hh] = q[:, hh * LANE:(hh + 1) * LANE].astype(BF16)
        pos = (pl.program_id(0) * tm + lax.broadcasted_iota(jnp.int32, (tm, NJ), 0)) % seq
        onehot = (lax.broadcasted_iota(jnp.int32, (tm, NJ), 1) == pos // SEL_LEN)
        kaug_ref[:, 0:NJ] = onehot.astype(BF16)
        kaug_ref[:, NJ:NJ + LANE] = slc[:, 0:LANE].astype(BF16)
        kwin_ref[...] = win[:, 0:LANE].astype(BF16)
        slc_t = slc.T
        win_t = win.T
        for g in range(KV_HEADS):
            r0 = LANE + HEAD_DIM * g
            for c in range(tm // SEL_TILE):
                vts_ref[g, c] = slc_t[r0:r0 + HEAD_DIM, c * SEL_TILE:(c + 1) * SEL_TILE].astype(BF16)
            for c in range(tm // LANE):
                vtw_ref[g, c] = win_t[r0:r0 + HEAD_DIM, c * LANE:(c + 1) * LANE].astype(BF16)
        gt_ref[...] = gate.T[0:32, :]
        cmpt_ref[...] = cmp.T
        slct_ref[...] = slc_t
        wint_ref[...] = win_t
    else:
        q_ref, cmp_ref, slc_ref, win_ref, gate_ref, u_ref, vn_ref, ga_ref, gb_ref = outs
        q_ref[...] = q
        gate_ref[...] = gate
        slc_ref[...] = slc
        win_ref[...] = win
    cmp_ref[...] = cmp
    u_ref[...] = u.astype(u_ref.dtype)
    vn_ref[...] = vn.astype(vn_ref.dtype)
    ga_ref[...] = mg[:, :D_MODEL].astype(ga_ref.dtype)
    gb_ref[...] = mg[:, D_MODEL:].astype(gb_ref.dtype)


def _inproj(x, w, *, prompt, seq, tm):
    n = x.shape[0]
    row = lambda i: (i, 0)
    const = lambda i: (0, 0)
    in_specs = [pl.BlockSpec((tm, D_MODEL), row), pl.BlockSpec((1, D_MODEL), const),
                pl.BlockSpec(w["wq"].shape, const), pl.BlockSpec(w["wkv"].shape, const),
                pl.BlockSpec(w["wg"].shape, const), pl.BlockSpec(w["wuv"].shape, const),
                pl.BlockSpec(w["wmg"].shape, const), pl.BlockSpec((1, GM_DIM), const),
                pl.BlockSpec((1, GM_DIM), const)]
    kv_shapes = [jax.ShapeDtypeStruct((n, 256), F32)] * 3
    kv_specs = [pl.BlockSpec((tm, 256), row)] * 3
    if prompt:
        act = BF16
        tiles = seq // tm
        kvt_shapes = [jax.ShapeDtypeStruct((n // seq, 256, seq), F32)] * 3
        kvt_specs = [pl.BlockSpec((None, 256, tm), lambda i: (i // tiles, 0, i % tiles))] * 3
        out_shape = ([jax.ShapeDtypeStruct((N_HEADS, n, LANE), BF16)] + kv_shapes[:1] + kvt_shapes + [
            jax.ShapeDtypeStruct((n, NJ + LANE), BF16),
            jax.ShapeDtypeStruct((KV_HEADS, n // SEL_TILE, HEAD_DIM, SEL_TILE), BF16),
            jax.ShapeDtypeStruct((n, LANE), BF16),
            jax.ShapeDtypeStruct((KV_HEADS, n // LANE, HEAD_DIM, LANE), BF16),
            jax.ShapeDtypeStruct((32, n), F32)])
        out_specs = ([pl.BlockSpec((N_HEADS, tm, LANE), lambda i: (0, i, 0))] + kv_specs[:1] + kvt_specs + [
            pl.BlockSpec((tm, NJ + LANE), row),
            pl.BlockSpec((KV_HEADS, tm // SEL_TILE, HEAD_DIM, SEL_TILE), lambda i: (0, i, 0, 0)),
            pl.BlockSpec((tm, LANE), row),
            pl.BlockSpec((KV_HEADS, tm // LANE, HEAD_DIM, LANE), lambda i: (0, i, 0, 0)),
            pl.BlockSpec((32, tm), lambda i: (0, i))])
    else:
        act = F32
        out_shape = ([jax.ShapeDtypeStruct((n, N_HEADS * LANE), F32)] + kv_shapes
                     + [jax.ShapeDtypeStruct((n, LANE), F32)])
        out_specs = [pl.BlockSpec((tm, N_HEADS * LANE), row)] + kv_specs + [pl.BlockSpec((tm, LANE), row)]
    out_shape += [jax.ShapeDtypeStruct((n, GM_DIM), act)] * 2 + [jax.ShapeDtypeStruct((n, D_MODEL), act)] * 2
    out_specs += [pl.BlockSpec((tm, GM_DIM), row)] * 2 + [pl.BlockSpec((tm, D_MODEL), row)] * 2
    return pl.pallas_call(
        functools.partial(_inproj_kernel, prompt=prompt, seq=seq, tm=tm),
        grid=(n // tm,), in_specs=in_specs, out_specs=out_specs, out_shape=out_shape,
        compiler_params=_params(("parallel",)),
        name="inproj_prompt" if prompt else "inproj_sample",
    )(x, w["ln_mix_pre"], w["wq"], w["wkv"], w["wg"], w["wuv"], w["wmg"], w["gm_ln_g"], w["gm_ln_b"])


def _row_pair(x_ref, r, n_sub):
    return jnp.concatenate([x_ref[pl.ds(r, n_sub, stride=CMP_STRIDE), :],
                            x_ref[pl.ds(r + 1, n_sub, stride=CMP_STRIDE), :]], axis=1).astype(BF16)


def _compress_finish(hk, hv, pek_ref, w1k_ref, pev_ref, w1v_ref, w2ka_ref, w2kb_ref, n_sub):
    pwk = _dot(pek_ref[...].astype(BF16), w1k_ref[...])[0:1, :]
    pwv = _dot(pev_ref[...].astype(BF16), w1v_ref[...])[0:1, :]

    def act(hh, g, pw):
        a0 = hh[:, 256 * g:256 * g + CMP_HID]
        a1 = hh[:, 256 * g + CMP_HID:256 * g + 2 * CMP_HID]
        return jax.nn.silu(a0 + pltpu.roll(a1, n_sub - 1, 0) + pw).astype(BF16)

    return [act(hk, 0, pwk), act(hk, 1, pwk)], [act(hv, 0, pwv), act(hv, 1, pwv)]


def _compress_prompt_kernel(cmpk_ref, cmpv_ref, wbk_ref, wbv_ref, pek_ref, w1k_ref, pev_ref, w1v_ref,
                            w2ka_ref, w2kb_ref, w2vt_ref, kc_ref, vct_ref, *, n_sub):
    hk = jnp.zeros((n_sub, 4 * CMP_HID), F32)
    hv = jnp.zeros((n_sub, 4 * CMP_HID), F32)
    for r in range(0, CMP_STRIDE, 2):
        hk += _dot(_row_pair(cmpk_ref, r, n_sub), wbk_ref[r // 2])
        hv += _dot(_row_pair(cmpv_ref, r, n_sub), wbv_ref[r // 2])
    ak, av = _compress_finish(hk, hv, pek_ref, w1k_ref, pev_ref, w1v_ref, w2ka_ref, w2kb_ref, n_sub)
    kc_ref[...] = (_dot(ak[0], w2ka_ref[...]) + _dot(ak[1], w2kb_ref[...])).astype(BF16)
    for g in range(KV_HEADS):
        vct_ref[g] = _dot_nt(w2vt_ref[...], av[g]).astype(BF16)


def _compress_prompt(cmp, w, *, bsz, seq):
    n_sub = seq // CMP_STRIDE
    const2 = lambda b: (0, 0)
    const3 = lambda b: (0, 0, 0)
    return pl.pallas_call(
        functools.partial(_compress_prompt_kernel, n_sub=n_sub),
        grid=(bsz,),
        in_specs=[pl.BlockSpec((seq, LANE), lambda b: (b, 0)), pl.BlockSpec((seq, LANE), lambda b: (b, 1)),
                  pl.BlockSpec(w["wbk"].shape, const3), pl.BlockSpec(w["wbv"].shape, const3),
                  pl.BlockSpec(w["pek"].shape, const2), pl.BlockSpec(w["w1k"].shape, const2),
                  pl.BlockSpec(w["pev"].shape, const2), pl.BlockSpec(w["w1v"].shape, const2),
                  pl.BlockSpec(w["w2ka"].shape, const2), pl.BlockSpec(w["w2kb"].shape, const2),
                  pl.BlockSpec(w["w2vt"].shape, const2)],
        out_specs=[pl.BlockSpec((None, n_sub, LANE), lambda b: (b, 0, 0)),
                   pl.BlockSpec((None, KV_HEADS, HEAD_DIM, n_sub), lambda b: (b, 0, 0, 0))],
        out_shape=[jax.ShapeDtypeStruct((bsz, n_sub, LANE), BF16),
                   jax.ShapeDtypeStruct((bsz, KV_HEADS, HEAD_DIM, n_sub), BF16)],
        compiler_params=_params(("parallel",)),
        name="compress_prompt",
    )(cmp, cmp, w["wbk"], w["wbv"], w["pek"], w["w1k"], w["pev"], w["w1v"], w["w2ka"], w["w2kb"], w["w2vt"])


def _compress_paged_kernel(pt_ref, *refs, pps):
    kpages, vpages = refs[:pps], refs[pps:2 * pps]
    wbk_ref, wbv_ref, h_ref, xk_ref, xv_ref = refs[2 * pps:]
    sub_per_page = PAGE // CMP_STRIDE
    n = pps * sub_per_page
    for k in range(pps):
        xk_ref[k * PAGE:(k + 1) * PAGE, :] = kpages[k][...].T
        xv_ref[k * PAGE:(k + 1) * PAGE, :] = vpages[k][...].T
    hk = jnp.zeros((n, 4 * CMP_HID), F32)
    hv = jnp.zeros((n, 4 * CMP_HID), F32)
    for r in range(0, CMP_STRIDE, 2):
        hk += _dot(_row_pair(xk_ref, r, n), wbk_ref[r // 2])
        hv += _dot(_row_pair(xv_ref, r, n), wbv_ref[r // 2])
    h_ref[:, 0:4 * CMP_HID] = hk
    h_ref[:, 4 * CMP_HID:8 * CMP_HID] = hv


def _compress_paged(cache_t, page_table, w, *, pps):
    dbsz, n_pages = page_table.shape
    sub_per_page = PAGE // CMP_STRIDE
    n_sub = n_pages * sub_per_page

    def page_spec(k, c):
        return pl.BlockSpec((None, None, LANE, PAGE), lambda b, t, pt: (pt[b, t * pps + k], c, 0, 0))

    const3 = lambda b, t, pt: (0, 0, 0)
    grid_spec = pltpu.PrefetchScalarGridSpec(
        num_scalar_prefetch=1, grid=(dbsz, n_pages // pps),
        in_specs=[page_spec(k, c) for c in range(2) for k in range(pps)]
        + [pl.BlockSpec(w["wbk"].shape, const3), pl.BlockSpec(w["wbv"].shape, const3)],
        out_specs=pl.BlockSpec((None, pps * sub_per_page, 8 * CMP_HID), lambda b, t, pt: (b, t, 0)),
        scratch_shapes=[pltpu.VMEM((pps * PAGE, LANE), F32), pltpu.VMEM((pps * PAGE, LANE), F32)])
    return pl.pallas_call(
        functools.partial(_compress_paged_kernel, pps=pps),
        grid_spec=grid_spec,
        out_shape=jax.ShapeDtypeStruct((dbsz, n_sub, 8 * CMP_HID), F32),
        compiler_params=_params(("parallel", "arbitrary")),
        name="compress_paged",
    )(page_table, *([cache_t] * (2 * pps)), w["wbk"], w["wbv"])


def _pages_t(cache):
    n, npos = cache.shape[0], cache.shape[1]
    return cache.transpose(0, 2, 3, 4, 1).reshape(n, 2, KV_HEADS * HEAD_DIM, npos)


def _col_softmax(s):
    p, inv = _col_softmax_parts(s)
    return p * inv


def _col_softmax_parts(s):
    m = jnp.max(s, axis=0, keepdims=True)
    p = jnp.exp2(s - m)
    return p, jnp.where(m > 0.5 * NEG, 1.0 / jnp.sum(p, axis=0, keepdims=True), 0.0)


N_FORCED = 3


def _select_blocks(score, forced, jio, axis):
    selneg = jnp.where(forced, 0.0, NEG)
    score = jnp.where(forced, -3e38, score)
    picked = []
    for _ in range(N_SEL_BLOCKS - N_FORCED):
        mx = jnp.max(score, axis=axis, keepdims=True)
        jm = jnp.min(jnp.where(score == mx, jio, 1 << 20), axis=axis, keepdims=True)
        pick = jio == jm
        selneg = jnp.where(pick, 0.0, selneg)
        score = jnp.where(pick, -3e38, score)
        picked.append(jm)
    return selneg, picked


def _nsa_prompt_kernel(q_ref, kc_ref, vct_ref, kaug_ref, vts_ref, kwin_ref, vtw_ref, gt_ref,
                       tc_ref, ts_ref, tw_ref, o_ref,
                       sc_ref, ps_ref, qaug_ref, m_ref, l_ref, acc_ref, sa_ref, sb_ref, *, n_cmp):
    qb = pl.program_id(1)
    gl = HPG * Q_BLOCK
    nl = KV_HEADS * gl
    q2 = q_ref[...].reshape(nl, LANE)

    def group_dots(vt, p):
        return jnp.concatenate([_dot(vt(g), p[:, g * gl:(g + 1) * gl]) for g in range(KV_HEADS)], axis=1)

    pad = 2 * (Q_BLOCK // CMP_STRIDE)
    band = 3 * (Q_BLOCK // CMP_STRIDE)
    sc_ref[0:pad, :] = jnp.zeros((pad, nl), F32)
    sc_ref[pad:pad + n_cmp, :] = _dot_nt(kc_ref[...], q2)
    w0 = pl.multiple_of(qb * (Q_BLOCK // CMP_STRIDE), 8)
    sc_ref[pl.ds(w0, band), :] += tc_ref[...]
    sc = sc_ref[pad:pad + n_cmp, :]
    row = lax.broadcasted_iota(jnp.int32, (n_cmp, nl), 0)
    sc = jnp.where(row < (qb + 1) * (Q_BLOCK // CMP_STRIDE), sc, NEG)
    pc = _col_softmax(sc)
    o_c = group_dots(lambda g: vct_ref[g], pc.astype(BF16))

    n_wt = WINDOW // Q_BLOCK + 1
    tix = [jnp.maximum(qb - (n_wt - 1) + t, 0) for t in range(n_wt)]
    kw = jnp.concatenate([kwin_ref[pl.ds(pl.multiple_of(t * Q_BLOCK, Q_BLOCK), Q_BLOCK), :] for t in tix], axis=0)
    sw = _dot_nt(kw, q2) + tw_ref[...]
    wrow = lax.broadcasted_iota(jnp.int32, (n_wt * Q_BLOCK, nl), 0) + (qb - (n_wt - 1)) * Q_BLOCK
    pw, winv = _col_softmax_parts(jnp.where(wrow >= 0, sw, NEG))
    o_w = group_dots(lambda g: jnp.concatenate([vtw_ref[g, t] for t in tix], axis=1), pw.astype(BF16)) * winv

    per_sel = SEL_LEN // CMP_STRIDE
    n_sel = n_cmp // per_sel
    jio = lax.broadcasted_iota(jnp.int32, (NJ, Q_BLOCK), 0)
    qi = lax.broadcasted_iota(jnp.int32, (NJ, Q_BLOCK), 1)
    qblk = qb * (Q_BLOCK // SEL_LEN) + qi // SEL_LEN
    forced = (jio == 0) | (jio == qblk) | (jio == qblk - 1)
    for g in range(KV_HEADS):
        psum = pc[:, g * gl:g * gl + Q_BLOCK]
        for hh in range(1, HPG):
            psum = psum + pc[:, g * gl + hh * Q_BLOCK:g * gl + (hh + 1) * Q_BLOCK]
        ps_ref[g, 0:8, :] = jnp.zeros((8, Q_BLOCK), F32)
        ps_ref[g, 8:8 + n_cmp, :] = psum
        imp_t = ps_ref[g, pl.ds(8 + _OVERLAP_OFFSETS[0], n_sel, stride=per_sel), :]
        for off in _OVERLAP_OFFSETS[1:]:
            imp_t = imp_t + ps_ref[g, pl.ds(8 + off, n_sel, stride=per_sel), :]
        if n_sel < NJ:
            imp_t = jnp.concatenate([imp_t, jnp.zeros((NJ - n_sel, Q_BLOCK), F32)], axis=0)
        score = jnp.where(forced, FORCE, jnp.where(jio <= qblk, imp_t, -FORCE))
        selneg_t, _ = _select_blocks(score, forced, jio, 0)
        selneg = selneg_t.T.astype(BF16)
        for hh in range(HPG):
            qaug_ref[g * gl + hh * Q_BLOCK:g * gl + (hh + 1) * Q_BLOCK, 0:NJ] = selneg
    qaug_ref[:, NJ:NJ + LANE] = q2

    m_ref[...] = jnp.full((1, nl), -1e38, F32)
    l_ref[...] = jnp.zeros((1, nl), F32)
    acc_ref[...] = jnp.zeros((HEAD_DIM, nl), F32)

    def scores(u0, ntile):
        rows = ntile * SEL_TILE
        ka = kaug_ref[pl.ds(pl.multiple_of(u0 * SEL_TILE, SEL_TILE), rows), :]
        return _dot_nt(ka, qaug_ref[...])

    def fold(s, u0, ntile):
        m_old = m_ref[...]
        m_new = jnp.maximum(m_old, jnp.max(s, axis=0, keepdims=True))
        alpha = jnp.exp2(m_old - m_new)
        p = jnp.exp2(s - m_new)
        l_ref[...] = alpha * l_ref[...] + jnp.sum(p, axis=0, keepdims=True)
        p = p.astype(BF16)
        pv = group_dots(lambda g: vts_ref[g, u0], p[0:SEL_TILE])
        for t in range(1, ntile):
            pv = pv + group_dots(lambda g: vts_ref[g, u0 + t], p[t * SEL_TILE:(t + 1) * SEL_TILE])
        acc_ref[...] = alpha * acc_ref[...] + pv
        m_ref[...] = m_new

    tiles_per_q = SEL_TILE // Q_BLOCK
    u_last = qb // tiles_per_q
    par = qb % tiles_per_q
    n_far = jnp.maximum(u_last - 1, 0)
    n_pair = n_far // 2

    @pl.when(n_pair >= 1)
    def _():
        sa_ref[...] = scores(0, 2)

    def far(i, carry):
        sb_ref[...] = scores(4 * i + 2, 2)
        fold(sa_ref[...], 4 * i, 2)
        sa_ref[...] = scores(2 * jnp.minimum(2 * i + 2, n_pair - 1), 2)
        fold(sb_ref[...], 4 * i + 2, 2)
        return carry

    lax.fori_loop(0, n_pair // 2, far, 0)

    @pl.when(n_pair % 2 == 1)
    def _():
        fold(sa_ref[...], 2 * (n_pair - 1), 2)

    @pl.when(n_far % 2 == 1)
    def _():
        s = scores(u_last - 2, 3)
        s = jnp.concatenate([s[0:SEL_TILE], s[SEL_TILE:3 * SEL_TILE] + ts_ref[par]], axis=0)
        fold(s, u_last - 2, 3)

    @pl.when((n_far % 2 == 0) & (u_last >= 1))
    def _():
        fold(scores(u_last - 1, 2) + ts_ref[par], u_last - 1, 2)

    @pl.when(u_last == 0)
    def _():
        fold(scores(0, 1) + ts_ref[par, SEL_TILE:2 * SEL_TILE, :], 0, 1)

    o_s = acc_ref[...] / l_ref[...]

    def gate_row(br):
        rows = gt_ref[br * N_HEADS:(br + 1) * N_HEADS, :]
        return jnp.concatenate([rows[hh:hh + 1, :] for hh in range(N_HEADS)], axis=1)

    o_t = gate_row(0) * o_c + gate_row(1) * o_s + gate_row(2) * o_w
    stack = jnp.concatenate([o_t[:, hh * Q_BLOCK:(hh + 1) * Q_BLOCK] for hh in range(N_HEADS)], axis=0)
    o_ref[...] = stack.T.astype(o_ref.dtype)


def _nsa_prompt(qhm, kc, vct, kaug, vts, kwin, vtw, gt, tabs, *, bsz, seq):
    n = bsz * seq
    nqb = seq // Q_BLOCK
    n_cmp = seq // CMP_STRIDE
    nl = N_HEADS * Q_BLOCK
    tc, ts, tw = tabs
    in_specs = [
        pl.BlockSpec((N_HEADS, Q_BLOCK, LANE), lambda b, i: (0, b * nqb + i, 0)),
        pl.BlockSpec((None, n_cmp, LANE), lambda b, i: (b, 0, 0)),
        pl.BlockSpec((None, KV_HEADS, HEAD_DIM, n_cmp), lambda b, i: (b, 0, 0, 0)),
        pl.BlockSpec((seq, NJ + LANE), lambda b, i: (b, 0)),
        pl.BlockSpec((KV_HEADS, seq // SEL_TILE, HEAD_DIM, SEL_TILE), lambda b, i: (0, b, 0, 0)),
        pl.BlockSpec((seq, LANE), lambda b, i: (b, 0)),
        pl.BlockSpec((KV_HEADS, seq // LANE, HEAD_DIM, LANE), lambda b, i: (0, b, 0, 0)),
        pl.BlockSpec((32, Q_BLOCK), lambda b, i: (0, b * nqb + i)),
        pl.BlockSpec(tc.shape, lambda b, i: (0, 0)),
        pl.BlockSpec(ts.shape, lambda b, i: (0, 0, 0)),
        pl.BlockSpec(tw.shape, lambda b, i: (0, 0)),
    ]
    pad = 2 * (Q_BLOCK // CMP_STRIDE)
    return pl.pallas_call(
        functools.partial(_nsa_prompt_kernel, n_cmp=n_cmp),
        grid=(bsz, nqb),
        in_specs=in_specs,
        out_specs=pl.BlockSpec((Q_BLOCK, N_HEADS * HEAD_DIM), lambda b, i: (b * nqb + i, 0)),
        out_shape=jax.ShapeDtypeStruct((n, N_HEADS * HEAD_DIM), BF16),
        scratch_shapes=[pltpu.VMEM((pad + n_cmp + pad, nl), F32), pltpu.VMEM((KV_HEADS, 8 + n_cmp, Q_BLOCK), F32),
                        pltpu.VMEM((nl, NJ + LANE), BF16),
                        pltpu.VMEM((1, nl), F32), pltpu.VMEM((1, nl), F32), pltpu.VMEM((HEAD_DIM, nl), F32),
                        pltpu.VMEM((2 * SEL_TILE, nl), F32), pltpu.VMEM((2 * SEL_TILE, nl), F32)],
        compiler_params=_params(("parallel", "arbitrary")),
        name="nsa_prompt",
    )(qhm, kc, vct, kaug, vts, kwin, vtw, gt, tc, ts, tw)


def _row_softmax_parts(s, s_new):
    m = jnp.maximum(jnp.max(s, axis=1, keepdims=True), s_new)
    p = jnp.exp2(s - m)
    pn = jnp.exp2(s_new - m)
    return p, pn, jnp.sum(p, axis=1, keepdims=True) + pn


def _nsa_sample_cmp_kernel(h_ref, pek_ref, w1k_ref, pev_ref, w1v_ref, w2ka_ref, w2kb_ref, w2va_ref, w2vb_ref,
                           q_ref, tcs_ref, ovs_ref, oc_ref, idx_ref, *, n_sub, qblk):
    hh = h_ref[...]
    ak, av = _compress_finish(hh[:, 0:4 * CMP_HID], hh[:, 4 * CMP_HID:8 * CMP_HID],
                              pek_ref, w1k_ref, pev_ref, w1v_ref, w2ka_ref, w2kb_ref, n_sub)
    kc = (_dot(ak[0], w2ka_ref[...]) + _dot(ak[1], w2kb_ref[...])).astype(BF16)
    vc = (_dot(av[0], w2va_ref[...]) + _dot(av[1], w2vb_ref[...])).astype(BF16)
    q8 = q_ref[...].astype(BF16)
    s = _dot_nt(q8, kc) + tcs_ref[...]
    m = jnp.max(s, axis=1, keepdims=True)
    p = jnp.exp2(s - m)
    pc = p / jnp.sum(p, axis=1, keepdims=True)
    oc_ref[...] = _dot(pc.astype(BF16), vc)
    hrow = lax.broadcasted_iota(jnp.int32, (N_HEADS, n_sub), 0)
    rows = []
    for g in range(KV_HEADS):
        rows.append(jnp.sum(jnp.where(hrow // HPG == g, pc, 0.0), axis=0, keepdims=True))
    psum = jnp.concatenate(rows + [jnp.zeros((N_HEADS - KV_HEADS, n_sub), F32)], axis=0)
    imp = _dot_f32(psum, ovs_ref[...])
    jl = lax.broadcasted_iota(jnp.int32, imp.shape, 1)
    forced = (jl == 0) | (jl == qblk) | (jl == qblk - 1)
    score = jnp.where(forced, FORCE, jnp.where(jl <= qblk, imp, -FORCE))
    _, picked = _select_blocks(score, forced, jl, 1)
    lane = lax.broadcasted_iota(jnp.int32, (N_HEADS, LANE), 1)
    out = jnp.zeros((N_HEADS, LANE), jnp.int32)
    for r, jm in enumerate([0, qblk - 1, qblk] + picked):
        out = jnp.where(lane == r, jm, out)
    idx_ref[...] = out


def _nsa_sample_cmp(hbuf, q8, w, tcs, ovs, *, qblk):
    dbsz, n_sub, _ = hbuf.shape
    c2 = lambda b: (0, 0)
    names = ["pek", "w1k", "pev", "w1v", "w2ka", "w2kb", "w2va", "w2vb"]
    return pl.pallas_call(
        functools.partial(_nsa_sample_cmp_kernel, n_sub=n_sub, qblk=qblk),
        grid=(dbsz,),
        in_specs=[pl.BlockSpec((None, n_sub, 8 * CMP_HID), lambda b: (b, 0, 0))]
        + [pl.BlockSpec(w[k].shape, c2) for k in names]
        + [pl.BlockSpec((None, N_HEADS, LANE), lambda b: (b, 0, 0)),
           pl.BlockSpec(tcs.shape, c2), pl.BlockSpec(ovs.shape, c2)],
        out_specs=[pl.BlockSpec((None, N_HEADS, LANE), lambda b: (b, 0, 0)),
                   pl.BlockSpec((None, N_HEADS, LANE), lambda b: (b, 0, 0))],
        out_shape=[jax.ShapeDtypeStruct((dbsz, N_HEADS, LANE), F32),
                   jax.ShapeDtypeStruct((dbsz, N_HEADS, LANE), jnp.int32)],
        compiler_params=_params(("parallel",)),
        name="nsa_sample_cmp",
    )(hbuf, *[w[k] for k in names], q8, tcs, ovs)


def _nsa_sample_sel_kernel(idx_ref, pt_ref, *refs, qblk, n_win):
    nblk = KV_HEADS * N_SEL_BLOCKS
    blocks = refs[:nblk]
    (q_ref, slcn_ref, winn_ref, state_ref, gs_ref, oc_ref, tnear_ref, b0_ref, tws_ref,
     o_ref, wout_ref) = refs[nblk:]
    b = pl.program_id(0)
    q8f = q_ref[...]
    q8 = q8f.astype(BF16)
    hrow = lax.broadcasted_iota(jnp.int32, (N_HEADS, LANE), 0)
    b0 = b0_ref[:, 0:1]

    slcn = slcn_ref[0:1, :]
    s_new = jnp.sum(q8f * slcn[:, 0:LANE], axis=1, keepdims=True) + b0
    halves = PAGE // SEL_LEN
    lane = lax.broadcasted_iota(jnp.int32, (N_HEADS, PAGE), 1)
    o_sel = []
    for g in range(KV_HEADS):
        ss = []
        for r in range(N_SEL_BLOCKS):
            blk = idx_ref[b, g * N_SEL_BLOCKS + r]
            s = _dot(q8, blocks[g * N_SEL_BLOCKS + r][0].astype(BF16))
            near = jnp.where(blk // halves == qblk // halves - 1, tnear_ref[...], 0.0)
            ok = (blk < qblk) & (lane // SEL_LEN == blk % halves)
            ss.append(jnp.where(ok, s + near, NEG))
        s = jnp.concatenate(ss, axis=1)
        p, pn, l = _row_softmax_parts(s, s_new)
        o = pn * slcn[:, LANE:2 * LANE]
        for r in range(N_SEL_BLOCKS):
            pr = p[:, r * PAGE:(r + 1) * PAGE].astype(BF16)
            o = o + _dot_nt(pr, blocks[g * N_SEL_BLOCKS + r][1].astype(BF16))
        o_sel.append(o / l)
    o_s = jnp.where(hrow // HPG == 0, o_sel[0], o_sel[1])

    winn = winn_ref[0:1, :]
    sw = _dot(q8, state_ref[0].astype(BF16)) + tws_ref[...]
    sw_new = jnp.sum(q8f * winn[:, 0:LANE], axis=1, keepdims=True) + b0
    p, pn, l = _row_softmax_parts(sw, sw_new)
    o_w = (_dot_nt(p.astype(BF16), state_ref[1].astype(BF16)) + pn * winn[:, LANE:2 * LANE]) / l

    o_ref[...] = gs_ref[0] * oc_ref[...] + gs_ref[1] * o_s + gs_ref[2] * o_w
    wlane = lax.broadcasted_iota(jnp.int32, (LANE, n_win), 1)
    for kv in range(2):
        newcol = jnp.broadcast_to(winn[:, kv * LANE:(kv + 1) * LANE], (LANE, LANE)).T
        newcol = jnp.concatenate([newcol] * (n_win // LANE), axis=1)
        wout_ref[kv] = jnp.where(wlane == n_win - 1, newcol, pltpu.roll(state_ref[kv], n_win - 1, 1))


def _nsa_sample_sel(idx, page_table, slc_cache_t, q8, slc_new, win_new, state, gs, oc, tnear, b0, tws, *, qblk):
    dbsz, n_pages = page_table.shape
    n_win = state.shape[-1]
    halves = PAGE // SEL_LEN

    def blk_spec(k):
        def imap(b, idx_ref, pt_ref):
            j = idx_ref[b, k]
            return (pt_ref[b, jnp.minimum(j // halves, n_pages - 1)], 0, 0, 0)
        return pl.BlockSpec((None, 2, LANE, PAGE), imap)

    def per_b(shape):
        nd = len(shape)
        return pl.BlockSpec((None,) + tuple(shape[1:]), lambda b, i, p: (b,) + (0,) * (nd - 1))

    def const(shape):
        nd = len(shape)
        return pl.BlockSpec(tuple(shape), lambda b, i, p: (0,) * nd)

    nblk = KV_HEADS * N_SEL_BLOCKS
    grid_spec = pltpu.PrefetchScalarGridSpec(
        num_scalar_prefetch=2, grid=(dbsz,),
        in_specs=[blk_spec(k) for k in range(nblk)]
        + [per_b(q8.shape), per_b(slc_new.shape), per_b(win_new.shape), per_b(state.shape), per_b(gs.shape),
           per_b(oc.shape), const(tnear.shape), const(b0.shape), const(tws.shape)],
        out_specs=[pl.BlockSpec((N_HEADS, LANE), lambda b, i, p: (b, 0)),
                   pl.BlockSpec((None, 2, LANE, n_win), lambda b, i, p: (b, 0, 0, 0))])
    return pl.pallas_call(
        functools.partial(_nsa_sample_sel_kernel, qblk=qblk, n_win=n_win),
        grid_spec=grid_spec,
        out_shape=[jax.ShapeDtypeStruct((dbsz * N_HEADS, LANE), F32),
                   jax.ShapeDtypeStruct((dbsz, 2, LANE, n_win), F32)],
        compiler_params=_params(("arbitrary",)),
        name="nsa_sample_sel",
    )(idx, page_table, *([slc_cache_t] * nblk), q8, slc_new, win_new, state, gs, oc, tnear, b0, tws)


def _merge_kernel(x_ref, o_ref, u_ref, vn_ref, ga_ref, gb_ref, ws_ref, bs_ref, wbn_ref, wbg_ref, wout_ref,
                  lng_ref, x1_ref, *, tm):
    chunk = ws_ref.shape[1]
    keep = (lax.broadcasted_iota(jnp.int32, (chunk, chunk), 0)
            >= lax.broadcasted_iota(jnp.int32, (chunk, chunk), 1))
    vb = vn_ref[...].astype(BF16)
    cols = []
    for gg in range(GM_GROUPS):
        wt = jnp.where(keep, ws_ref[gg], 0.0).astype(BF16)
        rows = [_dot(wt, vb[c * chunk:(c + 1) * chunk, gg * LANE:(gg + 1) * LANE]) for c in range(tm // chunk)]
        cols.append(jnp.concatenate(rows, axis=0) if len(rows) > 1 else rows[0])
    s = jnp.concatenate(cols, axis=1) + jnp.concatenate([bs_ref[...]] * (tm // chunk), axis=0)
    o_gm = (u_ref[...].astype(F32) * s).astype(BF16)
    if len(wbn_ref.shape) == 2:
        a = _dot(o_ref[...].astype(BF16), wbn_ref[...])
    else:
        a = jnp.zeros((tm, D_MODEL), F32)
        for hh in range(N_HEADS):
            a += _dot(o_ref[pl.ds(hh, tm, stride=N_HEADS), :].astype(BF16), wbn_ref[hh])
    bm = _dot(o_gm, wbg_ref[...])
    mixed = (ga_ref[...].astype(F32) * a + gb_ref[...].astype(F32) * bm).astype(BF16)
    x1_ref[...] = x_ref[...] + _rms(_dot(mixed, wout_ref[...]), lng_ref[...])


def _merge(x, o_nsa, u, vn, ga, gb, ws, bs, wbn, w, *, tm, name):
    n = x.shape[0]
    row = lambda i: (i, 0)
    c2 = lambda i: (0, 0)
    c3 = lambda i: (0, 0, 0)
    ws_spec = pl.BlockSpec(ws.shape, c3)
    o_rows = o_nsa.shape[0] // n
    return pl.pallas_call(
        functools.partial(_merge_kernel, tm=tm),
        grid=(n // tm,),
        in_specs=[pl.BlockSpec((tm, D_MODEL), row), pl.BlockSpec((tm * o_rows, o_nsa.shape[1]), row),
                  pl.BlockSpec((tm, GM_DIM), row), pl.BlockSpec((tm, GM_DIM), row),
                  pl.BlockSpec((tm, D_MODEL), row), pl.BlockSpec((tm, D_MODEL), row),
                  ws_spec, pl.BlockSpec(bs.shape, c2),
                  pl.BlockSpec(wbn.shape, c2 if wbn.ndim == 2 else c3), pl.BlockSpec(w["wbg"].shape, c2),
                  pl.BlockSpec(w["wout"].shape, c2), pl.BlockSpec((1, D_MODEL), c2)],
        out_specs=pl.BlockSpec((tm, D_MODEL), row),
        out_shape=jax.ShapeDtypeStruct((n, D_MODEL), F32),
        compiler_params=_params(("parallel",)),
        name=name,
    )(x, o_nsa, u, vn, ga, gb, ws, bs, wbn, w["wbg"], w["wout"], w["ln_mix_post"])


def _ffn_kernel(x_ref, p_ref, lnf_ref, wg_ref, wu_ref, wd_ref, lnp_ref, wple_ref, wpg_ref, lne_ref, y_ref,
                h2_ref, acc_ref):
    j = pl.program_id(1)

    @pl.when(j == 0)
    def _():
        h2_ref[...] = _rms(x_ref[...], lnf_ref[...]).astype(BF16)
        acc_ref[...] = jnp.zeros(acc_ref.shape, F32)

    h2 = h2_ref[...]
    act = (jax.nn.silu(_dot(h2, wg_ref[...])) * _dot(h2, wu_ref[...])).astype(BF16)
    acc_ref[...] += _dot(act, wd_ref[...])

    @pl.when(j == pl.num_programs(1) - 1)
    def _():
        x2 = x_ref[...] + _rms(acc_ref[...], lnp_ref[...])
        e = _dot(p_ref[...].astype(BF16), wple_ref[...])
        gp = jax.nn.sigmoid(_dot(x2.astype(BF16), wpg_ref[...]))
        y_ref[...] = x2 + _rms(gp * e, lne_ref[...])


def _ffn(x1, ple, w, *, tm, name):
    n = x1.shape[0]
    row = lambda i, j: (i, 0)
    c2 = lambda i, j: (0, 0)
    return pl.pallas_call(
        _ffn_kernel,
        grid=(n // tm, D_FF // FF_CHUNK),
        in_specs=[pl.BlockSpec((tm, D_MODEL), row), pl.BlockSpec((tm, PLE_DIM), row),
                  pl.BlockSpec((1, D_MODEL), c2),
                  pl.BlockSpec((D_MODEL, FF_CHUNK), lambda i, j: (0, j)),
                  pl.BlockSpec((D_MODEL, FF_CHUNK), lambda i, j: (0, j)),
                  pl.BlockSpec((FF_CHUNK, D_MODEL), lambda i, j: (j, 0)),
                  pl.BlockSpec((1, D_MODEL), c2), pl.BlockSpec(w["wple"].shape, c2),
                  pl.BlockSpec(w["wpg"].shape, c2), pl.BlockSpec((1, D_MODEL), c2)],
        out_specs=pl.BlockSpec((tm, D_MODEL), row),
        out_shape=jax.ShapeDtypeStruct((n, D_MODEL), F32),
        scratch_shapes=[pltpu.VMEM((tm, D_MODEL), BF16), pltpu.VMEM((tm, D_MODEL), F32)],
        compiler_params=_params(("parallel", "arbitrary")),
        name=name,
    )(x1, ple, w["ln_ffn_pre"], w["wfg"], w["wfu"], w["wfd"], w["ln_ffn_post"], w["wple"], w["wpg"],
      w["ln_ple_post"])


def _prep_weights(i, ln_mix_pre, w_in, cmp_pe_k, cmp_w1_k, cmp_w2_k, cmp_pe_v, cmp_w1_v, cmp_w2_v, gm_ln_g,
                  gm_ln_b, w_branch_nsa, w_branch_gm, w_out, ln_mix_post, ln_ffn_pre, w_ffn_gate, w_ffn_up,
                  w_ffn_down, ln_ffn_post, w_ple, w_ple_gate, ln_ple_post):
    q_dim = N_HEADS * HEAD_DIM
    kv3 = 6 * KV_HEADS * HEAD_DIM
    n_gate = 3 * N_HEADS
    win = w_in[i]
    o = 0
    wq = win[:, o:o + q_dim]; o += q_dim
    wkv = win[:, o:o + kv3]; o += kv3
    wg = win[:, o:o + n_gate]; o += n_gate
    wuv = win[:, o:o + 2 * GM_DIM]; o += 2 * GM_DIM
    wmg = win[:, o:o + 2 * D_MODEL]
    wq4 = wq.reshape(D_MODEL, N_HEADS, 1, HEAD_DIM)
    half = (np.arange(N_HEADS) // HPG)[None, :, None, None] == np.arange(KV_HEADS)[None, None, :, None]
    wq_pad = jnp.where(half, wq4, 0.0).reshape(D_MODEL, N_HEADS * LANE)
    wg_pad = jnp.pad(wg, ((0, 0), (0, LANE - n_gate)))

    def blockdiag(w1):
        w1s = w1.reshape(CMP_LEN // CMP_STRIDE, CMP_STRIDE, HEAD_DIM, CMP_HID).transpose(1, 2, 0, 3)
        w1s = w1s.reshape(CMP_STRIDE, HEAD_DIM, 2 * CMP_HID)
        z = jnp.zeros_like(w1s)
        bd = jnp.concatenate([jnp.concatenate([w1s, z], axis=2), jnp.concatenate([z, w1s], axis=2)], axis=1)
        return bd.reshape(CMP_STRIDE // 2, 2 * 2 * HEAD_DIM, 4 * CMP_HID)

    wbn4 = w_branch_nsa[i].reshape(N_HEADS, 1, HEAD_DIM, D_MODEL)
    wbn_heads = jnp.where(half.reshape(N_HEADS, KV_HEADS, 1, 1), wbn4, 0.0).reshape(N_HEADS, LANE, D_MODEL)

    zk = jnp.zeros_like(cmp_w2_k[i])
    zv = jnp.zeros_like(cmp_w2_v[i])
    b = lambda a: a.astype(BF16)
    r = lambda a: a[i][None, :].astype(F32)
    return dict(
        ln_mix_pre=r(ln_mix_pre), wq=b(wq_pad), wkv=b(wkv), wg=b(wg_pad), wuv=b(wuv), wmg=b(wmg),
        gm_ln_g=r(gm_ln_g), gm_ln_b=r(gm_ln_b),
        wbk=b(blockdiag(cmp_w1_k[i])), wbv=b(blockdiag(cmp_w1_v[i])),
        pek=jnp.broadcast_to(cmp_pe_k[i].reshape(1, -1), (8, CMP_LEN * HEAD_DIM)).astype(F32), w1k=b(cmp_w1_k[i]),
        pev=jnp.broadcast_to(cmp_pe_v[i].reshape(1, -1), (8, CMP_LEN * HEAD_DIM)).astype(F32), w1v=b(cmp_w1_v[i]),
        w2ka=b(jnp.concatenate([cmp_w2_k[i], zk], axis=1)), w2kb=b(jnp.concatenate([zk, cmp_w2_k[i]], axis=1)),
        w2va=b(jnp.concatenate([cmp_w2_v[i], zv], axis=1)), w2vb=b(jnp.concatenate([zv, cmp_w2_v[i]], axis=1)),
        w2vt=b(cmp_w2_v[i].T),
        wbn=b(w_branch_nsa[i]), wbn_heads=b(wbn_heads), wbg=b(w_branch_gm[i]), wout=b(w_out[i]), ln_mix_post=r(ln_mix_post),
        ln_ffn_pre=r(ln_ffn_pre), wfg=b(w_ffn_gate[i]), wfu=b(w_ffn_up[i]), wfd=b(w_ffn_down[i]),
        ln_ffn_post=r(ln_ffn_post), wple=b(w_ple[i]), wpg=b(w_ple_gate[i]), ln_ple_post=r(ln_ple_post),
    )


def _gate_perm():
    return np.array([hg * 3 + br for br in range(3) for hg in range(N_HEADS)])


def _prompt_layer(x, ple, w, rel_bias, gm_ws, gm_bs):
    bsz, seq, _ = x.shape
    assert seq % SEL_TILE == 0 and seq // SEL_LEN <= NJ and seq >= WINDOW
    n = bsz * seq
    tm = 256
    xf = x.reshape(n, D_MODEL)
    wp = dict(w)
    perm = _gate_perm()
    wp["wg"] = jnp.concatenate([w["wg"][:, perm], w["wg"][:, len(perm):]], axis=1)
    (qhm, cmp, cmpt, slct, wint, kaug, vts, kwin, vtw, gt, u, vn, ga, gb) = _inproj(
        xf, wp, prompt=True, seq=seq, tm=tm)
    kc, vct = _compress_prompt(cmp, w, bsz=bsz, seq=seq)

    o_nsa = _nsa_prompt(qhm, kc, vct, kaug, vts, kwin, vtw, gt, _bias_tables(rel_bias), bsz=bsz, seq=seq)

    bs_tile = jnp.repeat(gm_bs.T, LANE, axis=1).astype(F32)
    x1 = _merge(xf, o_nsa, u, vn, ga, gb, gm_ws.astype(F32), bs_tile, w["wbn"], w, tm=512, name="merge_prompt")
    y = _ffn(x1, ple.reshape(n, PLE_DIM), w, tm=512, name="ffn_prompt")

    def rows_form(at):
        return at.reshape(bsz, 2, KV_HEADS, HEAD_DIM, at.shape[-1]).transpose(0, 4, 1, 2, 3)

    return (y.reshape(bsz, seq, D_MODEL), rows_form(cmpt), rows_form(slct),
            rows_form(wint[:, :, seq - min(WINDOW, seq):]))


def _sample_layer(x, ple, w, rel_bias, gm_ws, gm_bs, cache_cmp, cache_slc, win_buf, page_table):
    dbsz, nq, _ = x.shape
    assert nq == 1
    n_pool = cache_cmp.shape[0]
    n_pages = page_table.shape[1]
    past = n_pages * PAGE
    n_win = win_buf.shape[1]
    assert past % GM_CHUNK == 0 and n_win == WINDOW and past >= WINDOW and dbsz % 8 == 0
    xf = x.reshape(dbsz, D_MODEL)
    q, cmp, slc, win, gate, u, vn, ga, gb = _inproj(xf, w, prompt=False, seq=1, tm=dbsz)

    pps = min(16, n_pages)
    hbuf = _compress_paged(_pages_t(cache_cmp), page_table, w, pps=pps)
    n_sub = past // CMP_STRIDE
    qblk = past // SEL_LEN
    n_sel_pad = -(-(qblk + 1) // LANE) * LANE
    sh = _shifted_bias(rel_bias)
    i = np.arange(n_sub)
    d_c = past - (CMP_STRIDE * i + CMP_LEN - 1)
    tcs = _sample_table(sh, d_c, i < n_sub - 1)
    ovs = jnp.asarray(_overlap_t(n_sub, n_sel_pad).T)
    q8 = q.reshape(dbsz, N_HEADS, LANE)
    oc, idx = _nsa_sample_cmp(hbuf, q8, w, tcs, ovs, qblk=qblk)
    idx2 = idx[:, 0:KV_HEADS, 0:N_SEL_BLOCKS].reshape(dbsz, KV_HEADS * N_SEL_BLOCKS)

    l = np.arange(PAGE)
    tnear = _sample_table(sh, PAGE - l, l >= 0)
    b0 = jnp.broadcast_to(sh[0][:, None] * LOG2E, (N_HEADS, LANE)).astype(F32)
    kpos = np.arange(n_win)
    tws = _sample_table(sh, n_win - kpos, kpos >= 0)
    gs = gate[:, 0:3 * N_HEADS].reshape(dbsz, N_HEADS, 3).transpose(0, 2, 1)
    gs = jnp.broadcast_to(gs[..., None], (dbsz, 3, N_HEADS, LANE))
    o8, win_out = _nsa_sample_sel(
        idx2, page_table, _pages_t(cache_slc), q8,
        jnp.broadcast_to(slc[:, None, :], (dbsz, 8, 256)), jnp.broadcast_to(win[:, None, :], (dbsz, 8, 256)),
        _pages_t(win_buf), gs, oc, tnear, b0, tws, qblk=qblk)
    win_out = win_out.reshape(dbsz, 2, KV_HEADS, HEAD_DIM, n_win).transpose(0, 4, 1, 2, 3)

    ws_diag = gm_ws[:, 0, 0][:, None, None] * jnp.eye(dbsz, dtype=F32)[None]
    bs_tile = jnp.broadcast_to(jnp.repeat(gm_bs[:, 0], LANE)[None, :], (dbsz, GM_GROUPS * LANE)).astype(F32)
    x1 = _merge(xf, o8, u, vn, ga, gb, ws_diag, bs_tile, w["wbn_heads"], w, tm=dbsz, name="merge_sample")
    y = _ffn(x1, ple.reshape(dbsz, PLE_DIM), w, tm=dbsz, name="ffn_sample")
    kvshape = (dbsz, 1, 2, KV_HEADS, HEAD_DIM)
    return (y.reshape(dbsz, 1, D_MODEL), cmp.reshape(kvshape), slc.reshape(kvshape),
            win_out, vn.reshape(dbsz, 1, GM_DIM))


def kernel(x_prompt, x_sample, cache_cmp_kv, cache_slc_kv, state_win_kv, page_table, p_prompt, p_sample, rel_bias,
           ln_mix_pre, w_in, cmp_pe_k, cmp_w1_k, cmp_w2_k, cmp_pe_v, cmp_w1_v, cmp_w2_v, gm_ln_g, gm_ln_b, gm_ws,
           gm_bs, w_branch_nsa, w_branch_gm, w_out, ln_mix_post, ln_ffn_pre, w_ffn_gate, w_ffn_up, w_ffn_down,
           ln_ffn_post, w_ple, w_ple_gate, ln_ple_post):
    depth = w_in.shape[0]
    xp, xs = x_prompt, x_sample
    outs = [[] for _ in range(7)]
    for i in range(depth):
        w = _prep_weights(i, ln_mix_pre, w_in, cmp_pe_k, cmp_w1_k, cmp_w2_k, cmp_pe_v, cmp_w1_v, cmp_w2_v, gm_ln_g,
                          gm_ln_b, w_branch_nsa, w_branch_gm, w_out, ln_mix_post, ln_ffn_pre, w_ffn_gate, w_ffn_up,
                          w_ffn_down, ln_ffn_post, w_ple, w_ple_gate, ln_ple_post)
        xp, c_p, s_p, w_p = _prompt_layer(xp, p_prompt[i], w, rel_bias, gm_ws[i], gm_bs[i])
        xs, c_s, s_s, w_s, v_s = _sample_layer(xs, p_sample[i], w, rel_bias, gm_ws[i], gm_bs[i], cache_cmp_kv[i],
                                               cache_slc_kv[i], state_win_kv[i], page_table)
        for lst, val in zip(outs, (c_p, s_p, w_p, c_s, s_s, w_s, v_s)):
            lst.append(val)
    return (xp, xs) + tuple(jnp.stack(o) for o in outs)
```

```python
import functools
import math

import jax
import jax.numpy as jnp
import numpy as np
from jax import lax
from jax.experimental import pallas as pl
from jax.experimental.pallas import tpu as pltpu

F32 = jnp.float32
BF16 = jnp.bfloat16

D_MODEL = 1024
N_HEADS = 8
KV_HEADS = 2
HPG = N_HEADS // KV_HEADS
HEAD_DIM = 64
CMP_LEN = 32
CMP_STRIDE = 16
CMP_HID = 2 * HEAD_DIM
SEL_LEN = 64
N_SEL_BLOCKS = 16
WINDOW = 512
Q_BLOCK = 128
GM_GROUPS = 4
GM_CHUNK = 128
GM_DIM = D_MODEL // 2
D_FF = -(-8 * D_MODEL // (3 * 256)) * 256
PLE_DIM = 256
N_BUCKETS = 32
MAX_DISTANCE = 128
EPS = 1e-6
NEG = -1e30
LOG2E = 1.4426950408889634
FORCE = 1e6
PAGE = 128

LANE = 128
NJ = 128
SEL_TILE = 256
FF_CHUNK = D_FF // 2
VMEM_LIMIT = 56 * 1024 * 1024


def _dot(a, b):
    return jnp.dot(a, b, preferred_element_type=F32)


def _dot_nt(a, b):
    return lax.dot_general(a, b, (((1,), (1,)), ((), ())), preferred_element_type=F32)


def _dot_f32(a, b):
    return jnp.dot(a, b, preferred_element_type=F32, precision=lax.Precision.HIGHEST)


def _rms(x, g):
    return x * lax.rsqrt(jnp.mean(x * x, axis=-1, keepdims=True) + EPS) * g


def _params(sem):
    return pltpu.CompilerParams(dimension_semantics=sem, vmem_limit_bytes=VMEM_LIMIT)


def _t5_bucket_np(d):
    d = np.maximum(d, 0)
    max_exact = N_BUCKETS // 2
    ratio = (np.log(np.maximum(d, 1).astype(np.float32) / np.float32(max_exact))
             / np.float32(math.log(MAX_DISTANCE / max_exact)))
    large = np.minimum(max_exact + (ratio * np.float32(N_BUCKETS - max_exact)).astype(np.int32), N_BUCKETS - 1)
    return np.where(d < max_exact, d, large)


_BUCKET = _t5_bucket_np(np.arange(MAX_DISTANCE + 1))
assert _BUCKET[MAX_DISTANCE] == N_BUCKETS - 1


def _shifted_bias(rel_bias):
    return rel_bias[_BUCKET] - rel_bias[N_BUCKETS - 1][None, :]


def _sample_table(sh, dist, valid):
    t = sh[np.minimum(np.maximum(dist, 0), MAX_DISTANCE)] * LOG2E
    return jnp.where(valid[:, None], t, NEG).T.astype(F32)


_BUCKET_START = [int(np.argmax(_BUCKET >= k)) for k in range(N_BUCKETS)]
_BAND_ROWS = 3 * (Q_BLOCK // CMP_STRIDE)


def _bias_tables_kernel(rb_ref, tc_ref, ts_ref, tw_ref):
    def fill(store, rows, chunk, c0, stride, dmax):
        def body(c, carry):
            r0 = pl.multiple_of(c * chunk, chunk)
            r = r0 + lax.broadcasted_iota(jnp.int32, (chunk, Q_BLOCK), 0)
            d = c0 + lax.broadcasted_iota(jnp.int32, (chunk, Q_BLOCK), 1) - stride * r
            ok = (d >= 0) & (d <= dmax)
            dc = jnp.minimum(d, MAX_DISTANCE)
            for hd in range(N_HEADS):
                v = jnp.full((chunk, Q_BLOCK), rb_ref[0, hd], F32)
                for k in range(1, N_BUCKETS):
                    v = jnp.where(dc >= _BUCKET_START[k], rb_ref[k, hd], v)
                v = (v - rb_ref[N_BUCKETS - 1, hd]) * LOG2E
                store(r0, chunk, hd, jnp.where(ok, v, NEG))
            return carry
        lax.fori_loop(0, rows // chunk, body, 0)

    def lanes(hd):
        return slice(hd * Q_BLOCK, (hd + 1) * Q_BLOCK)

    def st_c(r0, n, hd, v):
        tc_ref[pl.ds(r0, n), lanes(hd)] = v

    def st_w(r0, n, hd, v):
        tw_ref[pl.ds(r0, n), lanes(hd)] = v

    big = 1 << 30
    fill(st_c, _BAND_ROWS, 8, 2 * Q_BLOCK - (CMP_LEN - 1), CMP_STRIDE, big)
    for par in range(SEL_TILE // Q_BLOCK):
        def st_s(r0, n, hd, v, par=par):
            ts_ref[par, pl.ds(r0, n), lanes(hd)] = v
        fill(st_s, 2 * SEL_TILE, 64, SEL_TILE + Q_BLOCK * par, 1, big)
    fill(st_w, WINDOW + Q_BLOCK, 64, WINDOW, 1, WINDOW)


def _bias_tables(rel_bias):
    nl = N_HEADS * Q_BLOCK
    return pl.pallas_call(
        _bias_tables_kernel,
        in_specs=[pl.BlockSpec(memory_space=pltpu.SMEM)],
        out_shape=[jax.ShapeDtypeStruct((_BAND_ROWS, nl), F32),
                   jax.ShapeDtypeStruct((SEL_TILE // Q_BLOCK, 2 * SEL_TILE, nl), F32),
                   jax.ShapeDtypeStruct((WINDOW + Q_BLOCK, nl), F32)],
        compiler_params=pltpu.CompilerParams(vmem_limit_bytes=VMEM_LIMIT),
        name="bias_tables",
    )(rel_bias.astype(F32))


def _overlap_t(n_cmp, n_sel_pad):
    ci = np.arange(n_cmp)[None, :] * CMP_STRIDE
    sj = np.arange(n_sel_pad)[:, None] * SEL_LEN
    return ((ci <= sj + SEL_LEN - 1) & (ci + CMP_LEN - 1 >= sj)).astype(np.float32)


def _overlap_offsets():
    per = SEL_LEN // CMP_STRIDE
    ov = _overlap_t(16 * per, 16)
    jj, ii = np.nonzero(ov)
    offs = sorted({int(i - per * j) for j, i in zip(jj, ii)})
    band = np.isin(np.arange(16 * per)[None, :] - per * np.arange(16)[:, None], offs)
    assert (band == (ov > 0)).all() and min(offs) >= -8
    return offs


_OVERLAP_OFFSETS = _overlap_offsets()


def _inproj_kernel(x_ref, lng_ref, wq_ref, wkv_ref, wg_ref, wuv_ref, wmg_ref, glg_ref, glb_ref, *outs,
                   prompt, seq, tm):
    h = _rms(x_ref[...], lng_ref[...]).astype(BF16)
    q = _dot(h, wq_ref[...]) * (HEAD_DIM ** -0.5 * LOG2E)
    kv = _dot(h, wkv_ref[...])
    gate = jax.nn.sigmoid(_dot(h, wg_ref[...]))
    uv = jax.nn.gelu(_dot(h, wuv_ref[...]))
    u = uv[:, :GM_DIM]
    v = uv[:, GM_DIM:]
    mu = jnp.mean(v, axis=-1, keepdims=True)
    var = jnp.mean(jnp.square(v - mu), axis=-1, keepdims=True)
    vn = (v - mu) * lax.rsqrt(var + EPS) * glg_ref[...] + glb_ref[...]
    mg = jax.nn.sigmoid(_dot(h, wmg_ref[...]))
    cmp, slc, win = kv[:, 0:256], kv[:, 256:512], kv[:, 512:768]
    if prompt:
        (q_ref, cmp_ref, cmpt_ref, slct_ref, wint_ref, kaug_ref, vts_ref, kwin_ref, vtw_ref, gt_ref,
         u_ref, vn_ref, ga_ref, gb_ref) = outs
        for hh in range(N_HEADS):
            q_ref[hh] = q[:, hh * LANE:(hh + 1) * LANE].astype(BF16)
        pos = (pl.program_id(0) * tm + lax.broadcasted_iota(jnp.int32, (tm, NJ), 0)) % seq
        onehot = (lax.broadcasted_iota(jnp.int32, (tm, NJ), 1) == pos // SEL_LEN)
        kaug_ref[:, 0:NJ] = onehot.astype(BF16)
        kaug_ref[:, NJ:NJ + LANE] = slc[:, 0:LANE].astype(BF16)
        kwin_ref[...] = win[:, 0:LANE].astype(BF16)
        slc_t = slc.T
        win_t = win.T
        for g in range(KV_HEADS):
            r0 = LANE + HEAD_DIM * g
            for c in range(tm // SEL_TILE):
                vts_ref[g, c] = slc_t[r0:r0 + HEAD_DIM, c * SEL_TILE:(c + 1) * SEL_TILE].astype(BF16)
            for c in range(tm // LANE):
                vtw_ref[g, c] = win_t[r0:r0 + HEAD_DIM, c * LANE:(c + 1) * LANE].astype(BF16)
        gt_ref[...] = gate.T[0:32, :]
        cmpt_ref[...] = cmp.T
        slct_ref[...] = slc_t
        wint_ref[...] = win_t
    else:
        q_ref, cmp_ref, slc_ref, win_ref, gate_ref, u_ref, vn_ref, ga_ref, gb_ref = outs
        q_ref[...] = q
        gate_ref[...] = gate
        slc_ref[...] = slc
        win_ref[...] = win
    cmp_ref[...] = cmp
    u_ref[...] = u.astype(u_ref.dtype)
    vn_ref[...] = vn.astype(vn_ref.dtype)
    ga_ref[...] = mg[:, :D_MODEL].astype(ga_ref.dtype)
    gb_ref[...] = mg[:, D_MODEL:].astype(gb_ref.dtype)


def _inproj(x, w, *, prompt, seq, tm):
    n = x.shape[0]
    row = lambda i: (i, 0)
    const = lambda i: (0, 0)
    in_specs = [pl.BlockSpec((tm, D_MODEL), row), pl.BlockSpec((1, D_MODEL), const),
                pl.BlockSpec(w["wq"].shape, const), pl.BlockSpec(w["wkv"].shape, const),
                pl.BlockSpec(w["wg"].shape, const), pl.BlockSpec(w["wuv"].shape, const),
                pl.BlockSpec(w["wmg"].shape, const), pl.BlockSpec((1, GM_DIM), const),
                pl.BlockSpec((1, GM_DIM), const)]
    kv_shapes = [jax.ShapeDtypeStruct((n, 256), F32)] * 3
    kv_specs = [pl.BlockSpec((tm, 256), row)] * 3
    if prompt:
        act = BF16
        tiles = seq // tm
        kvt_shapes = [jax.ShapeDtypeStruct((n // seq, 256, seq), F32)] * 3
        kvt_specs = [pl.BlockSpec((None, 256, tm), lambda i: (i // tiles, 0, i % tiles))] * 3
        out_shape = ([jax.ShapeDtypeStruct((N_HEADS, n, LANE), BF16)] + kv_shapes[:1] + kvt_shapes + [
            jax.ShapeDtypeStruct((n, NJ + LANE), BF16),
            jax.ShapeDtypeStruct((KV_HEADS, n // SEL_TILE, HEAD_DIM, SEL_TILE), BF16),
            jax.ShapeDtypeStruct((n, LANE), BF16),
            jax.ShapeDtypeStruct((KV_HEADS, n // LANE, HEAD_DIM, LANE), BF16),
            jax.ShapeDtypeStruct((32, n), F32)])
        out_specs = ([pl.BlockSpec((N_HEADS, tm, LANE), lambda i: (0, i, 0))] + kv_specs[:1] + kvt_specs + [
            pl.BlockSpec((tm, NJ + LANE), row),
            pl.BlockSpec((KV_HEADS, tm // SEL_TILE, HEAD_DIM, SEL_TILE), lambda i: (0, i, 0, 0)),
            pl.BlockSpec((tm, LANE), row),
            pl.BlockSpec((KV_HEADS, tm // LANE, HEAD_DIM, LANE), lambda i: (0, i, 0, 0)),
            pl.BlockSpec((32, tm), lambda i: (0, i))])
    else:
        act = F32
        out_shape = ([jax.ShapeDtypeStruct((n, N_HEADS * LANE), F32)] + kv_shapes
                     + [jax.ShapeDtypeStruct((n, LANE), F32)])
        out_specs = [pl.BlockSpec((tm, N_HEADS * LANE), row)] + kv_specs + [pl.BlockSpec((tm, LANE), row)]
    out_shape += [jax.ShapeDtypeStruct((n, GM_DIM), act)] * 2 + [jax.ShapeDtypeStruct((n, D_MODEL), act)] * 2
    out_specs += [pl.BlockSpec((tm, GM_DIM), row)] * 2 + [pl.BlockSpec((tm, D_MODEL), row)] * 2
    return pl.pallas_call(
        functools.partial(_inproj_kernel, prompt=prompt, seq=seq, tm=tm),
        grid=(n // tm,), in_specs=in_specs, out_specs=out_specs, out_shape=out_shape,
        compiler_params=_params(("parallel",)),
        name="inproj_prompt" if prompt else "inproj_sample",
    )(x, w["ln_mix_pre"], w["wq"], w["wkv"], w["wg"], w["wuv"], w["wmg"], w["gm_ln_g"], w["gm_ln_b"])


def _row_pair(x_ref, r, n_sub):
    return jnp.concatenate([x_ref[pl.ds(r, n_sub, stride=CMP_STRIDE), :],
                            x_ref[pl.ds(r + 1, n_sub, stride=CMP_STRIDE), :]], axis=1).astype(BF16)


def _compress_finish(hk, hv, pek_ref, w1k_ref, pev_ref, w1v_ref, w2ka_ref, w2kb_ref, n_sub):
    pwk = _dot(pek_ref[...].astype(BF16), w1k_ref[...])[0:1, :]
    pwv = _dot(pev_ref[...].astype(BF16), w1v_ref[...])[0:1, :]

    def act(hh, g, pw):
        a0 = hh[:, 256 * g:256 * g + CMP_HID]
        a1 = hh[:, 256 * g + CMP_HID:256 * g + 2 * CMP_HID]
        return jax.nn.silu(a0 + pltpu.roll(a1, n_sub - 1, 0) + pw).astype(BF16)

    return [act(hk, 0, pwk), act(hk, 1, pwk)], [act(hv, 0, pwv), act(hv, 1, pwv)]


def _compress_prompt_kernel(cmpk_ref, cmpv_ref, wbk_ref, wbv_ref, pek_ref, w1k_ref, pev_ref, w1v_ref,
                            w2ka_ref, w2kb_ref, w2vt_ref, kc_ref, vct_ref, *, n_sub):
    hk = jnp.zeros((n_sub, 4 * CMP_HID), F32)
    hv = jnp.zeros((n_sub, 4 * CMP_HID), F32)
    for r in range(0, CMP_STRIDE, 2):
        hk += _dot(_row_pair(cmpk_ref, r, n_sub), wbk_ref[r // 2])
        hv += _dot(_row_pair(cmpv_ref, r, n_sub), wbv_ref[r // 2])
    ak, av = _compress_finish(hk, hv, pek_ref, w1k_ref, pev_ref, w1v_ref, w2ka_ref, w2kb_ref, n_sub)
    kc_ref[...] = (_dot(ak[0], w2ka_ref[...]) + _dot(ak[1], w2kb_ref[...])).astype(BF16)
    for g in range(KV_HEADS):
        vct_ref[g] = _dot_nt(w2vt_ref[...], av[g]).astype(BF16)


def _compress_prompt(cmp, w, *, bsz, seq):
    n_sub = seq // CMP_STRIDE
    const2 = lambda b: (0, 0)
    const3 = lambda b: (0, 0, 0)
    return pl.pallas_call(
        functools.partial(_compress_prompt_kernel, n_sub=n_sub),
        grid=(bsz,),
        in_specs=[pl.BlockSpec((seq, LANE), lambda b: (b, 0)), pl.BlockSpec((seq, LANE), lambda b: (b, 1)),
                  pl.BlockSpec(w["wbk"].shape, const3), pl.BlockSpec(w["wbv"].shape, const3),
                  pl.BlockSpec(w["pek"].shape, const2), pl.BlockSpec(w["w1k"].shape, const2),
                  pl.BlockSpec(w["pev"].shape, const2), pl.BlockSpec(w["w1v"].shape, const2),
                  pl.BlockSpec(w["w2ka"].shape, const2), pl.BlockSpec(w["w2kb"].shape, const2),
                  pl.BlockSpec(w["w2vt"].shape, const2)],
        out_specs=[pl.BlockSpec((None, n_sub, LANE), lambda b: (b, 0, 0)),
                   pl.BlockSpec((None, KV_HEADS, HEAD_DIM, n_sub), lambda b: (b, 0, 0, 0))],
        out_shape=[jax.ShapeDtypeStruct((bsz, n_sub, LANE), BF16),
                   jax.ShapeDtypeStruct((bsz, KV_HEADS, HEAD_DIM, n_sub), BF16)],
        compiler_params=_params(("parallel",)),
        name="compress_prompt",
    )(cmp, cmp, w["wbk"], w["wbv"], w["pek"], w["w1k"], w["pev"], w["w1v"], w["w2ka"], w["w2kb"], w["w2vt"])


def _compress_paged_kernel(pt_ref, *refs, pps):
    kpages, vpages = refs[:pps], refs[pps:2 * pps]
    wbk_ref, wbv_ref, h_ref, xk_ref, xv_ref = refs[2 * pps:]
    sub_per_page = PAGE // CMP_STRIDE
    n = pps * sub_per_page
    for k in range(pps):
        xk_ref[k * PAGE:(k + 1) * PAGE, :] = kpages[k][...].T
        xv_ref[k * PAGE:(k + 1) * PAGE, :] = vpages[k][...].T
    hk = jnp.zeros((n, 4 * CMP_HID), F32)
    hv = jnp.zeros((n, 4 * CMP_HID), F32)
    for r in range(0, CMP_STRIDE, 2):
        hk += _dot(_row_pair(xk_ref, r, n), wbk_ref[r // 2])
        hv += _dot(_row_pair(xv_ref, r, n), wbv_ref[r // 2])
    h_ref[:, 0:4 * CMP_HID] = hk
    h_ref[:, 4 * CMP_HID:8 * CMP_HID] = hv


def _compress_paged(cache_t, page_table, w, *, pps):
    dbsz, n_pages = page_table.shape
    sub_per_page = PAGE // CMP_STRIDE
    n_sub = n_pages * sub_per_page

    def page_spec(k, c):
        return pl.BlockSpec((None, None, LANE, PAGE), lambda b, t, pt: (pt[b, t * pps + k], c, 0, 0))

    const3 = lambda b, t, pt: (0, 0, 0)
    grid_spec = pltpu.PrefetchScalarGridSpec(
        num_scalar_prefetch=1, grid=(dbsz, n_pages // pps),
        in_specs=[page_spec(k, c) for c in range(2) for k in range(pps)]
        + [pl.BlockSpec(w["wbk"].shape, const3), pl.BlockSpec(w["wbv"].shape, const3)],
        out_specs=pl.BlockSpec((None, pps * sub_per_page, 8 * CMP_HID), lambda b, t, pt: (b, t, 0)),
        scratch_shapes=[pltpu.VMEM((pps * PAGE, LANE), F32), pltpu.VMEM((pps * PAGE, LANE), F32)])
    return pl.pallas_call(
        functools.partial(_compress_paged_kernel, pps=pps),
        grid_spec=grid_spec,
        out_shape=jax.ShapeDtypeStruct((dbsz, n_sub, 8 * CMP_HID), F32),
        compiler_params=_params(("parallel", "arbitrary")),
        name="compress_paged",
    )(page_table, *([cache_t] * (2 * pps)), w["wbk"], w["wbv"])


def _pages_t(cache):
    n, npos = cache.shape[0], cache.shape[1]
    return cache.transpose(0, 2, 3, 4, 1).reshape(n, 2, KV_HEADS * HEAD_DIM, npos)


def _col_softmax(s):
    p, inv = _col_softmax_parts(s)
    return p * inv


def _col_softmax_parts(s):
    m = jnp.max(s, axis=0, keepdims=True)
    p = jnp.exp2(s - m)
    return p, jnp.where(m > 0.5 * NEG, 1.0 / jnp.sum(p, axis=0, keepdims=True), 0.0)


N_FORCED = 3


def _select_blocks(score, forced, jio, axis):
    selneg = jnp.where(forced, 0.0, NEG)
    score = jnp.where(forced, -3e38, score)
    picked = []
    for _ in range(N_SEL_BLOCKS - N_FORCED):
        mx = jnp.max(score, axis=axis, keepdims=True)
        jm = jnp.min(jnp.where(score == mx, jio, 1 << 20), axis=axis, keepdims=True)
        pick = jio == jm
        selneg = jnp.where(pick, 0.0, selneg)
        score = jnp.where(pick, -3e38, score)
        picked.append(jm)
    return selneg, picked


def _nsa_prompt_kernel(q_ref, kc_ref, vct_ref, kaug_ref, vts_ref, kwin_ref, vtw_ref, gt_ref,
                       tc_ref, ts_ref, tw_ref, o_ref,
                       sc_ref, ps_ref, qaug_ref, m_ref, l_ref, acc_ref, sa_ref, sb_ref, *, n_cmp):
    qb = pl.program_id(1)
    gl = HPG * Q_BLOCK
    nl = KV_HEADS * gl
    q2 = q_ref[...].reshape(nl, LANE)

    def group_dots(vt, p):
        return jnp.concatenate([_dot(vt(g), p[:, g * gl:(g + 1) * gl]) for g in range(KV_HEADS)], axis=1)

    pad = 2 * (Q_BLOCK // CMP_STRIDE)
    band = 3 * (Q_BLOCK // CMP_STRIDE)
    sc_ref[0:pad, :] = jnp.zeros((pad, nl), F32)
    sc_ref[pad:pad + n_cmp, :] = _dot_nt(kc_ref[...], q2)
    w0 = pl.multiple_of(qb * (Q_BLOCK // CMP_STRIDE), 8)
    sc_ref[pl.ds(w0, band), :] += tc_ref[...]
    sc = sc_ref[pad:pad + n_cmp, :]
    row = lax.broadcasted_iota(jnp.int32, (n_cmp, nl), 0)
    sc = jnp.where(row < (qb + 1) * (Q_BLOCK // CMP_STRIDE), sc, NEG)
    pc = _col_softmax(sc)
    o_c = group_dots(lambda g: vct_ref[g], pc.astype(BF16))

    n_wt = WINDOW // Q_BLOCK + 1
    tix = [jnp.maximum(qb - (n_wt - 1) + t, 0) for t in range(n_wt)]
    kw = jnp.concatenate([kwin_ref[pl.ds(pl.multiple_of(t * Q_BLOCK, Q_BLOCK), Q_BLOCK), :] for t in tix], axis=0)
    sw = _dot_nt(kw, q2) + tw_ref[...]
    wrow = lax.broadcasted_iota(jnp.int32, (n_wt * Q_BLOCK, nl), 0) + (qb - (n_wt - 1)) * Q_BLOCK
    pw, winv = _col_softmax_parts(jnp.where(wrow >= 0, sw, NEG))
    o_w = group_dots(lambda g: jnp.concatenate([vtw_ref[g, t] for t in tix], axis=1), pw.astype(BF16)) * winv

    per_sel = SEL_LEN // CMP_STRIDE
    n_sel = n_cmp // per_sel
    jio = lax.broadcasted_iota(jnp.int32, (NJ, Q_BLOCK), 0)
    qi = lax.broadcasted_iota(jnp.int32, (NJ, Q_BLOCK), 1)
    qblk = qb * (Q_BLOCK // SEL_LEN) + qi // SEL_LEN
    forced = (jio == 0) | (jio == qblk) | (jio == qblk - 1)
    for g in range(KV_HEADS):
        psum = pc[:, g * gl:g * gl + Q_BLOCK]
        for hh in range(1, HPG):
            psum = psum + pc[:, g * gl + hh * Q_BLOCK:g * gl + (hh + 1) * Q_BLOCK]
        ps_ref[g, 0:8, :] = jnp.zeros((8, Q_BLOCK), F32)
        ps_ref[g, 8:8 + n_cmp, :] = psum
        imp_t = ps_ref[g, pl.ds(8 + _OVERLAP_OFFSETS[0], n_sel, stride=per_sel), :]
        for off in _OVERLAP_OFFSETS[1:]:
            imp_t = imp_t + ps_ref[g, pl.ds(8 + off, n_sel, stride=per_sel), :]
        if n_sel < NJ:
            imp_t = jnp.concatenate([imp_t, jnp.zeros((NJ - n_sel, Q_BLOCK), F32)], axis=0)
        score = jnp.where(forced, FORCE, jnp.where(jio <= qblk, imp_t, -FORCE))
        selneg_t, _ = _select_blocks(score, forced, jio, 0)
        selneg = selneg_t.T.astype(BF16)
        for hh in range(HPG):
            qaug_ref[g * gl + hh * Q_BLOCK:g * gl + (hh + 1) * Q_BLOCK, 0:NJ] = selneg
    qaug_ref[:, NJ:NJ + LANE] = q2

    m_ref[...] = jnp.full((1, nl), -1e38, F32)
    l_ref[...] = jnp.zeros((1, nl), F32)
    acc_ref[...] = jnp.zeros((HEAD_DIM, nl), F32)

    def scores(u0, ntile):
        rows = ntile * SEL_TILE
        ka = kaug_ref[pl.ds(pl.multiple_of(u0 * SEL_TILE, SEL_TILE), rows), :]
        return _dot_nt(ka, qaug_ref[...])

    def fold(s, u0, ntile):
        m_old = m_ref[...]
        m_new = jnp.maximum(m_old, jnp.max(s, axis=0, keepdims=True))
        alpha = jnp.exp2(m_old - m_new)
        p = jnp.exp2(s - m_new)
        l_ref[...] = alpha * l_ref[...] + jnp.sum(p, axis=0, keepdims=True)
        p = p.astype(BF16)
        pv = group_dots(lambda g: vts_ref[g, u0], p[0:SEL_TILE])
        for t in range(1, ntile):
            pv = pv + group_dots(lambda g: vts_ref[g, u0 + t], p[t * SEL_TILE:(t + 1) * SEL_TILE])
        acc_ref[...] = alpha * acc_ref[...] + pv
        m_ref[...] = m_new

    tiles_per_q = SEL_TILE // Q_BLOCK
    u_last = qb // tiles_per_q
    par = qb % tiles_per_q
    n_far = jnp.maximum(u_last - 1, 0)
    n_pair = n_far // 2

    @pl.when(n_pair >= 1)
    def _():
        sa_ref[...] = scores(0, 2)

    def far(i, carry):
        sb_ref[...] = scores(4 * i + 2, 2)
        fold(sa_ref[...], 4 * i, 2)
        sa_ref[...] = scores(2 * jnp.minimum(2 * i + 2, n_pair - 1), 2)
        fold(sb_ref[...], 4 * i + 2, 2)
        return carry

    lax.fori_loop(0, n_pair // 2, far, 0)

    @pl.when(n_pair % 2 == 1)
    def _():
        fold(sa_ref[...], 2 * (n_pair - 1), 2)

    @pl.when(n_far % 2 == 1)
    def _():
        s = scores(u_last - 2, 3)
        s = jnp.concatenate([s[0:SEL_TILE], s[SEL_TILE:3 * SEL_TILE] + ts_ref[par]], axis=0)
        fold(s, u_last - 2, 3)

    @pl.when((n_far % 2 == 0) & (u_last >= 1))
    def _():
        fold(scores(u_last - 1, 2) + ts_ref[par], u_last - 1, 2)

    @pl.when(u_last == 0)
    def _():
        fold(scores(0, 1) + ts_ref[par, SEL_TILE:2 * SEL_TILE, :], 0, 1)

    o_s = acc_ref[...] / l_ref[...]

    def gate_row(br):
        rows = gt_ref[br * N_HEADS:(br + 1) * N_HEADS, :]
        return jnp.concatenate([rows[hh:hh + 1, :] for hh in range(N_HEADS)], axis=1)

    o_t = gate_row(0) * o_c + gate_row(1) * o_s + gate_row(2) * o_w
    stack = jnp.concatenate([o_t[:, hh * Q_BLOCK:(hh + 1) * Q_BLOCK] for hh in range(N_HEADS)], axis=0)
    o_ref[...] = stack.T.astype(o_ref.dtype)


def _nsa_prompt(qhm, kc, vct, kaug, vts, kwin, vtw, gt, tabs, *, bsz, seq):
    n = bsz * seq
    nqb = seq // Q_BLOCK
    n_cmp = seq // CMP_STRIDE
    nl = N_HEADS * Q_BLOCK
    tc, ts, tw = tabs
    in_specs = [
        pl.BlockSpec((N_HEADS, Q_BLOCK, LANE), lambda b, i: (0, b * nqb + i, 0)),
        pl.BlockSpec((None, n_cmp, LANE), lambda b, i: (b, 0, 0)),
        pl.BlockSpec((None, KV_HEADS, HEAD_DIM, n_cmp), lambda b, i: (b, 0, 0, 0)),
        pl.BlockSpec((seq, NJ + LANE), lambda b, i: (b, 0)),
        pl.BlockSpec((KV_HEADS, seq // SEL_TILE, HEAD_DIM, SEL_TILE), lambda b, i: (0, b, 0, 0)),
        pl.BlockSpec((seq, LANE), lambda b, i: (b, 0)),
        pl.BlockSpec((KV_HEADS, seq // LANE, HEAD_DIM, LANE), lambda b, i: (0, b, 0, 0)),
        pl.BlockSpec((32, Q_BLOCK), lambda b, i: (0, b * nqb + i)),
        pl.BlockSpec(tc.shape, lambda b, i: (0, 0)),
        pl.BlockSpec(ts.shape, lambda b, i: (0, 0, 0)),
        pl.BlockSpec(tw.shape, lambda b, i: (0, 0)),
    ]
    pad = 2 * (Q_BLOCK // CMP_STRIDE)
    return pl.pallas_call(
        functools.partial(_nsa_prompt_kernel, n_cmp=n_cmp),
        grid=(bsz, nqb),
        in_specs=in_specs,
        out_specs=pl.BlockSpec((Q_BLOCK, N_HEADS * HEAD_DIM), lambda b, i: (b * nqb + i, 0)),
        out_shape=jax.ShapeDtypeStruct((n, N_HEADS * HEAD_DIM), BF16),
        scratch_shapes=[pltpu.VMEM((pad + n_cmp + pad, nl), F32), pltpu.VMEM((KV_HEADS, 8 + n_cmp, Q_BLOCK), F32),
                        pltpu.VMEM((nl, NJ + LANE), BF16),
                        pltpu.VMEM((1, nl), F32), pltpu.VMEM((1, nl), F32), pltpu.VMEM((HEAD_DIM, nl), F32),
                        pltpu.VMEM((2 * SEL_TILE, nl), F32), pltpu.VMEM((2 * SEL_TILE, nl), F32)],
        compiler_params=_params(("parallel", "arbitrary")),
        name="nsa_prompt",
    )(qhm, kc, vct, kaug, vts, kwin, vtw, gt, tc, ts, tw)


def _row_softmax_parts(s, s_new):
    m = jnp.maximum(jnp.max(s, axis=1, keepdims=True), s_new)
    p = jnp.exp2(s - m)
    pn = jnp.exp2(s_new - m)
    return p, pn, jnp.sum(p, axis=1, keepdims=True) + pn


def _nsa_sample_cmp_kernel(h_ref, pek_ref, w1k_ref, pev_ref, w1v_ref, w2ka_ref, w2kb_ref, w2va_ref, w2vb_ref,
                           q_ref, tcs_ref, ovs_ref, oc_ref, idx_ref, *, n_sub, qblk):
    hh = h_ref[...]
    ak, av = _compress_finish(hh[:, 0:4 * CMP_HID], hh[:, 4 * CMP_HID:8 * CMP_HID],
                              pek_ref, w1k_ref, pev_ref, w1v_ref, w2ka_ref, w2kb_ref, n_sub)
    kc = (_dot(ak[0], w2ka_ref[...]) + _dot(ak[1], w2kb_ref[...])).astype(BF16)
    vc = (_dot(av[0], w2va_ref[...]) + _dot(av[1], w2vb_ref[...])).astype(BF16)
    q8 = q_ref[...].astype(BF16)
    s = _dot_nt(q8, kc) + tcs_ref[...]
    m = jnp.max(s, axis=1, keepdims=True)
    p = jnp.exp2(s - m)
    pc = p / jnp.sum(p, axis=1, keepdims=True)
    oc_ref[...] = _dot(pc.astype(BF16), vc)
    hrow = lax.broadcasted_iota(jnp.int32, (N_HEADS, n_sub), 0)
    rows = []
    for g in range(KV_HEADS):
        rows.append(jnp.sum(jnp.where(hrow // HPG == g, pc, 0.0), axis=0, keepdims=True))
    psum = jnp.concatenate(rows + [jnp.zeros((N_HEADS - KV_HEADS, n_sub), F32)], axis=0)
    imp = _dot_f32(psum, ovs_ref[...])
    jl = lax.broadcasted_iota(jnp.int32, imp.shape, 1)
    forced = (jl == 0) | (jl == qblk) | (jl == qblk - 1)
    score = jnp.where(forced, FORCE, jnp.where(jl <= qblk, imp, -FORCE))
    _, picked = _select_blocks(score, forced, jl, 1)
    lane = lax.broadcasted_iota(jnp.int32, (N_HEADS, LANE), 1)
    out = jnp.zeros((N_HEADS, LANE), jnp.int32)
    for r, jm in enumerate([0, qblk - 1, qblk] + picked):
        out = jnp.where(lane == r, jm, out)
    idx_ref[...] = out


def _nsa_sample_cmp(hbuf, q8, w, tcs, ovs, *, qblk):
    dbsz, n_sub, _ = hbuf.shape
    c2 = lambda b: (0, 0)
    names = ["pek", "w1k", "pev", "w1v", "w2ka", "w2kb", "w2va", "w2vb"]
    return pl.pallas_call(
        functools.partial(_nsa_sample_cmp_kernel, n_sub=n_sub, qblk=qblk),
        grid=(dbsz,),
        in_specs=[pl.BlockSpec((None, n_sub, 8 * CMP_HID), lambda b: (b, 0, 0))]
        + [pl.BlockSpec(w[k].shape, c2) for k in names]
        + [pl.BlockSpec((None, N_HEADS, LANE), lambda b: (b, 0, 0)),
           pl.BlockSpec(tcs.shape, c2), pl.BlockSpec(ovs.shape, c2)],
        out_specs=[pl.BlockSpec((None, N_HEADS, LANE), lambda b: (b, 0, 0)),
                   pl.BlockSpec((None, N_HEADS, LANE), lambda b: (b, 0, 0))],
        out_shape=[jax.ShapeDtypeStruct((dbsz, N_HEADS, LANE), F32),
                   jax.ShapeDtypeStruct((dbsz, N_HEADS, LANE), jnp.int32)],
        compiler_params=_params(("parallel",)),
        name="nsa_sample_cmp",
    )(hbuf, *[w[k] for k in names], q8, tcs, ovs)


def _nsa_sample_sel_kernel(idx_ref, pt_ref, *refs, qblk, n_win):
    nblk = KV_HEADS * N_SEL_BLOCKS
    blocks = refs[:nblk]
    (q_ref, slcn_ref, winn_ref, state_ref, gs_ref, oc_ref, tnear_ref, b0_ref, tws_ref,
     o_ref, wout_ref) = refs[nblk:]
    b = pl.program_id(0)
    q8f = q_ref[...]
    q8 = q8f.astype(BF16)
    hrow = lax.broadcasted_iota(jnp.int32, (N_HEADS, LANE), 0)
    b0 = b0_ref[:, 0:1]

    slcn = slcn_ref[0:1, :]
    s_new = jnp.sum(q8f * slcn[:, 0:LANE], axis=1, keepdims=True) + b0
    halves = PAGE // SEL_LEN
    lane = lax.broadcasted_iota(jnp.int32, (N_HEADS, PAGE), 1)
    o_sel = []
    for g in range(KV_HEADS):
        ss = []
        for r in range(N_SEL_BLOCKS):
            blk = idx_ref[b, g * N_SEL_BLOCKS + r]
            s = _dot(q8, blocks[g * N_SEL_BLOCKS + r][0].astype(BF16))
            near = jnp.where(blk // halves == qblk // halves - 1, tnear_ref[...], 0.0)
            ok = (blk < qblk) & (lane // SEL_LEN == blk % halves)
            ss.append(jnp.where(ok, s + near, NEG))
        s = jnp.concatenate(ss, axis=1)
        p, pn, l = _row_softmax_parts(s, s_new)
        o = pn * slcn[:, LANE:2 * LANE]
        for r in range(N_SEL_BLOCKS):
            pr = p[:, r * PAGE:(r + 1) * PAGE].astype(BF16)
            o = o + _dot_nt(pr, blocks[g * N_SEL_BLOCKS + r][1].astype(BF16))
        o_sel.append(o / l)
    o_s = jnp.where(hrow // HPG == 0, o_sel[0], o_sel[1])

    winn = winn_ref[0:1, :]
    sw = _dot(q8, state_ref[0].astype(BF16)) + tws_ref[...]
    sw_new = jnp.sum(q8f * winn[:, 0:LANE], axis=1, keepdims=True) + b0
    p, pn, l = _row_softmax_parts(sw, sw_new)
    o_w = (_dot_nt(p.astype(BF16), state_ref[1].astype(BF16)) + pn * winn[:, LANE:2 * LANE]) / l

    o_ref[...] = gs_ref[0] * oc_ref[...] + gs_ref[1] * o_s + gs_ref[2] * o_w
    wlane = lax.broadcasted_iota(jnp.int32, (LANE, n_win), 1)
    for kv in range(2):
        newcol = jnp.broadcast_to(winn[:, kv * LANE:(kv + 1) * LANE], (LANE, LANE)).T
        newcol = jnp.concatenate([newcol] * (n_win // LANE), axis=1)
        wout_ref[kv] = jnp.where(wlane == n_win - 1, newcol, pltpu.roll(state_ref[kv], n_win - 1, 1))


def _nsa_sample_sel(idx, page_table, slc_cache_t, q8, slc_new, win_new, state, gs, oc, tnear, b0, tws, *, qblk):
    dbsz, n_pages = page_table.shape
    n_win = state.shape[-1]
    halves = PAGE // SEL_LEN

    def blk_spec(k):
        def imap(b, idx_ref, pt_ref):
            j = idx_ref[b, k]
            return (pt_ref[b, jnp.minimum(j // halves, n_pages - 1)], 0, 0, 0)
        return pl.BlockSpec((None, 2, LANE, PAGE), imap)

    def per_b(shape):
        nd = len(shape)
        return pl.BlockSpec((None,) + tuple(shape[1:]), lambda b, i, p: (b,) + (0,) * (nd - 1))

    def const(shape):
        nd = len(shape)
        return pl.BlockSpec(tuple(shape), lambda b, i, p: (0,) * nd)

    nblk = KV_HEADS * N_SEL_BLOCKS
    grid_spec = pltpu.PrefetchScalarGridSpec(
        num_scalar_prefetch=2, grid=(dbsz,),
        in_specs=[blk_spec(k) for k in range(nblk)]
        + [per_b(q8.shape), per_b(slc_new.shape), per_b(win_new.shape), per_b(state.shape), per_b(gs.shape),
           per_b(oc.shape), const(tnear.shape), const(b0.shape), const(tws.shape)],
        out_specs=[pl.BlockSpec((N_HEADS, LANE), lambda b, i, p: (b, 0)),
                   pl.BlockSpec((None, 2, LANE, n_win), lambda b, i, p: (b, 0, 0, 0))])
    return pl.pallas_call(
        functools.partial(_nsa_sample_sel_kernel, qblk=qblk, n_win=n_win),
        grid_spec=grid_spec,
        out_shape=[jax.ShapeDtypeStruct((dbsz * N_HEADS, LANE), F32),
                   jax.ShapeDtypeStruct((dbsz, 2, LANE, n_win), F32)],
        compiler_params=_params(("arbitrary",)),
        name="nsa_sample_sel",
    )(idx, page_table, *([slc_cache_t] * nblk), q8, slc_new, win_new, state, gs, oc, tnear, b0, tws)


def _merge_kernel(x_ref, o_ref, u_ref, vn_ref, ga_ref, gb_ref, ws_ref, bs_ref, wbn_ref, wbg_ref, wout_ref,
                  lng_ref, x1_ref, *, tm):
    chunk = ws_ref.shape[1]
    keep = (lax.broadcasted_iota(jnp.int32, (chunk, chunk), 0)
            >= lax.broadcasted_iota(jnp.int32, (chunk, chunk), 1))
    vb = vn_ref[...].astype(BF16)
    cols = []
    for gg in range(GM_GROUPS):
        wt = jnp.where(keep, ws_ref[gg], 0.0).astype(BF16)
        rows = [_dot(wt, vb[c * chunk:(c + 1) * chunk, gg * LANE:(gg + 1) * LANE]) for c in range(tm // chunk)]
        cols.append(jnp.concatenate(rows, axis=0) if len(rows) > 1 else rows[0])
    s = jnp.concatenate(cols, axis=1) + jnp.concatenate([bs_ref[...]] * (tm // chunk), axis=0)
    o_gm = (u_ref[...].astype(F32) * s).astype(BF16)
    if len(wbn_ref.shape) == 2:
        a = _dot(o_ref[...].astype(BF16), wbn_ref[...])
    else:
        a = jnp.zeros((tm, D_MODEL), F32)
        for hh in range(N_HEADS):
            a += _dot(o_ref[pl.ds(hh, tm, stride=N_HEADS), :].astype(BF16), wbn_ref[hh])
    bm = _dot(o_gm, wbg_ref[...])
    mixed = (ga_ref[...].astype(F32) * a + gb_ref[...].astype(F32) * bm).astype(BF16)
    x1_ref[...] = x_ref[...] + _rms(_dot(mixed, wout_ref[...]), lng_ref[...])


def _merge(x, o_nsa, u, vn, ga, gb, ws, bs, wbn, w, *, tm, name):
    n = x.shape[0]
    row = lambda i: (i, 0)
    c2 = lambda i: (0, 0)
    c3 = lambda i: (0, 0, 0)
    ws_spec = pl.BlockSpec(ws.shape, c3)
    o_rows = o_nsa.shape[0] // n
    return pl.pallas_call(
        functools.partial(_merge_kernel, tm=tm),
        grid=(n // tm,),
        in_specs=[pl.BlockSpec((tm, D_MODEL), row), pl.BlockSpec((tm * o_rows, o_nsa.shape[1]), row),
                  pl.BlockSpec((tm, GM_DIM), row), pl.BlockSpec((tm, GM_DIM), row),
                  pl.BlockSpec((tm, D_MODEL), row), pl.BlockSpec((tm, D_MODEL), row),
                  ws_spec, pl.BlockSpec(bs.shape, c2),
                  pl.BlockSpec(wbn.shape, c2 if wbn.ndim == 2 else c3), pl.BlockSpec(w["wbg"].shape, c2),
                  pl.BlockSpec(w["wout"].shape, c2), pl.BlockSpec((1, D_MODEL), c2)],
        out_specs=pl.BlockSpec((tm, D_MODEL), row),
        out_shape=jax.ShapeDtypeStruct((n, D_MODEL), F32),
        compiler_params=_params(("parallel",)),
        name=name,
    )(x, o_nsa, u, vn, ga, gb, ws, bs, wbn, w["wbg"], w["wout"], w["ln_mix_post"])


def _ffn_kernel(x_ref, p_ref, lnf_ref, wg_ref, wu_ref, wd_ref, lnp_ref, wple_ref, wpg_ref, lne_ref, y_ref,
                h2_ref, acc_ref):
    j = pl.program_id(1)

    @pl.when(j == 0)
    def _():
        h2_ref[...] = _rms(x_ref[...], lnf_ref[...]).astype(BF16)
        acc_ref[...] = jnp.zeros(acc_ref.shape, F32)

    h2 = h2_ref[...]
    act = (jax.nn.silu(_dot(h2, wg_ref[...])) * _dot(h2, wu_ref[...])).astype(BF16)
    acc_ref[...] += _dot(act, wd_ref[...])

    @pl.when(j == pl.num_programs(1) - 1)
    def _():
        x2 = x_ref[...] + _rms(acc_ref[...], lnp_ref[...])
        e = _dot(p_ref[...].astype(BF16), wple_ref[...])
        gp = jax.nn.sigmoid(_dot(x2.astype(BF16), wpg_ref[...]))
        y_ref[...] = x2 + _rms(gp * e, lne_ref[...])


def _ffn(x1, ple, w, *, tm, name):
    n = x1.shape[0]
    row = lambda i, j: (i, 0)
    c2 = lambda i, j: (0, 0)
    return pl.pallas_call(
        _ffn_kernel,
        grid=(n // tm, D_FF // FF_CHUNK),
        in_specs=[pl.BlockSpec((tm, D_MODEL), row), pl.BlockSpec((tm, PLE_DIM), row),
                  pl.BlockSpec((1, D_MODEL), c2),
                  pl.BlockSpec((D_MODEL, FF_CHUNK), lambda i, j: (0, j)),
                  pl.BlockSpec((D_MODEL, FF_CHUNK), lambda i, j: (0, j)),
                  pl.BlockSpec((FF_CHUNK, D_MODEL), lambda i, j: (j, 0)),
                  pl.BlockSpec((1, D_MODEL), c2), pl.BlockSpec(w["wple"].shape, c2),
                  pl.BlockSpec(w["wpg"].shape, c2), pl.BlockSpec((1, D_MODEL), c2)],
        out_specs=pl.BlockSpec((tm, D_MODEL), row),
        out_shape=jax.ShapeDtypeStruct((n, D_MODEL), F32),
        scratch_shapes=[pltpu.VMEM((tm, D_MODEL), BF16), pltpu.VMEM((tm, D_MODEL), F32)],
        compiler_params=_params(("parallel", "arbitrary")),
        name=name,
    )(x1, ple, w["ln_ffn_pre"], w["wfg"], w["wfu"], w["wfd"], w["ln_ffn_post"], w["wple"], w["wpg"],
      w["ln_ple_post"])


def _prep_weights(i, ln_mix_pre, w_in, cmp_pe_k, cmp_w1_k, cmp_w2_k, cmp_pe_v, cmp_w1_v, cmp_w2_v, gm_ln_g,
                  gm_ln_b, w_branch_nsa, w_branch_gm, w_out, ln_mix_post, ln_ffn_pre, w_ffn_gate, w_ffn_up,
                  w_ffn_down, ln_ffn_post, w_ple, w_ple_gate, ln_ple_post):
    q_dim = N_HEADS * HEAD_DIM
    kv3 = 6 * KV_HEADS * HEAD_DIM
    n_gate = 3 * N_HEADS
    win = w_in[i]
    o = 0
    wq = win[:, o:o + q_dim]; o += q_dim
    wkv = win[:, o:o + kv3]; o += kv3
    wg = win[:, o:o + n_gate]; o += n_gate
    wuv = win[:, o:o + 2 * GM_DIM]; o += 2 * GM_DIM
    wmg = win[:, o:o + 2 * D_MODEL]
    wq4 = wq.reshape(D_MODEL, N_HEADS, 1, HEAD_DIM)
    half = (np.arange(N_HEADS) // HPG)[None, :, None, None] == np.arange(KV_HEADS)[None, None, :, None]
    wq_pad = jnp.where(half, wq4, 0.0).reshape(D_MODEL, N_HEADS * LANE)
    wg_pad = jnp.pad(wg, ((0, 0), (0, LANE - n_gate)))

    def blockdiag(w1):
        w1s = w1.reshape(CMP_LEN // CMP_STRIDE, CMP_STRIDE, HEAD_DIM, CMP_HID).transpose(1, 2, 0, 3)
        w1s = w1s.reshape(CMP_STRIDE, HEAD_DIM, 2 * CMP_HID)
        z = jnp.zeros_like(w1s)
        bd = jnp.concatenate([jnp.concatenate([w1s, z], axis=2), jnp.concatenate([z, w1s], axis=2)], axis=1)
        return bd.reshape(CMP_STRIDE // 2, 2 * 2 * HEAD_DIM, 4 * CMP_HID)

    wbn4 = w_branch_nsa[i].reshape(N_HEADS, 1, HEAD_DIM, D_MODEL)
    wbn_heads = jnp.where(half.reshape(N_HEADS, KV_HEADS, 1, 1), wbn4, 0.0).reshape(N_HEADS, LANE, D_MODEL)

    zk = jnp.zeros_like(cmp_w2_k[i])
    zv = jnp.zeros_like(cmp_w2_v[i])
    b = lambda a: a.astype(BF16)
    r = lambda a: a[i][None, :].astype(F32)
    return dict(
        ln_mix_pre=r(ln_mix_pre), wq=b(wq_pad), wkv=b(wkv), wg=b(wg_pad), wuv=b(wuv), wmg=b(wmg),
        gm_ln_g=r(gm_ln_g), gm_ln_b=r(gm_ln_b),
        wbk=b(blockdiag(cmp_w1_k[i])), wbv=b(blockdiag(cmp_w1_v[i])),
        pek=jnp.broadcast_to(cmp_pe_k[i].reshape(1, -1), (8, CMP_LEN * HEAD_DIM)).astype(F32), w1k=b(cmp_w1_k[i]),
        pev=jnp.broadcast_to(cmp_pe_v[i].reshape(1, -1), (8, CMP_LEN * HEAD_DIM)).astype(F32), w1v=b(cmp_w1_v[i]),
        w2ka=b(jnp.concatenate([cmp_w2_k[i], zk], axis=1)), w2kb=b(jnp.concatenate([zk, cmp_w2_k[i]], axis=1)),
        w2va=b(jnp.concatenate([cmp_w2_v[i], zv], axis=1)), w2vb=b(jnp.concatenate([zv, cmp_w2_v[i]], axis=1)),
        w2vt=b(cmp_w2_v[i].T),
        wbn=b(w_branch_nsa[i]), wbn_heads=b(wbn_heads), wbg=b(w_branch_gm[i]), wout=b(w_out[i]), ln_mix_post=r(ln_mix_post),
        ln_ffn_pre=r(ln_ffn_pre), wfg=b(w_ffn_gate[i]), wfu=b(w_ffn_up[i]), wfd=b(w_ffn_down[i]),
        ln_ffn_post=r(ln_ffn_post), wple=b(w_ple[i]), wpg=b(w_ple_gate[i]), ln_ple_post=r(ln_ple_post),
    )


def _gate_perm():
    return np.array([hg * 3 + br for br in range(3) for hg in range(N_HEADS)])


def _prompt_layer(x, ple, w, rel_bias, gm_ws, gm_bs):
    bsz, seq, _ = x.shape
    assert seq % SEL_TILE == 0 and seq // SEL_LEN <= NJ and seq >= WINDOW
    n = bsz * seq
    tm = 512
    xf = x.reshape(n, D_MODEL)
    wp = dict(w)
    perm = _gate_perm()
    wp["wg"] = jnp.concatenate([w["wg"][:, perm], w["wg"][:, len(perm):]], axis=1)
    (qhm, cmp, cmpt, slct, wint, kaug, vts, kwin, vtw, gt, u, vn, ga, gb) = _inproj(
        xf, wp, prompt=True, seq=seq, tm=tm)
    kc, vct = _compress_prompt(cmp, w, bsz=bsz, seq=seq)

    o_nsa = _nsa_prompt(qhm, kc, vct, kaug, vts, kwin, vtw, gt, _bias_tables(rel_bias), bsz=bsz, seq=seq)

    bs_tile = jnp.repeat(gm_bs.T, LANE, axis=1).astype(F32)
    x1 = _merge(xf, o_nsa, u, vn, ga, gb, gm_ws.astype(F32), bs_tile, w["wbn"], w, tm=512, name="merge_prompt")
    y = _ffn(x1, ple.reshape(n, PLE_DIM), w, tm=512, name="ffn_prompt")

    def rows_form(at):
        return at.reshape(bsz, 2, KV_HEADS, HEAD_DIM, at.shape[-1]).transpose(0, 4, 1, 2, 3)

    return (y.reshape(bsz, seq, D_MODEL), rows_form(cmpt), rows_form(slct),
            rows_form(wint[:, :, seq - min(WINDOW, seq):]))


def _sample_layer(x, ple, w, rel_bias, gm_ws, gm_bs, cache_cmp, cache_slc, win_buf, page_table):
    dbsz, nq, _ = x.shape
    assert nq == 1
    n_pool = cache_cmp.shape[0]
    n_pages = page_table.shape[1]
    past = n_pages * PAGE
    n_win = win_buf.shape[1]
    assert past % GM_CHUNK == 0 and n_win == WINDOW and past >= WINDOW and dbsz % 8 == 0
    xf = x.reshape(dbsz, D_MODEL)
    q, cmp, slc, win, gate, u, vn, ga, gb = _inproj(xf, w, prompt=False, seq=1, tm=dbsz)

    pps = min(32, n_pages)
    hbuf = _compress_paged(_pages_t(cache_cmp), page_table, w, pps=pps)
    n_sub = past // CMP_STRIDE
    qblk = past // SEL_LEN
    n_sel_pad = -(-(qblk + 1) // LANE) * LANE
    sh = _shifted_bias(rel_bias)
    i = np.arange(n_sub)
    d_c = past - (CMP_STRIDE * i + CMP_LEN - 1)
    tcs = _sample_table(sh, d_c, i < n_sub - 1)
    ovs = jnp.asarray(_overlap_t(n_sub, n_sel_pad).T)
    q8 = q.reshape(dbsz, N_HEADS, LANE)
    oc, idx = _nsa_sample_cmp(hbuf, q8, w, tcs, ovs, qblk=qblk)
    idx2 = idx[:, 0:KV_HEADS, 0:N_SEL_BLOCKS].reshape(dbsz, KV_HEADS * N_SEL_BLOCKS)

    l = np.arange(PAGE)
    tnear = _sample_table(sh, PAGE - l, l >= 0)
    b0 = jnp.broadcast_to(sh[0][:, None] * LOG2E, (N_HEADS, LANE)).astype(F32)
    kpos = np.arange(n_win)
    tws = _sample_table(sh, n_win - kpos, kpos >= 0)
    gs = gate[:, 0:3 * N_HEADS].reshape(dbsz, N_HEADS, 3).transpose(0, 2, 1)
    gs = jnp.broadcast_to(gs[..., None], (dbsz, 3, N_HEADS, LANE))
    o8, win_out = _nsa_sample_sel(
        idx2, page_table, _pages_t(cache_slc), q8,
        jnp.broadcast_to(slc[:, None, :], (dbsz, 8, 256)), jnp.broadcast_to(win[:, None, :], (dbsz, 8, 256)),
        _pages_t(win_buf), gs, oc, tnear, b0, tws, qblk=qblk)
    win_out = win_out.reshape(dbsz, 2, KV_HEADS, HEAD_DIM, n_win).transpose(0, 4, 1, 2, 3)

    ws_diag = gm_ws[:, 0, 0][:, None, None] * jnp.eye(dbsz, dtype=F32)[None]
    bs_tile = jnp.broadcast_to(jnp.repeat(gm_bs[:, 0], LANE)[None, :], (dbsz, GM_GROUPS * LANE)).astype(F32)
    x1 = _merge(xf, o8, u, vn, ga, gb, ws_diag, bs_tile, w["wbn_heads"], w, tm=dbsz, name="merge_sample")
    y = _ffn(x1, ple.reshape(dbsz, PLE_DIM), w, tm=dbsz, name="ffn_sample")
    kvshape = (dbsz, 1, 2, KV_HEADS, HEAD_DIM)
    return (y.reshape(dbsz, 1, D_MODEL), cmp.reshape(kvshape), slc.reshape(kvshape),
            win_out, vn.reshape(dbsz, 1, GM_DIM))


def kernel(x_prompt, x_sample, cache_cmp_kv, cache_slc_kv, state_win_kv, page_table, p_prompt, p_sample, rel_bias,
           ln_mix_pre, w_in, cmp_pe_k, cmp_w1_k, cmp_w2_k, cmp_pe_v, cmp_w1_v, cmp_w2_v, gm_ln_g, gm_ln_b, gm_ws,
           gm_bs, w_branch_nsa, w_branch_gm, w_out, ln_mix_post, ln_ffn_pre, w_ffn_gate, w_ffn_up, w_ffn_down,
           ln_ffn_post, w_ple, w_ple_gate, ln_ple_post):
    depth = w_in.shape[0]
    xp, xs = x_prompt, x_sample
    outs = [[] for _ in range(7)]
    for i in range(depth):
        w = _prep_weights(i, ln_mix_pre, w_in, cmp_pe_k, cmp_w1_k, cmp_w2_k, cmp_pe_v, cmp_w1_v, cmp_w2_v, gm_ln_g,
                          gm_ln_b, w_branch_nsa, w_branch_gm, w_out, ln_mix_post, ln_ffn_pre, w_ffn_gate, w_ffn_up,
                          w_ffn_down, ln_ffn_post, w_ple, w_ple_gate, ln_ple_post)
        xp, c_p, s_p, w_p = _prompt_layer(xp, p_prompt[i], w, rel_bias, gm_ws[i], gm_bs[i])
        xs, c_s, s_s, w_s, v_s = _sample_layer(xs, p_sample[i], w, rel_bias, gm_ws[i], gm_bs[i], cache_cmp_kv[i],
                                               cache_slc_kv[i], state_win_kv[i], page_table)
        for lst, val in zip(outs, (c_p, s_p, w_p, c_s, s_s, w_s, v_s)):
            lst.append(val)
    return (xp, xs) + tuple(jnp.stack(o) for o in outs)
```

```python
import functools
import math

import jax
import jax.numpy as jnp
import numpy as np
from jax import lax
from jax.experimental import pallas as pl
from jax.experimental.pallas import tpu as pltpu

F32 = jnp.float32
BF16 = jnp.bfloat16

D_MODEL = 1024
N_HEADS = 8
KV_HEADS = 2
HPG = N_HEADS // KV_HEADS
HEAD_DIM = 64
CMP_LEN = 32
CMP_STRIDE = 16
CMP_HID = 2 * HEAD_DIM
SEL_LEN = 64
N_SEL_BLOCKS = 16
WINDOW = 512
Q_BLOCK = 128
GM_GROUPS = 4
GM_CHUNK = 128
GM_DIM = D_MODEL // 2
D_FF = -(-8 * D_MODEL // (3 * 256)) * 256
PLE_DIM = 256
N_BUCKETS = 32
MAX_DISTANCE = 128
EPS = 1e-6
NEG = -1e30
LOG2E = 1.4426950408889634
FORCE = 1e6
PAGE = 128

LANE = 128
NJ = 128
SEL_TILE = 256
FOLD_LANES = 256
FF_CHUNK = D_FF // 2
VMEM_LIMIT = 56 * 1024 * 1024


def _dot(a, b):
    return jnp.dot(a, b, preferred_element_type=F32)


def _dot_nt(a, b):
    return lax.dot_general(a, b, (((1,), (1,)), ((), ())), preferred_element_type=F32)


def _dot_f32(a, b):
    return jnp.dot(a, b, preferred_element_type=F32, precision=lax.Precision.HIGHEST)


def _rms(x, g):
    return x * lax.rsqrt(jnp.mean(x * x, axis=-1, keepdims=True) + EPS) * g


def _params(sem):
    return pltpu.CompilerParams(dimension_semantics=sem, vmem_limit_bytes=VMEM_LIMIT)


def _t5_bucket_np(d):
    d = np.maximum(d, 0)
    max_exact = N_BUCKETS // 2
    ratio = (np.log(np.maximum(d, 1).astype(np.float32) / np.float32(max_exact))
             / np.float32(math.log(MAX_DISTANCE / max_exact)))
    large = np.minimum(max_exact + (ratio * np.float32(N_BUCKETS - max_exact)).astype(np.int32), N_BUCKETS - 1)
    return np.where(d < max_exact, d, large)


_BUCKET = _t5_bucket_np(np.arange(MAX_DISTANCE + 1))
assert _BUCKET[MAX_DISTANCE] == N_BUCKETS - 1


def _shifted_bias(rel_bias):
    return rel_bias[_BUCKET] - rel_bias[N_BUCKETS - 1][None, :]


def _sample_table(sh, dist, valid):
    t = sh[np.minimum(np.maximum(dist, 0), MAX_DISTANCE)] * LOG2E
    return jnp.where(valid[:, None], t, NEG).T.astype(F32)


_BUCKET_START = [int(np.argmax(_BUCKET >= k)) for k in range(N_BUCKETS)]
_BAND_ROWS = 3 * (Q_BLOCK // CMP_STRIDE)


def _bias_tables_kernel(rb_ref, tc_ref, ts_ref, tw_ref):
    def fill(store, rows, chunk, c0, stride, dmax):
        def body(c, carry):
            r0 = pl.multiple_of(c * chunk, chunk)
            r = r0 + lax.broadcasted_iota(jnp.int32, (chunk, Q_BLOCK), 0)
            d = c0 + lax.broadcasted_iota(jnp.int32, (chunk, Q_BLOCK), 1) - stride * r
            ok = (d >= 0) & (d <= dmax)
            dc = jnp.minimum(d, MAX_DISTANCE)
            for hd in range(N_HEADS):
                v = jnp.full((chunk, Q_BLOCK), rb_ref[0, hd], F32)
                for k in range(1, N_BUCKETS):
                    v = jnp.where(dc >= _BUCKET_START[k], rb_ref[k, hd], v)
                v = (v - rb_ref[N_BUCKETS - 1, hd]) * LOG2E
                store(r0, chunk, hd, jnp.where(ok, v, NEG))
            return carry
        lax.fori_loop(0, rows // chunk, body, 0)

    def lanes(hd):
        return slice(hd * Q_BLOCK, (hd + 1) * Q_BLOCK)

    def st_c(r0, n, hd, v):
        tc_ref[pl.ds(r0, n), lanes(hd)] = v

    def st_w(r0, n, hd, v):
        tw_ref[pl.ds(r0, n), lanes(hd)] = v

    big = 1 << 30
    fill(st_c, _BAND_ROWS, 8, 2 * Q_BLOCK - (CMP_LEN - 1), CMP_STRIDE, big)
    for par in range(SEL_TILE // Q_BLOCK):
        def st_s(r0, n, hd, v, par=par):
            ts_ref[par, pl.ds(r0, n), lanes(hd)] = v
        fill(st_s, 2 * SEL_TILE, 64, SEL_TILE + Q_BLOCK * par, 1, big)
    fill(st_w, WINDOW + Q_BLOCK, 64, WINDOW, 1, WINDOW)


def _bias_tables(rel_bias):
    nl = N_HEADS * Q_BLOCK
    return pl.pallas_call(
        _bias_tables_kernel,
        in_specs=[pl.BlockSpec(memory_space=pltpu.SMEM)],
        out_shape=[jax.ShapeDtypeStruct((_BAND_ROWS, nl), F32),
                   jax.ShapeDtypeStruct((SEL_TILE // Q_BLOCK, 2 * SEL_TILE, nl), F32),
                   jax.ShapeDtypeStruct((WINDOW + Q_BLOCK, nl), F32)],
        compiler_params=pltpu.CompilerParams(vmem_limit_bytes=VMEM_LIMIT),
        name="bias_tables",
    )(rel_bias.astype(F32))


def _overlap_t(n_cmp, n_sel_pad):
    ci = np.arange(n_cmp)[None, :] * CMP_STRIDE
    sj = np.arange(n_sel_pad)[:, None] * SEL_LEN
    return ((ci <= sj + SEL_LEN - 1) & (ci + CMP_LEN - 1 >= sj)).astype(np.float32)


def _overlap_offsets():
    per = SEL_LEN // CMP_STRIDE
    ov = _overlap_t(16 * per, 16)
    jj, ii = np.nonzero(ov)
    offs = sorted({int(i - per * j) for j, i in zip(jj, ii)})
    band = np.isin(np.arange(16 * per)[None, :] - per * np.arange(16)[:, None], offs)
    assert (band == (ov > 0)).all() and min(offs) >= -8
    return offs


_OVERLAP_OFFSETS = _overlap_offsets()


def _inproj_kernel(x_ref, lng_ref, wq_ref, wkv_ref, wg_ref, wuv_ref, wmg_ref, glg_ref, glb_ref, *outs,
                   prompt, seq, tm):
    h = _rms(x_ref[...], lng_ref[...]).astype(BF16)
    q = _dot(h, wq_ref[...]) * (HEAD_DIM ** -0.5 * LOG2E)
    kv = _dot(h, wkv_ref[...])
    gate = jax.nn.sigmoid(_dot(h, wg_ref[...]))
    uv = jax.nn.gelu(_dot(h, wuv_ref[...]))
    u = uv[:, :GM_DIM]
    v = uv[:, GM_DIM:]
    mu = jnp.mean(v, axis=-1, keepdims=True)
    var = jnp.mean(jnp.square(v - mu), axis=-1, keepdims=True)
    vn = (v - mu) * lax.rsqrt(var + EPS) * glg_ref[...] + glb_ref[...]
    mg = jax.nn.sigmoid(_dot(h, wmg_ref[...]))
    cmp, slc, win = kv[:, 0:256], kv[:, 256:512], kv[:, 512:768]
    if prompt:
        (q_ref, cmp_ref, cmpt_ref, slct_ref, wint_ref, kaug_ref, vts_ref, kwin_ref, vtw_ref, gt_ref,
         u_ref, vn_ref, ga_ref, gb_ref) = outs
        for hh in range(N_HEADS):
            q_ref[hh] = q[:, hh * LANE:(hh + 1) * LANE].astype(BF16)
        pos = (pl.program_id(0) * tm + lax.broadcasted_iota(jnp.int32, (tm, NJ), 0)) % seq
        onehot = (lax.broadcasted_iota(jnp.int32, (tm, NJ), 1) == pos // SEL_LEN)
        kaug_ref[:, 0:NJ] = onehot.astype(BF16)
        kaug_ref[:, NJ:NJ + LANE] = slc[:, 0:LANE].astype(BF16)
        kwin_ref[...] = win[:, 0:LANE].astype(BF16)
        slc_t = slc.T
        win_t = win.T
        for g in range(KV_HEADS):
            r0 = LANE + HEAD_DIM * g
            for c in range(tm // SEL_TILE):
                vts_ref[g, c] = slc_t[r0:r0 + HEAD_DIM, c * SEL_TILE:(c + 1) * SEL_TILE].astype(BF16)
            for c in range(tm // LANE):
                vtw_ref[g, c] = win_t[r0:r0 + HEAD_DIM, c * LANE:(c + 1) * LANE].astype(BF16)
        gt_ref[...] = gate.T[0:32, :]
        cmpt_ref[...] = cmp.T
        slct_ref[...] = slc_t
        wint_ref[...] = win_t
    else:
        q_ref, cmp_ref, slc_ref, win_ref, gate_ref, u_ref, vn_ref, ga_ref, gb_ref = outs
        q_ref[...] = q
        gate_ref[...] = gate
        slc_ref[...] = slc
        win_ref[...] = win
    cmp_ref[...] = cmp
    u_ref[...] = u.astype(u_ref.dtype)
    vn_ref[...] = vn.astype(vn_ref.dtype)
    ga_ref[...] = mg[:, :D_MODEL].astype(ga_ref.dtype)
    gb_ref[...] = mg[:, D_MODEL:].astype(gb_ref.dtype)


def _inproj(x, w, *, prompt, seq, tm):
    n = x.shape[0]
    row = lambda i: (i, 0)
    const = lambda i: (0, 0)
    in_specs = [pl.BlockSpec((tm, D_MODEL), row), pl.BlockSpec((1, D_MODEL), const),
                pl.BlockSpec(w["wq"].shape, const), pl.BlockSpec(w["wkv"].shape, const),
                pl.BlockSpec(w["wg"].shape, const), pl.BlockSpec(w["wuv"].shape, const),
                pl.BlockSpec(w["wmg"].shape, const), pl.BlockSpec((1, GM_DIM), const),
                pl.BlockSpec((1, GM_DIM), const)]
    kv_shapes = [jax.ShapeDtypeStruct((n, 256), F32)] * 3
    kv_specs = [pl.BlockSpec((tm, 256), row)] * 3
    if prompt:
        act = BF16
        tiles = seq // tm
        kvt_shapes = [jax.ShapeDtypeStruct((n // seq, 256, seq), F32)] * 3
        kvt_specs = [pl.BlockSpec((None, 256, tm), lambda i: (i // tiles, 0, i % tiles))] * 3
        out_shape = ([jax.ShapeDtypeStruct((N_HEADS, n, LANE), BF16)] + kv_shapes[:1] + kvt_shapes + [
            jax.ShapeDtypeStruct((n, NJ + LANE), BF16),
            jax.ShapeDtypeStruct((KV_HEADS, n // SEL_TILE, HEAD_DIM, SEL_TILE), BF16),
            jax.ShapeDtypeStruct((n, LANE), BF16),
            jax.ShapeDtypeStruct((KV_HEADS, n // LANE, HEAD_DIM, LANE), BF16),
            jax.ShapeDtypeStruct((32, n), F32)])
        out_specs = ([pl.BlockSpec((N_HEADS, tm, LANE), lambda i: (0, i, 0))] + kv_specs[:1] + kvt_specs + [
            pl.BlockSpec((tm, NJ + LANE), row),
            pl.BlockSpec((KV_HEADS, tm // SEL_TILE, HEAD_DIM, SEL_TILE), lambda i: (0, i, 0, 0)),
            pl.BlockSpec((tm, LANE), row),
            pl.BlockSpec((KV_HEADS, tm // LANE, HEAD_DIM, LANE), lambda i: (0, i, 0, 0)),
            pl.BlockSpec((32, tm), lambda i: (0, i))])
    else:
        act = F32
        out_shape = ([jax.ShapeDtypeStruct((n, N_HEADS * LANE), F32)] + kv_shapes
                     + [jax.ShapeDtypeStruct((n, LANE), F32)])
        out_specs = [pl.BlockSpec((tm, N_HEADS * LANE), row)] + kv_specs + [pl.BlockSpec((tm, LANE), row)]
    out_shape += [jax.ShapeDtypeStruct((n, GM_DIM), act)] * 2 + [jax.ShapeDtypeStruct((n, D_MODEL), act)] * 2
    out_specs += [pl.BlockSpec((tm, GM_DIM), row)] * 2 + [pl.BlockSpec((tm, D_MODEL), row)] * 2
    return pl.pallas_call(
        functools.partial(_inproj_kernel, prompt=prompt, seq=seq, tm=tm),
        grid=(n // tm,), in_specs=in_specs, out_specs=out_specs, out_shape=out_shape,
        compiler_params=_params(("parallel",)),
        name="inproj_prompt" if prompt else "inproj_sample",
    )(x, w["ln_mix_pre"], w["wq"], w["wkv"], w["wg"], w["wuv"], w["wmg"], w["gm_ln_g"], w["gm_ln_b"])


def _row_pair(x_ref, r, n_sub):
    return jnp.concatenate([x_ref[pl.ds(r, n_sub, stride=CMP_STRIDE), :],
                            x_ref[pl.ds(r + 1, n_sub, stride=CMP_STRIDE), :]], axis=1).astype(BF16)


def _compress_finish(hk, hv, pek_ref, w1k_ref, pev_ref, w1v_ref, w2ka_ref, w2kb_ref, n_sub):
    pwk = _dot(pek_ref[...].astype(BF16), w1k_ref[...])[0:1, :]
    pwv = _dot(pev_ref[...].astype(BF16), w1v_ref[...])[0:1, :]

    def act(hh, g, pw):
        a0 = hh[:, 256 * g:256 * g + CMP_HID]
        a1 = hh[:, 256 * g + CMP_HID:256 * g + 2 * CMP_HID]
        return jax.nn.silu(a0 + pltpu.roll(a1, n_sub - 1, 0) + pw).astype(BF16)

    return [act(hk, 0, pwk), act(hk, 1, pwk)], [act(hv, 0, pwv), act(hv, 1, pwv)]


def _compress_prompt_kernel(cmpk_ref, cmpv_ref, wbk_ref, wbv_ref, pek_ref, w1k_ref, pev_ref, w1v_ref,
                            w2ka_ref, w2kb_ref, w2vt_ref, kc_ref, vct_ref, *, n_sub):
    hk = jnp.zeros((n_sub, 4 * CMP_HID), F32)
    hv = jnp.zeros((n_sub, 4 * CMP_HID), F32)
    for r in range(0, CMP_STRIDE, 2):
        hk += _dot(_row_pair(cmpk_ref, r, n_sub), wbk_ref[r // 2])
        hv += _dot(_row_pair(cmpv_ref, r, n_sub), wbv_ref[r // 2])
    ak, av = _compress_finish(hk, hv, pek_ref, w1k_ref, pev_ref, w1v_ref, w2ka_ref, w2kb_ref, n_sub)
    kc_ref[...] = (_dot(ak[0], w2ka_ref[...]) + _dot(ak[1], w2kb_ref[...])).astype(BF16)
    for g in range(KV_HEADS):
        vct_ref[g] = _dot_nt(w2vt_ref[...], av[g]).astype(BF16)


def _compress_prompt(cmp, w, *, bsz, seq):
    n_sub = seq // CMP_STRIDE
    const2 = lambda b: (0, 0)
    const3 = lambda b: (0, 0, 0)
    return pl.pallas_call(
        functools.partial(_compress_prompt_kernel, n_sub=n_sub),
        grid=(bsz,),
        in_specs=[pl.BlockSpec((seq, LANE), lambda b: (b, 0)), pl.BlockSpec((seq, LANE), lambda b: (b, 1)),
                  pl.BlockSpec(w["wbk"].shape, const3), pl.BlockSpec(w["wbv"].shape, const3),
                  pl.BlockSpec(w["pek"].shape, const2), pl.BlockSpec(w["w1k"].shape, const2),
                  pl.BlockSpec(w["pev"].shape, const2), pl.BlockSpec(w["w1v"].shape, const2),
                  pl.BlockSpec(w["w2ka"].shape, const2), pl.BlockSpec(w["w2kb"].shape, const2),
                  pl.BlockSpec(w["w2vt"].shape, const2)],
        out_specs=[pl.BlockSpec((None, n_sub, LANE), lambda b: (b, 0, 0)),
                   pl.BlockSpec((None, KV_HEADS, HEAD_DIM, n_sub), lambda b: (b, 0, 0, 0))],
        out_shape=[jax.ShapeDtypeStruct((bsz, n_sub, LANE), BF16),
                   jax.ShapeDtypeStruct((bsz, KV_HEADS, HEAD_DIM, n_sub), BF16)],
        compiler_params=_params(("parallel",)),
        name="compress_prompt",
    )(cmp, cmp, w["wbk"], w["wbv"], w["pek"], w["w1k"], w["pev"], w["w1v"], w["w2ka"], w["w2kb"], w["w2vt"])


def _compress_paged_kernel(pt_ref, *refs, pps):
    kpages, vpages = refs[:pps], refs[pps:2 * pps]
    wbk_ref, wbv_ref, h_ref, xk_ref, xv_ref = refs[2 * pps:]
    sub_per_page = PAGE // CMP_STRIDE
    n = pps * sub_per_page
    for k in range(pps):
        xk_ref[k * PAGE:(k + 1) * PAGE, :] = kpages[k][...].T
        xv_ref[k * PAGE:(k + 1) * PAGE, :] = vpages[k][...].T
    hk = jnp.zeros((n, 4 * CMP_HID), F32)
    hv = jnp.zeros((n, 4 * CMP_HID), F32)
    for r in range(0, CMP_STRIDE, 2):
        hk += _dot(_row_pair(xk_ref, r, n), wbk_ref[r // 2])
        hv += _dot(_row_pair(xv_ref, r, n), wbv_ref[r // 2])
    h_ref[:, 0:4 * CMP_HID] = hk
    h_ref[:, 4 * CMP_HID:8 * CMP_HID] = hv


def _compress_paged(cache_t, page_table, w, *, pps):
    dbsz, n_pages = page_table.shape
    sub_per_page = PAGE // CMP_STRIDE
    n_sub = n_pages * sub_per_page

    def page_spec(k, c):
        return pl.BlockSpec((None, None, LANE, PAGE), lambda b, t, pt: (pt[b, t * pps + k], c, 0, 0))

    const3 = lambda b, t, pt: (0, 0, 0)
    grid_spec = pltpu.PrefetchScalarGridSpec(
        num_scalar_prefetch=1, grid=(dbsz, n_pages // pps),
        in_specs=[page_spec(k, c) for c in range(2) for k in range(pps)]
        + [pl.BlockSpec(w["wbk"].shape, const3), pl.BlockSpec(w["wbv"].shape, const3)],
        out_specs=pl.BlockSpec((None, pps * sub_per_page, 8 * CMP_HID), lambda b, t, pt: (b, t, 0)),
        scratch_shapes=[pltpu.VMEM((pps * PAGE, LANE), F32), pltpu.VMEM((pps * PAGE, LANE), F32)])
    return pl.pallas_call(
        functools.partial(_compress_paged_kernel, pps=pps),
        grid_spec=grid_spec,
        out_shape=jax.ShapeDtypeStruct((dbsz, n_sub, 8 * CMP_HID), F32),
        compiler_params=_params(("parallel", "arbitrary")),
        name="compress_paged",
    )(page_table, *([cache_t] * (2 * pps)), w["wbk"], w["wbv"])


def _pages_t(cache):
    n, npos = cache.shape[0], cache.shape[1]
    return cache.transpose(0, 2, 3, 4, 1).reshape(n, 2, KV_HEADS * HEAD_DIM, npos)


def _col_softmax(s):
    p, inv = _col_softmax_parts(s)
    return p * inv


def _col_softmax_parts(s):
    m = jnp.max(s, axis=0, keepdims=True)
    p = jnp.exp2(s - m)
    return p, jnp.where(m > 0.5 * NEG, 1.0 / jnp.sum(p, axis=0, keepdims=True), 0.0)


N_FORCED = 3


def _select_blocks(score, forced, jio, axis):
    selneg = jnp.where(forced, 0.0, NEG)
    score = jnp.where(forced, -3e38, score)
    picked = []
    for _ in range(N_SEL_BLOCKS - N_FORCED):
        mx = jnp.max(score, axis=axis, keepdims=True)
        jm = jnp.min(jnp.where(score == mx, jio, 1 << 20), axis=axis, keepdims=True)
        pick = jio == jm
        selneg = jnp.where(pick, 0.0, selneg)
        score = jnp.where(pick, -3e38, score)
        picked.append(jm)
    return selneg, picked


def _nsa_prompt_kernel(q_ref, kc_ref, vct_ref, kaug_ref, vts_ref, kwin_ref, vtw_ref, gt_ref,
                       tc_ref, ts_ref, tw_ref, o_ref,
                       sc_ref, ps_ref, qaug_ref, m_ref, l_ref, acc_ref, sa_ref, sb_ref, *, n_cmp):
    qb = pl.program_id(1)
    gl = HPG * Q_BLOCK
    nl = KV_HEADS * gl
    q2 = q_ref[...].reshape(nl, LANE)

    def group_dots(vt, p):
        return jnp.concatenate([_dot(vt(g), p[:, g * gl:(g + 1) * gl]) for g in range(KV_HEADS)], axis=1)

    pad = 2 * (Q_BLOCK // CMP_STRIDE)
    band = 3 * (Q_BLOCK // CMP_STRIDE)
    sc_ref[0:pad, :] = jnp.zeros((pad, nl), F32)
    sc_ref[pad:pad + n_cmp, :] = _dot_nt(kc_ref[...], q2)
    w0 = pl.multiple_of(qb * (Q_BLOCK // CMP_STRIDE), 8)
    sc_ref[pl.ds(w0, band), :] += tc_ref[...]
    sc = sc_ref[pad:pad + n_cmp, :]
    row = lax.broadcasted_iota(jnp.int32, (n_cmp, nl), 0)
    sc = jnp.where(row < (qb + 1) * (Q_BLOCK // CMP_STRIDE), sc, NEG)
    pc = _col_softmax(sc)
    o_c = group_dots(lambda g: vct_ref[g], pc.astype(BF16))

    n_wt = WINDOW // Q_BLOCK + 1
    tix = [jnp.maximum(qb - (n_wt - 1) + t, 0) for t in range(n_wt)]
    kw = jnp.concatenate([kwin_ref[pl.ds(pl.multiple_of(t * Q_BLOCK, Q_BLOCK), Q_BLOCK), :] for t in tix], axis=0)
    sw = _dot_nt(kw, q2) + tw_ref[...]
    wrow = lax.broadcasted_iota(jnp.int32, (n_wt * Q_BLOCK, nl), 0) + (qb - (n_wt - 1)) * Q_BLOCK
    pw, winv = _col_softmax_parts(jnp.where(wrow >= 0, sw, NEG))
    o_w = group_dots(lambda g: jnp.concatenate([vtw_ref[g, t] for t in tix], axis=1), pw.astype(BF16)) * winv

    per_sel = SEL_LEN // CMP_STRIDE
    n_sel = n_cmp // per_sel
    jio = lax.broadcasted_iota(jnp.int32, (NJ, Q_BLOCK), 0)
    qi = lax.broadcasted_iota(jnp.int32, (NJ, Q_BLOCK), 1)
    qblk = qb * (Q_BLOCK // SEL_LEN) + qi // SEL_LEN
    forced = (jio == 0) | (jio == qblk) | (jio == qblk - 1)
    for g in range(KV_HEADS):
        psum = pc[:, g * gl:g * gl + Q_BLOCK]
        for hh in range(1, HPG):
            psum = psum + pc[:, g * gl + hh * Q_BLOCK:g * gl + (hh + 1) * Q_BLOCK]
        ps_ref[g, 0:8, :] = jnp.zeros((8, Q_BLOCK), F32)
        ps_ref[g, 8:8 + n_cmp, :] = psum
        imp_t = ps_ref[g, pl.ds(8 + _OVERLAP_OFFSETS[0], n_sel, stride=per_sel), :]
        for off in _OVERLAP_OFFSETS[1:]:
            imp_t = imp_t + ps_ref[g, pl.ds(8 + off, n_sel, stride=per_sel), :]
        if n_sel < NJ:
            imp_t = jnp.concatenate([imp_t, jnp.zeros((NJ - n_sel, Q_BLOCK), F32)], axis=0)
        score = jnp.where(forced, FORCE, jnp.where(jio <= qblk, imp_t, -FORCE))
        selneg_t, _ = _select_blocks(score, forced, jio, 0)
        selneg = selneg_t.T.astype(BF16)
        for hh in range(HPG):
            qaug_ref[g * gl + hh * Q_BLOCK:g * gl + (hh + 1) * Q_BLOCK, 0:NJ] = selneg
    qaug_ref[:, NJ:NJ + LANE] = q2

    m_ref[...] = jnp.full((1, nl), -1e38, F32)
    l_ref[...] = jnp.zeros((1, nl), F32)
    acc_ref[...] = jnp.zeros((HEAD_DIM, nl), F32)

    def scores(u0, ntile):
        rows = ntile * SEL_TILE
        ka = kaug_ref[pl.ds(pl.multiple_of(u0 * SEL_TILE, SEL_TILE), rows), :]
        return _dot_nt(ka, qaug_ref[...])

    def fold(s, u0, ntile):
        for c in range(nl // FOLD_LANES):
            ls = slice(c * FOLD_LANES, (c + 1) * FOLD_LANES)
            g = c * FOLD_LANES // gl
            sl = s[:, ls]
            m_old = m_ref[:, ls]
            m_new = jnp.maximum(m_old, jnp.max(sl, axis=0, keepdims=True))
            alpha = jnp.exp2(m_old - m_new)
            p = jnp.exp2(sl - m_new)
            l_ref[:, ls] = alpha * l_ref[:, ls] + jnp.sum(p, axis=0, keepdims=True)
            p = p.astype(BF16)
            pv = _dot(vts_ref[g, u0], p[0:SEL_TILE])
            for t in range(1, ntile):
                pv = pv + _dot(vts_ref[g, u0 + t], p[t * SEL_TILE:(t + 1) * SEL_TILE])
            acc_ref[:, ls] = alpha * acc_ref[:, ls] + pv
            m_ref[:, ls] = m_new

    tiles_per_q = SEL_TILE // Q_BLOCK
    u_last = qb // tiles_per_q
    par = qb % tiles_per_q
    n_far = jnp.maximum(u_last - 1, 0)
    n_pair = n_far // 2

    @pl.when(n_pair >= 1)
    def _():
        sa_ref[...] = scores(0, 2)

    def far(i, carry):
        sb_ref[...] = scores(4 * i + 2, 2)
        fold(sa_ref[...], 4 * i, 2)
        sa_ref[...] = scores(2 * jnp.minimum(2 * i + 2, n_pair - 1), 2)
        fold(sb_ref[...], 4 * i + 2, 2)
        return carry

    lax.fori_loop(0, n_pair // 2, far, 0)

    @pl.when(n_pair % 2 == 1)
    def _():
        fold(sa_ref[...], 2 * (n_pair - 1), 2)

    @pl.when(n_far % 2 == 1)
    def _():
        s = scores(u_last - 2, 3)
        s = jnp.concatenate([s[0:SEL_TILE], s[SEL_TILE:3 * SEL_TILE] + ts_ref[par]], axis=0)
        fold(s, u_last - 2, 3)

    @pl.when((n_far % 2 == 0) & (u_last >= 1))
    def _():
        fold(scores(u_last - 1, 2) + ts_ref[par], u_last - 1, 2)

    @pl.when(u_last == 0)
    def _():
        fold(scores(0, 1) + ts_ref[par, SEL_TILE:2 * SEL_TILE, :], 0, 1)

    o_s = acc_ref[...] / l_ref[...]

    def gate_row(br):
        rows = gt_ref[br * N_HEADS:(br + 1) * N_HEADS, :]
        return jnp.concatenate([rows[hh:hh + 1, :] for hh in range(N_HEADS)], axis=1)

    o_t = gate_row(0) * o_c + gate_row(1) * o_s + gate_row(2) * o_w
    stack = jnp.concatenate([o_t[:, hh * Q_BLOCK:(hh + 1) * Q_BLOCK] for hh in range(N_HEADS)], axis=0)
    o_ref[...] = stack.T.astype(o_ref.dtype)


def _nsa_prompt(qhm, kc, vct, kaug, vts, kwin, vtw, gt, tabs, *, bsz, seq):
    n = bsz * seq
    nqb = seq // Q_BLOCK
    n_cmp = seq // CMP_STRIDE
    nl = N_HEADS * Q_BLOCK
    tc, ts, tw = tabs
    in_specs = [
        pl.BlockSpec((N_HEADS, Q_BLOCK, LANE), lambda b, i: (0, b * nqb + i, 0)),
        pl.BlockSpec((None, n_cmp, LANE), lambda b, i: (b, 0, 0)),
        pl.BlockSpec((None, KV_HEADS, HEAD_DIM, n_cmp), lambda b, i: (b, 0, 0, 0)),
        pl.BlockSpec((seq, NJ + LANE), lambda b, i: (b, 0)),
        pl.BlockSpec((KV_HEADS, seq // SEL_TILE, HEAD_DIM, SEL_TILE), lambda b, i: (0, b, 0, 0)),
        pl.BlockSpec((seq, LANE), lambda b, i: (b, 0)),
        pl.BlockSpec((KV_HEADS, seq // LANE, HEAD_DIM, LANE), lambda b, i: (0, b, 0, 0)),
        pl.BlockSpec((32, Q_BLOCK), lambda b, i: (0, b * nqb + i)),
        pl.BlockSpec(tc.shape, lambda b, i: (0, 0)),
        pl.BlockSpec(ts.shape, lambda b, i: (0, 0, 0)),
        pl.BlockSpec(tw.shape, lambda b, i: (0, 0)),
    ]
    pad = 2 * (Q_BLOCK // CMP_STRIDE)
    return pl.pallas_call(
        functools.partial(_nsa_prompt_kernel, n_cmp=n_cmp),
        grid=(bsz, nqb),
        in_specs=in_specs,
        out_specs=pl.BlockSpec((Q_BLOCK, N_HEADS * HEAD_DIM), lambda b, i: (b * nqb + i, 0)),
        out_shape=jax.ShapeDtypeStruct((n, N_HEADS * HEAD_DIM), BF16),
        scratch_shapes=[pltpu.VMEM((pad + n_cmp + pad, nl), F32), pltpu.VMEM((KV_HEADS, 8 + n_cmp, Q_BLOCK), F32),
                        pltpu.VMEM((nl, NJ + LANE), BF16),
                        pltpu.VMEM((1, nl), F32), pltpu.VMEM((1, nl), F32), pltpu.VMEM((HEAD_DIM, nl), F32),
                        pltpu.VMEM((2 * SEL_TILE, nl), F32), pltpu.VMEM((2 * SEL_TILE, nl), F32)],
        compiler_params=_params(("parallel", "arbitrary")),
        name="nsa_prompt",
    )(qhm, kc, vct, kaug, vts, kwin, vtw, gt, tc, ts, tw)


def _row_softmax_parts(s, s_new):
    m = jnp.maximum(jnp.max(s, axis=1, keepdims=True), s_new)
    p = jnp.exp2(s - m)
    pn = jnp.exp2(s_new - m)
    return p, pn, jnp.sum(p, axis=1, keepdims=True) + pn


def _nsa_sample_cmp_kernel(h_ref, pek_ref, w1k_ref, pev_ref, w1v_ref, w2ka_ref, w2kb_ref, w2va_ref, w2vb_ref,
                           q_ref, tcs_ref, ovs_ref, oc_ref, idx_ref, *, n_sub, qblk):
    hh = h_ref[...]
    ak, av = _compress_finish(hh[:, 0:4 * CMP_HID], hh[:, 4 * CMP_HID:8 * CMP_HID],
                              pek_ref, w1k_ref, pev_ref, w1v_ref, w2ka_ref, w2kb_ref, n_sub)
    kc = (_dot(ak[0], w2ka_ref[...]) + _dot(ak[1], w2kb_ref[...])).astype(BF16)
    vc = (_dot(av[0], w2va_ref[...]) + _dot(av[1], w2vb_ref[...])).astype(BF16)
    q8 = q_ref[...].astype(BF16)
    s = _dot_nt(q8, kc) + tcs_ref[...]
    m = jnp.max(s, axis=1, keepdims=True)
    p = jnp.exp2(s - m)
    pc = p / jnp.sum(p, axis=1, keepdims=True)
    oc_ref[...] = _dot(pc.astype(BF16), vc)
    hrow = lax.broadcasted_iota(jnp.int32, (N_HEADS, n_sub), 0)
    rows = []
    for g in range(KV_HEADS):
        rows.append(jnp.sum(jnp.where(hrow // HPG == g, pc, 0.0), axis=0, keepdims=True))
    psum = jnp.concatenate(rows + [jnp.zeros((N_HEADS - KV_HEADS, n_sub), F32)], axis=0)
    imp = _dot_f32(psum, ovs_ref[...])
    jl = lax.broadcasted_iota(jnp.int32, imp.shape, 1)
    forced = (jl == 0) | (jl == qblk) | (jl == qblk - 1)
    score = jnp.where(forced, FORCE, jnp.where(jl <= qblk, imp, -FORCE))
    _, picked = _select_blocks(score, forced, jl, 1)
    lane = lax.broadcasted_iota(jnp.int32, (N_HEADS, LANE), 1)
    out = jnp.zeros((N_HEADS, LANE), jnp.int32)
    for r, jm in enumerate([0, qblk - 1, qblk] + picked):
        out = jnp.where(lane == r, jm, out)
    idx_ref[...] = out


def _nsa_sample_cmp(hbuf, q8, w, tcs, ovs, *, qblk):
    dbsz, n_sub, _ = hbuf.shape
    c2 = lambda b: (0, 0)
    names = ["pek", "w1k", "pev", "w1v", "w2ka", "w2kb", "w2va", "w2vb"]
    return pl.pallas_call(
        functools.partial(_nsa_sample_cmp_kernel, n_sub=n_sub, qblk=qblk),
        grid=(dbsz,),
        in_specs=[pl.BlockSpec((None, n_sub, 8 * CMP_HID), lambda b: (b, 0, 0))]
        + [pl.BlockSpec(w[k].shape, c2) for k in names]
        + [pl.BlockSpec((None, N_HEADS, LANE), lambda b: (b, 0, 0)),
           pl.BlockSpec(tcs.shape, c2), pl.BlockSpec(ovs.shape, c2)],
        out_specs=[pl.BlockSpec((None, N_HEADS, LANE), lambda b: (b, 0, 0)),
                   pl.BlockSpec((None, N_HEADS, LANE), lambda b: (b, 0, 0))],
        out_shape=[jax.ShapeDtypeStruct((dbsz, N_HEADS, LANE), F32),
                   jax.ShapeDtypeStruct((dbsz, N_HEADS, LANE), jnp.int32)],
        compiler_params=_params(("parallel",)),
        name="nsa_sample_cmp",
    )(hbuf, *[w[k] for k in names], q8, tcs, ovs)


def _nsa_sample_sel_kernel(idx_ref, pt_ref, *refs, qblk, n_win):
    nblk = KV_HEADS * N_SEL_BLOCKS
    blocks = refs[:nblk]
    (q_ref, slcn_ref, winn_ref, state_ref, gs_ref, oc_ref, tnear_ref, b0_ref, tws_ref,
     o_ref, wout_ref) = refs[nblk:]
    b = pl.program_id(0)
    q8f = q_ref[...]
    q8 = q8f.astype(BF16)
    hrow = lax.broadcasted_iota(jnp.int32, (N_HEADS, LANE), 0)
    b0 = b0_ref[:, 0:1]

    slcn = slcn_ref[0:1, :]
    s_new = jnp.sum(q8f * slcn[:, 0:LANE], axis=1, keepdims=True) + b0
    halves = PAGE // SEL_LEN
    lane = lax.broadcasted_iota(jnp.int32, (N_HEADS, PAGE), 1)
    o_sel = []
    for g in range(KV_HEADS):
        ss = []
        for r in range(N_SEL_BLOCKS):
            blk = idx_ref[b, g * N_SEL_BLOCKS + r]
            s = _dot(q8, blocks[g * N_SEL_BLOCKS + r][0].astype(BF16))
            near = jnp.where(blk // halves == qblk // halves - 1, tnear_ref[...], 0.0)
            ok = (blk < qblk) & (lane // SEL_LEN == blk % halves)
            ss.append(jnp.where(ok, s + near, NEG))
        s = jnp.concatenate(ss, axis=1)
        p, pn, l = _row_softmax_parts(s, s_new)
        o = pn * slcn[:, LANE:2 * LANE]
        for r in range(N_SEL_BLOCKS):
            pr = p[:, r * PAGE:(r + 1) * PAGE].astype(BF16)
            o = o + _dot_nt(pr, blocks[g * N_SEL_BLOCKS + r][1].astype(BF16))
        o_sel.append(o / l)
    o_s = jnp.where(hrow // HPG == 0, o_sel[0], o_sel[1])

    winn = winn_ref[0:1, :]
    sw = _dot(q8, state_ref[0].astype(BF16)) + tws_ref[...]
    sw_new = jnp.sum(q8f * winn[:, 0:LANE], axis=1, keepdims=True) + b0
    p, pn, l = _row_softmax_parts(sw, sw_new)
    o_w = (_dot_nt(p.astype(BF16), state_ref[1].astype(BF16)) + pn * winn[:, LANE:2 * LANE]) / l

    o_ref[...] = gs_ref[0] * oc_ref[...] + gs_ref[1] * o_s + gs_ref[2] * o_w
    wlane = lax.broadcasted_iota(jnp.int32, (LANE, n_win), 1)
    for kv in range(2):
        newcol = jnp.broadcast_to(winn[:, kv * LANE:(kv + 1) * LANE], (LANE, LANE)).T
        newcol = jnp.concatenate([newcol] * (n_win // LANE), axis=1)
        wout_ref[kv] = jnp.where(wlane == n_win - 1, newcol, pltpu.roll(state_ref[kv], n_win - 1, 1))


def _nsa_sample_sel(idx, page_table, slc_cache_t, q8, slc_new, win_new, state, gs, oc, tnear, b0, tws, *, qblk):
    dbsz, n_pages = page_table.shape
    n_win = state.shape[-1]
    halves = PAGE // SEL_LEN

    def blk_spec(k):
        def imap(b, idx_ref, pt_ref):
            j = idx_ref[b, k]
            return (pt_ref[b, jnp.minimum(j // halves, n_pages - 1)], 0, 0, 0)
        return pl.BlockSpec((None, 2, LANE, PAGE), imap)

    def per_b(shape):
        nd = len(shape)
        return pl.BlockSpec((None,) + tuple(shape[1:]), lambda b, i, p: (b,) + (0,) * (nd - 1))

    def const(shape):
        nd = len(shape)
        return pl.BlockSpec(tuple(shape), lambda b, i, p: (0,) * nd)

    nblk = KV_HEADS * N_SEL_BLOCKS
    grid_spec = pltpu.PrefetchScalarGridSpec(
        num_scalar_prefetch=2, grid=(dbsz,),
        in_specs=[blk_spec(k) for k in range(nblk)]
        + [per_b(q8.shape), per_b(slc_new.shape), per_b(win_new.shape), per_b(state.shape), per_b(gs.shape),
           per_b(oc.shape), const(tnear.shape), const(b0.shape), const(tws.shape)],
        out_specs=[pl.BlockSpec((N_HEADS, LANE), lambda b, i, p: (b, 0)),
                   pl.BlockSpec((None, 2, LANE, n_win), lambda b, i, p: (b, 0, 0, 0))])
    return pl.pallas_call(
        functools.partial(_nsa_sample_sel_kernel, qblk=qblk, n_win=n_win),
        grid_spec=grid_spec,
        out_shape=[jax.ShapeDtypeStruct((dbsz * N_HEADS, LANE), F32),
                   jax.ShapeDtypeStruct((dbsz, 2, LANE, n_win), F32)],
        compiler_params=_params(("arbitrary",)),
        name="nsa_sample_sel",
    )(idx, page_table, *([slc_cache_t] * nblk), q8, slc_new, win_new, state, gs, oc, tnear, b0, tws)


def _merge_kernel(x_ref, o_ref, u_ref, vn_ref, ga_ref, gb_ref, ws_ref, bs_ref, wbn_ref, wbg_ref, wout_ref,
                  lng_ref, x1_ref, *, tm):
    chunk = ws_ref.shape[1]
    keep = (lax.broadcasted_iota(jnp.int32, (chunk, chunk), 0)
            >= lax.broadcasted_iota(jnp.int32, (chunk, chunk), 1))
    vb = vn_ref[...].astype(BF16)
    cols = []
    for gg in range(GM_GROUPS):
        wt = jnp.where(keep, ws_ref[gg], 0.0).astype(BF16)
        rows = [_dot(wt, vb[c * chunk:(c + 1) * chunk, gg * LANE:(gg + 1) * LANE]) for c in range(tm // chunk)]
        cols.append(jnp.concatenate(rows, axis=0) if len(rows) > 1 else rows[0])
    s = jnp.concatenate(cols, axis=1) + jnp.concatenate([bs_ref[...]] * (tm // chunk), axis=0)
    o_gm = (u_ref[...].astype(F32) * s).astype(BF16)
    if len(wbn_ref.shape) == 2:
        a = _dot(o_ref[...].astype(BF16), wbn_ref[...])
    else:
        a = jnp.zeros((tm, D_MODEL), F32)
        for hh in range(N_HEADS):
            a += _dot(o_ref[pl.ds(hh, tm, stride=N_HEADS), :].astype(BF16), wbn_ref[hh])
    bm = _dot(o_gm, wbg_ref[...])
    mixed = (ga_ref[...].astype(F32) * a + gb_ref[...].astype(F32) * bm).astype(BF16)
    x1_ref[...] = x_ref[...] + _rms(_dot(mixed, wout_ref[...]), lng_ref[...])


def _merge(x, o_nsa, u, vn, ga, gb, ws, bs, wbn, w, *, tm, name):
    n = x.shape[0]
    row = lambda i: (i, 0)
    c2 = lambda i: (0, 0)
    c3 = lambda i: (0, 0, 0)
    ws_spec = pl.BlockSpec(ws.shape, c3)
    o_rows = o_nsa.shape[0] // n
    return pl.pallas_call(
        functools.partial(_merge_kernel, tm=tm),
        grid=(n // tm,),
        in_specs=[pl.BlockSpec((tm, D_MODEL), row), pl.BlockSpec((tm * o_rows, o_nsa.shape[1]), row),
                  pl.BlockSpec((tm, GM_DIM), row), pl.BlockSpec((tm, GM_DIM), row),
                  pl.BlockSpec((tm, D_MODEL), row), pl.BlockSpec((tm, D_MODEL), row),
                  ws_spec, pl.BlockSpec(bs.shape, c2),
                  pl.BlockSpec(wbn.shape, c2 if wbn.ndim == 2 else c3), pl.BlockSpec(w["wbg"].shape, c2),
                  pl.BlockSpec(w["wout"].shape, c2), pl.BlockSpec((1, D_MODEL), c2)],
        out_specs=pl.BlockSpec((tm, D_MODEL), row),
        out_shape=jax.ShapeDtypeStruct((n, D_MODEL), F32),
        compiler_params=_params(("parallel",)),
        name=name,
    )(x, o_nsa, u, vn, ga, gb, ws, bs, wbn, w["wbg"], w["wout"], w["ln_mix_post"])


def _ffn_kernel(x_ref, p_ref, lnf_ref, wg_ref, wu_ref, wd_ref, lnp_ref, wple_ref, wpg_ref, lne_ref, y_ref,
                h2_ref, acc_ref):
    j = pl.program_id(1)

    @pl.when(j == 0)
    def _():
        h2_ref[...] = _rms(x_ref[...], lnf_ref[...]).astype(BF16)
        acc_ref[...] = jnp.zeros(acc_ref.shape, F32)

    h2 = h2_ref[...]
    act = (jax.nn.silu(_dot(h2, wg_ref[...])) * _dot(h2, wu_ref[...])).astype(BF16)
    acc_ref[...] += _dot(act, wd_ref[...])

    @pl.when(j == pl.num_programs(1) - 1)
    def _():
        x2 = x_ref[...] + _rms(acc_ref[...], lnp_ref[...])
        e = _dot(p_ref[...].astype(BF16), wple_ref[...])
        gp = jax.nn.sigmoid(_dot(x2.astype(BF16), wpg_ref[...]))
        y_ref[...] = x2 + _rms(gp * e, lne_ref[...])


def _ffn(x1, ple, w, *, tm, name):
    n = x1.shape[0]
    row = lambda i, j: (i, 0)
    c2 = lambda i, j: (0, 0)
    return pl.pallas_call(
        _ffn_kernel,
        grid=(n // tm, D_FF // FF_CHUNK),
        in_specs=[pl.BlockSpec((tm, D_MODEL), row), pl.BlockSpec((tm, PLE_DIM), row),
                  pl.BlockSpec((1, D_MODEL), c2),
                  pl.BlockSpec((D_MODEL, FF_CHUNK), lambda i, j: (0, j)),
                  pl.BlockSpec((D_MODEL, FF_CHUNK), lambda i, j: (0, j)),
                  pl.BlockSpec((FF_CHUNK, D_MODEL), lambda i, j: (j, 0)),
                  pl.BlockSpec((1, D_MODEL), c2), pl.BlockSpec(w["wple"].shape, c2),
                  pl.BlockSpec(w["wpg"].shape, c2), pl.BlockSpec((1, D_MODEL), c2)],
        out_specs=pl.BlockSpec((tm, D_MODEL), row),
        out_shape=jax.ShapeDtypeStruct((n, D_MODEL), F32),
        scratch_shapes=[pltpu.VMEM((tm, D_MODEL), BF16), pltpu.VMEM((tm, D_MODEL), F32)],
        compiler_params=_params(("parallel", "arbitrary")),
        name=name,
    )(x1, ple, w["ln_ffn_pre"], w["wfg"], w["wfu"], w["wfd"], w["ln_ffn_post"], w["wple"], w["wpg"],
      w["ln_ple_post"])


def _prep_weights(i, ln_mix_pre, w_in, cmp_pe_k, cmp_w1_k, cmp_w2_k, cmp_pe_v, cmp_w1_v, cmp_w2_v, gm_ln_g,
                  gm_ln_b, w_branch_nsa, w_branch_gm, w_out, ln_mix_post, ln_ffn_pre, w_ffn_gate, w_ffn_up,
                  w_ffn_down, ln_ffn_post, w_ple, w_ple_gate, ln_ple_post):
    q_dim = N_HEADS * HEAD_DIM
    kv3 = 6 * KV_HEADS * HEAD_DIM
    n_gate = 3 * N_HEADS
    win = w_in[i]
    o = 0
    wq = win[:, o:o + q_dim]; o += q_dim
    wkv = win[:, o:o + kv3]; o += kv3
    wg = win[:, o:o + n_gate]; o += n_gate
    wuv = win[:, o:o + 2 * GM_DIM]; o += 2 * GM_DIM
    wmg = win[:, o:o + 2 * D_MODEL]
    wq4 = wq.reshape(D_MODEL, N_HEADS, 1, HEAD_DIM)
    half = (np.arange(N_HEADS) // HPG)[None, :, None, None] == np.arange(KV_HEADS)[None, None, :, None]
    wq_pad = jnp.where(half, wq4, 0.0).reshape(D_MODEL, N_HEADS * LANE)
    wg_pad = jnp.pad(wg, ((0, 0), (0, LANE - n_gate)))

    def blockdiag(w1):
        w1s = w1.reshape(CMP_LEN // CMP_STRIDE, CMP_STRIDE, HEAD_DIM, CMP_HID).transpose(1, 2, 0, 3)
        w1s = w1s.reshape(CMP_STRIDE, HEAD_DIM, 2 * CMP_HID)
        z = jnp.zeros_like(w1s)
        bd = jnp.concatenate([jnp.concatenate([w1s, z], axis=2), jnp.concatenate([z, w1s], axis=2)], axis=1)
        return bd.reshape(CMP_STRIDE // 2, 2 * 2 * HEAD_DIM, 4 * CMP_HID)

    wbn4 = w_branch_nsa[i].reshape(N_HEADS, 1, HEAD_DIM, D_MODEL)
    wbn_heads = jnp.where(half.reshape(N_HEADS, KV_HEADS, 1, 1), wbn4, 0.0).reshape(N_HEADS, LANE, D_MODEL)

    zk = jnp.zeros_like(cmp_w2_k[i])
    zv = jnp.zeros_like(cmp_w2_v[i])
    b = lambda a: a.astype(BF16)
    r = lambda a: a[i][None, :].astype(F32)
    return dict(
        ln_mix_pre=r(ln_mix_pre), wq=b(wq_pad), wkv=b(wkv), wg=b(wg_pad), wuv=b(wuv), wmg=b(wmg),
        gm_ln_g=r(gm_ln_g), gm_ln_b=r(gm_ln_b),
        wbk=b(blockdiag(cmp_w1_k[i])), wbv=b(blockdiag(cmp_w1_v[i])),
        pek=jnp.broadcast_to(cmp_pe_k[i].reshape(1, -1), (8, CMP_LEN * HEAD_DIM)).astype(F32), w1k=b(cmp_w1_k[i]),
        pev=jnp.broadcast_to(cmp_pe_v[i].reshape(1, -1), (8, CMP_LEN * HEAD_DIM)).astype(F32), w1v=b(cmp_w1_v[i]),
        w2ka=b(jnp.concatenate([cmp_w2_k[i], zk], axis=1)), w2kb=b(jnp.concatenate([zk, cmp_w2_k[i]], axis=1)),
        w2va=b(jnp.concatenate([cmp_w2_v[i], zv], axis=1)), w2vb=b(jnp.concatenate([zv, cmp_w2_v[i]], axis=1)),
        w2vt=b(cmp_w2_v[i].T),
        wbn=b(w_branch_nsa[i]), wbn_heads=b(wbn_heads), wbg=b(w_branch_gm[i]), wout=b(w_out[i]), ln_mix_post=r(ln_mix_post),
        ln_ffn_pre=r(ln_ffn_pre), wfg=b(w_ffn_gate[i]), wfu=b(w_ffn_up[i]), wfd=b(w_ffn_down[i]),
        ln_ffn_post=r(ln_ffn_post), wple=b(w_ple[i]), wpg=b(w_ple_gate[i]), ln_ple_post=r(ln_ple_post),
    )


def _gate_perm():
    return np.array([hg * 3 + br for br in range(3) for hg in range(N_HEADS)])


def _prompt_layer(x, ple, w, rel_bias, gm_ws, gm_bs):
    bsz, seq, _ = x.shape
    assert seq % SEL_TILE == 0 and seq // SEL_LEN <= NJ and seq >= WINDOW
    n = bsz * seq
    tm = 512
    xf = x.reshape(n, D_MODEL)
    wp = dict(w)
    perm = _gate_perm()
    wp["wg"] = jnp.concatenate([w["wg"][:, perm], w["wg"][:, len(perm):]], axis=1)
    (qhm, cmp, cmpt, slct, wint, kaug, vts, kwin, vtw, gt, u, vn, ga, gb) = _inproj(
        xf, wp, prompt=True, seq=seq, tm=tm)
    kc, vct = _compress_prompt(cmp, w, bsz=bsz, seq=seq)

    o_nsa = _nsa_prompt(qhm, kc, vct, kaug, vts, kwin, vtw, gt, _bias_tables(rel_bias), bsz=bsz, seq=seq)

    bs_tile = jnp.repeat(gm_bs.T, LANE, axis=1).astype(F32)
    x1 = _merge(xf, o_nsa, u, vn, ga, gb, gm_ws.astype(F32), bs_tile, w["wbn"], w, tm=512, name="merge_prompt")
    y = _ffn(x1, ple.reshape(n, PLE_DIM), w, tm=512, name="ffn_prompt")

    def rows_form(at):
        return at.reshape(bsz, 2, KV_HEADS, HEAD_DIM, at.shape[-1]).transpose(0, 4, 1, 2, 3)

    return (y.reshape(bsz, seq, D_MODEL), rows_form(cmpt), rows_form(slct),
            rows_form(wint[:, :, seq - min(WINDOW, seq):]))


def _sample_layer(x, ple, w, rel_bias, gm_ws, gm_bs, cache_cmp, cache_slc, win_buf, page_table):
    dbsz, nq, _ = x.shape
    assert nq == 1
    n_pool = cache_cmp.shape[0]
    n_pages = page_table.shape[1]
    past = n_pages * PAGE
    n_win = win_buf.shape[1]
    assert past % GM_CHUNK == 0 and n_win == WINDOW and past >= WINDOW and dbsz % 8 == 0
    xf = x.reshape(dbsz, D_MODEL)
    q, cmp, slc, win, gate, u, vn, ga, gb = _inproj(xf, w, prompt=False, seq=1, tm=dbsz)

    pps = min(32, n_pages)
    hbuf = _compress_paged(_pages_t(cache_cmp), page_table, w, pps=pps)
    n_sub = past // CMP_STRIDE
    qblk = past // SEL_LEN
    n_sel_pad = -(-(qblk + 1) // LANE) * LANE
    sh = _shifted_bias(rel_bias)
    i = np.arange(n_sub)
    d_c = past - (CMP_STRIDE * i + CMP_LEN - 1)
    tcs = _sample_table(sh, d_c, i < n_sub - 1)
    ovs = jnp.asarray(_overlap_t(n_sub, n_sel_pad).T)
    q8 = q.reshape(dbsz, N_HEADS, LANE)
    oc, idx = _nsa_sample_cmp(hbuf, q8, w, tcs, ovs, qblk=qblk)
    idx2 = idx[:, 0:KV_HEADS, 0:N_SEL_BLOCKS].reshape(dbsz, KV_HEADS * N_SEL_BLOCKS)

    l = np.arange(PAGE)
    tnear = _sample_table(sh, PAGE - l, l >= 0)
    b0 = jnp.broadcast_to(sh[0][:, None] * LOG2E, (N_HEADS, LANE)).astype(F32)
    kpos = np.arange(n_win)
    tws = _sample_table(sh, n_win - kpos, kpos >= 0)
    gs = gate[:, 0:3 * N_HEADS].reshape(dbsz, N_HEADS, 3).transpose(0, 2, 1)
    gs = jnp.broadcast_to(gs[..., None], (dbsz, 3, N_HEADS, LANE))
    o8, win_out = _nsa_sample_sel(
        idx2, page_table, _pages_t(cache_slc), q8,
        jnp.broadcast_to(slc[:, None, :], (dbsz, 8, 256)), jnp.broadcast_to(win[:, None, :], (dbsz, 8, 256)),
        _pages_t(win_buf), gs, oc, tnear, b0, tws, qblk=qblk)
    win_out = win_out.reshape(dbsz, 2, KV_HEADS, HEAD_DIM, n_win).transpose(0, 4, 1, 2, 3)

    ws_diag = gm_ws[:, 0, 0][:, None, None] * jnp.eye(dbsz, dtype=F32)[None]
    bs_tile = jnp.broadcast_to(jnp.repeat(gm_bs[:, 0], LANE)[None, :], (dbsz, GM_GROUPS * LANE)).astype(F32)
    x1 = _merge(xf, o8, u, vn, ga, gb, ws_diag, bs_tile, w["wbn_heads"], w, tm=dbsz, name="merge_sample")
    y = _ffn(x1, ple.reshape(dbsz, PLE_DIM), w, tm=dbsz, name="ffn_sample")
    kvshape = (dbsz, 1, 2, KV_HEADS, HEAD_DIM)
    return (y.reshape(dbsz, 1, D_MODEL), cmp.reshape(kvshape), slc.reshape(kvshape),
            win_out, vn.reshape(dbsz, 1, GM_DIM))


def kernel(x_prompt, x_sample, cache_cmp_kv, cache_slc_kv, state_win_kv, page_table, p_prompt, p_sample, rel_bias,
           ln_mix_pre, w_in, cmp_pe_k, cmp_w1_k, cmp_w2_k, cmp_pe_v, cmp_w1_v, cmp_w2_v, gm_ln_g, gm_ln_b, gm_ws,
           gm_bs, w_branch_nsa, w_branch_gm, w_out, ln_mix_post, ln_ffn_pre, w_ffn_gate, w_ffn_up, w_ffn_down,
           ln_ffn_post, w_ple, w_ple_gate, ln_ple_post):
    depth = w_in.shape[0]
    xp, xs = x_prompt, x_sample
    outs = [[] for _ in range(7)]
    for i in range(depth):
        w = _prep_weights(i, ln_mix_pre, w_in, cmp_pe_k, cmp_w1_k, cmp_w2_k, cmp_pe_v, cmp_w1_v, cmp_w2_v, gm_ln_g,
                          gm_ln_b, w_branch_nsa, w_branch_gm, w_out, ln_mix_post, ln_ffn_pre, w_ffn_gate, w_ffn_up,
                          w_ffn_down, ln_ffn_post, w_ple, w_ple_gate, ln_ple_post)
        xp, c_p, s_p, w_p = _prompt_layer(xp, p_prompt[i], w, rel_bias, gm_ws[i], gm_bs[i])
        xs, c_s, s_s, w_s, v_s = _sample_layer(xs, p_sample[i], w, rel_bias, gm_ws[i], gm_bs[i], cache_cmp_kv[i],
                                               cache_slc_kv[i], state_win_kv[i], page_table)
        for lst, val in zip(outs, (c_p, s_p, w_p, c_s, s_s, w_s, v_s)):
            lst.append(val)
    return (xp, xs) + tuple(jnp.stack(o) for o in outs)
```
